```python
import math
import jax, jax.numpy as jnp
from jax import lax
import numpy as np

D_MODEL = 2048
BATCH = 8
SEQ = 2048
DEPTH = 4

N_MIXERS = 3
N_POOL_LAYERS = (DEPTH + 2) // 3
N_DIL_LAYERS = (DEPTH + 1) // 3
N_CONV_LAYERS = DEPTH // 3

POOL_WINDOWS = (2, 4, 8, 16)
POOL_GROUPS = len(POOL_WINDOWS)
POOL_GROUP_DIM = D_MODEL // POOL_GROUPS

DIL_CONFIGS = ((128, 1), (512, 4), (2048, 16))
N_DIL_GROUPS = len(DIL_CONFIGS)
HEAD_DIM = 64
DIL_HEADS = 8
DIL_WIDTH = DIL_HEADS * HEAD_DIM
DIL_QKV_COLS = N_DIL_GROUPS * 3 * DIL_WIDTH
DIL_BLOCK = 128

N_BUCKETS = 32
MAX_DISTANCE = 2048
N_BIAS_HEADS = N_DIL_GROUPS * DIL_HEADS

CONV_WIDTH = 3

N_EXPERTS = 64
TOP_K = 8
N_EXPERT_GROUPS = 8
TOPK_GROUPS = 4
EXPERT_DIM = 128
SHARED_DIM = 256
ROUTED_SCALE = 2.5
ROW_BLOCK = 256

ALPHA = (2.0 * DEPTH) ** 0.25
BETA = (8.0 * DEPTH) ** -0.25
LN_EPS = 1e-5

kernel_name = "hybrid_pool_dilattn_shortconv_moe_deepnorm"


def layer_norm(x, g, b):
    xf = x.astype(jnp.float32)
    mu = jnp.mean(xf, axis=-1, keepdims=True)
    var = jnp.mean(jnp.square(xf - mu), axis=-1, keepdims=True)
    y = (xf - mu) * lax.rsqrt(var + LN_EPS)
    return (y * g.astype(jnp.float32) + b.astype(jnp.float32)).astype(x.dtype)


def trailing_mean_minus_self(u, w):
    s = u.shape[1]
    cs = jnp.cumsum(u, axis=1)
    prev = jnp.pad(cs, ((0, 0), (w, 0), (0, 0)))[:, :s]
    cnt = jnp.minimum(jnp.arange(1, s + 1), w).astype(jnp.float32)
    return (cs - prev) / cnt[None, :, None] - u


def pool_mixer(x, w_in, w_group, scale, w_out):
    bn, s, _ = x.shape
    u = (x @ w_in).astype(jnp.float32).reshape(bn, s, POOL_GROUPS, POOL_GROUP_DIM)
    pooled = jnp.stack([trailing_mean_minus_self(u[:, :, g], w)
                        for g, w in enumerate(POOL_WINDOWS)], axis=2)
    mixed = jnp.einsum('bsgc,gcd->bsgd', pooled.astype(x.dtype), w_group).reshape(bn, s, D_MODEL)
    return (mixed * scale) @ w_out


def t5_bucket(dist):
    max_exact = N_BUCKETS // 2
    is_small = dist < max_exact
    distf = jnp.maximum(dist, 1).astype(jnp.float32)
    large = max_exact + (jnp.log(distf / max_exact) / math.log(MAX_DISTANCE / max_exact)
                         * (N_BUCKETS - max_exact)).astype(jnp.int32)
    large = jnp.minimum(large, N_BUCKETS - 1)
    return jnp.where(is_small, dist, large)


def dilated_group(q, k, v, bias_tab, window, dil):
    bn, s, h, e = q.shape
    steps = window // dil
    span = DIL_BLOCK * dil
    s_pad = -(-s // span) * span
    nb = s_pad // span
    pad = ((0, 0), (0, s_pad - s), (0, 0), (0, 0))

    def blocks(t):
        return jnp.pad(t, pad).reshape(bn, nb, DIL_BLOCK, dil, h, e)

    def with_prev(t):
        prev = jnp.pad(t[:, :-1], ((0, 0), (1, 0), (0, 0), (0, 0), (0, 0), (0, 0)))
        return jnp.concatenate([prev, t], axis=2)

    qb = blocks(q)
    kk = with_prev(blocks(k))
    vv = with_prev(blocks(v))
    q_loc = jnp.arange(DIL_BLOCK)[:, None]
    k_loc = jnp.arange(2 * DIL_BLOCK)[None, :]
    rel = q_loc + DIL_BLOCK - k_loc
    band = (rel >= 0) & (rel <= steps)
    not_before_start = (jnp.arange(nb)[:, None, None] > 0) | (k_loc >= DIL_BLOCK)[None]
    mask = band[None] & not_before_start
    bias = bias_tab[t5_bucket(jnp.maximum(rel, 0) * dil)]
    bias = jnp.transpose(bias, (2, 0, 1)).astype(jnp.float32)
    logits = jnp.einsum('bnqrhe,bnkrhe->bnrhqk', qb, kk,
                        preferred_element_type=jnp.float32) * (e ** -0.5) + bias
    logits = jnp.where(mask[None, :, None, None], logits, -jnp.inf)
    lse = jax.nn.logsumexp(logits, axis=-1)
    p = jnp.exp(logits - lse[..., None])
    o = jnp.einsum('bnrhqk,bnkrhe->bnqrhe', p.astype(v.dtype), vv)
    o = o.reshape(bn, s_pad, h, e)[:, :s]
    lse = jnp.transpose(lse, (0, 1, 4, 2, 3)).reshape(bn, s_pad, h)[:, :s]
    return o, lse


def dilated_mixer(x, w_qkv, w_out, rel_bias):
    bn, s, _ = x.shape
    qkv = (x @ w_qkv).reshape(bn, s, N_DIL_GROUPS, 3, DIL_HEADS, HEAD_DIM)
    outs, lses = [], []
    for g, (win, dil) in enumerate(DIL_CONFIGS):
        o, l = dilated_group(qkv[:, :, g, 0], qkv[:, :, g, 1], qkv[:, :, g, 2],
                             rel_bias[:, g * DIL_HEADS:(g + 1) * DIL_HEADS], win, dil)
        outs.append(o)
        lses.append(l)
    wts = jax.nn.softmax(jnp.stack(lses, axis=0), axis=0)
    o = jnp.einsum('gbsh,gbshe->bshe', wts,
                   jnp.stack(outs, axis=0).astype(jnp.float32)).astype(x.dtype)
    return o.reshape(bn, s, DIL_WIDTH) @ w_out


def conv_mixer(x, w_in, conv_w, w_out):
    b_gate, c_gate, h = jnp.split(x @ w_in, 3, axis=-1)
    u = c_gate * h
    y = lax.conv_general_dilated(u, conv_w[:, None, :], window_strides=(1,),
                                 padding=[(CONV_WIDTH - 1, 0)],
                                 dimension_numbers=('NWC', 'WIO', 'NWC'),
                                 feature_group_count=u.shape[-1])
    return (b_gate * y) @ w_out


def moe(x2d, router_w, router_b, w_gu, w_down, sw_gu, sw_down):
    t = x2d.shape[0]
    scores = jax.nn.sigmoid(x2d.astype(jnp.float32) @ router_w.astype(jnp.float32))
    choice = scores + router_b.astype(jnp.float32)
    grouped = choice.reshape(t, N_EXPERT_GROUPS, N_EXPERTS // N_EXPERT_GROUPS)
    group_score = lax.top_k(grouped, 2)[0].sum(-1)
    _, gidx = lax.top_k(group_score, TOPK_GROUPS)
    gmask = jax.nn.one_hot(gidx, N_EXPERT_GROUPS, dtype=jnp.float32).sum(1) > 0
    emask = jnp.repeat(gmask, N_EXPERTS // N_EXPERT_GROUPS, axis=1)
    _, eidx = lax.top_k(jnp.where(emask, choice, -jnp.inf), TOP_K)
    gate = jnp.take_along_axis(scores, eidx, axis=1)
    gate = gate / jnp.sum(gate, axis=-1, keepdims=True) * ROUTED_SCALE

    tk = t * TOP_K
    flat_e = eidx.reshape(-1)
    order = jnp.argsort(flat_e)
    sorted_e = flat_e[order]
    sizes = jnp.bincount(flat_e, length=N_EXPERTS).astype(jnp.int32)
    start = jnp.cumsum(sizes) - sizes
    padded = (sizes + ROW_BLOCK - 1) // ROW_BLOCK * ROW_BLOCK
    pend = jnp.cumsum(padded)
    pstart = pend - padded
    dest = pstart[sorted_e] + (jnp.arange(tk, dtype=jnp.int32) - start[sorted_e])
    n_rows = -(-tk // ROW_BLOCK) * ROW_BLOCK + N_EXPERTS * ROW_BLOCK
    n_blk = n_rows // ROW_BLOCK
    row_tok = jnp.zeros((n_rows,), jnp.int32).at[dest].set((order // TOP_K).astype(jnp.int32))
    row_gate = jnp.zeros((n_rows,), jnp.float32).at[dest].set(gate.reshape(-1)[order])
    blk_e = jnp.minimum(jnp.searchsorted(pend, jnp.arange(n_blk, dtype=jnp.int32) * ROW_BLOCK,
                                         side='right'), N_EXPERTS - 1)
    xs = x2d[row_tok].reshape(n_blk, ROW_BLOCK, -1)
    hgu = jnp.einsum('nbd,ndf->nbf', xs, w_gu[blk_e])
    hg, hu = jnp.split(hgu, 2, axis=-1)
    y = jnp.einsum('nbf,nfd->nbd', jax.nn.silu(hg) * hu, w_down[blk_e]).reshape(n_rows, -1)
    y = y * row_gate[:, None].astype(y.dtype)
    routed = jax.ops.segment_sum(y, row_tok, num_segments=t)
    sg, su = jnp.split(x2d @ sw_gu, 2, axis=-1)
    shared = (jax.nn.silu(sg) * su) @ sw_down
    return routed + shared


def setup_inputs(seed: int = 0) -> dict:
    key = jax.random.key(seed)
    ks = jax.random.split(key, 21)

    def nrm(k, shape, scale):
        return jax.random.normal(k, shape, jnp.float32) * scale

    d = D_MODEL
    return {
        "x": nrm(ks[0], (BATCH, SEQ, d), 1.0),
        "rel_bias": nrm(ks[1], (N_BUCKETS, N_BIAS_HEADS), 0.5),
        "pool_w_in": nrm(ks[2], (N_POOL_LAYERS, d, d), d ** -0.5),
        "pool_w_group": nrm(ks[3], (N_POOL_LAYERS, POOL_GROUPS, POOL_GROUP_DIM, POOL_GROUP_DIM), POOL_GROUP_DIM ** -0.5),
        "pool_scale": 1.0 + nrm(ks[4], (N_POOL_LAYERS, d), 0.02),
        "pool_w_out": nrm(ks[5], (N_POOL_LAYERS, d, d), BETA * d ** -0.5),
        "dil_w_qkv": nrm(ks[6], (N_DIL_LAYERS, d, DIL_QKV_COLS), d ** -0.5),
        "dil_w_out": nrm(ks[7], (N_DIL_LAYERS, DIL_WIDTH, d), BETA * DIL_WIDTH ** -0.5),
        "conv_w_in": nrm(ks[8], (N_CONV_LAYERS, d, 3 * d), d ** -0.5),
        "conv_w": nrm(ks[9], (N_CONV_LAYERS, CONV_WIDTH, d), CONV_WIDTH ** -0.5),
        "conv_w_out": nrm(ks[10], (N_CONV_LAYERS, d, d), BETA * d ** -0.5),
        "ln_gain": 1.0 + nrm(ks[11], (DEPTH, 2, d), 0.02),
        "ln_bias": nrm(ks[12], (DEPTH, 2, d), 0.02),
        "router_w": nrm(ks[13], (DEPTH, d, N_EXPERTS), d ** -0.5),
        "router_bias": nrm(ks[14], (DEPTH, N_EXPERTS), 0.01),
        "expert_w_gu": nrm(ks[15], (DEPTH, N_EXPERTS, d, 2 * EXPERT_DIM), d ** -0.5),
        "expert_w_down": nrm(ks[16], (DEPTH, N_EXPERTS, EXPERT_DIM, d), BETA * EXPERT_DIM ** -0.5),
        "shared_w_gu": nrm(ks[17], (DEPTH, d, 2 * SHARED_DIM), d ** -0.5),
        "shared_w_down": nrm(ks[18], (DEPTH, SHARED_DIM, d), BETA * SHARED_DIM ** -0.5),
    }


def reference(x, rel_bias, pool_w_in, pool_w_group, pool_scale, pool_w_out,
              dil_w_qkv, dil_w_out, conv_w_in, conv_w, conv_w_out,
              ln_gain, ln_bias, router_w, router_bias, expert_w_gu, expert_w_down,
              shared_w_gu, shared_w_down):
    bn, s, d = x.shape
    ia = ib = ic = 0
    for i in range(DEPTH):
        kind = i % N_MIXERS
        if kind == 0:
            m = pool_mixer(x, pool_w_in[ia], pool_w_group[ia], pool_scale[ia], pool_w_out[ia])
            ia += 1
        elif kind == 1:
            m = dilated_mixer(x, dil_w_qkv[ib], dil_w_out[ib], rel_bias)
            ib += 1
        else:
            m = conv_mixer(x, conv_w_in[ic], conv_w[ic], conv_w_out[ic])
            ic += 1
        x = layer_norm(ALPHA * x + m, ln_gain[i, 0], ln_bias[i, 0])
        f = moe(x.reshape(bn * s, d), router_w[i], router_bias[i], expert_w_gu[i],
                expert_w_down[i], shared_w_gu[i], shared_w_down[i]).reshape(bn, s, d)
        x = layer_norm(ALPHA * x + f, ln_gain[i, 1], ln_bias[i, 1])
    return x
```

```python
import functools
import math

import jax
import jax.numpy as jnp
from jax import lax
from jax.experimental import pallas as pl
from jax.experimental.pallas import tpu as pltpu

F32 = jnp.float32
BF16 = jnp.bfloat16

DEPTH = 4
D_MODEL = 2048
POOL_WINDOWS = (2, 4, 8, 16)
POOL_GROUP_DIM = D_MODEL // len(POOL_WINDOWS)
POOL_HALO = 16
DIL_CONFIGS = ((128, 1), (512, 4), (2048, 16))
HEAD_DIM = 64
DIL_HEADS = 8
DIL_WIDTH = DIL_HEADS * HEAD_DIM
DIL_BLOCK = 128
N_BUCKETS = 32
MAX_DISTANCE = 2048
CONV_WIDTH = 3
CONV_HALO = 8
N_EXPERTS = 64
TOP_K = 8
N_EXPERT_GROUPS = 8
GROUP_SIZE = N_EXPERTS // N_EXPERT_GROUPS
TOPK_GROUPS = 4
EXPERT_DIM = 128
SHARED_DIM = 256
ROUTED_SCALE = 2.5
ROW_BLOCK = 256
ALPHA = (2.0 * DEPTH) ** 0.25
LN_EPS = 1e-5
NEG = -1e30

LANES = 128
ROW_CHUNKS = D_MODEL // LANES
VMEM_LIMIT = 52 * 1024 * 1024

_NT = (((1,), (1,)), ((), ()))


def _dot(a, b):
    return jnp.dot(a, b, preferred_element_type=F32)


def _params(*sem):
    return pltpu.CompilerParams(dimension_semantics=sem, vmem_limit_bytes=VMEM_LIMIT)


def _layer_norm(z, g, b):
    mu = jnp.mean(z, axis=-1, keepdims=True)
    zc = z - mu
    var = jnp.mean(zc * zc, axis=-1, keepdims=True)
    return zc * lax.rsqrt(var + LN_EPS) * g + b


def _silu(v):
    return v / (1.0 + jnp.exp(-v))


def _mm_kernel(x_ref, w_ref, o_ref):
    o_ref[...] = _dot(x_ref[...], w_ref[...]).astype(o_ref.dtype)


def _matmul(x, w, tm, tn):
    t, k = x.shape
    n = w.shape[1]
    return pl.pallas_call(
        _mm_kernel,
        grid=(n // tn, t // tm),
        in_specs=[pl.BlockSpec((tm, k), lambda j, i: (i, 0)),
                  pl.BlockSpec((k, tn), lambda j, i: (0, j))],
        out_specs=pl.BlockSpec((tm, tn), lambda j, i: (i, j)),
        out_shape=jax.ShapeDtypeStruct((t, n), BF16),
        compiler_params=_params("parallel", "arbitrary"),
        name="matmul",
    )(x, w)


def _proj_ln_kernel(a_ref, w_ref, x_ref, g_ref, b_ref, of_ref, ob_ref):
    z = ALPHA * x_ref[...] + _dot(a_ref[...], w_ref[...])
    y = _layer_norm(z, g_ref[...], b_ref[...])
    of_ref[...] = y
    ob_ref[...] = y.astype(BF16)


def _proj_ln(a, w, x, g, b, tm=256):
    t, k = a.shape
    d = w.shape[1]
    row = lambda i: (i, 0)
    const = lambda i: (0, 0)
    return pl.pallas_call(
        _proj_ln_kernel,
        grid=(t // tm,),
        in_specs=[pl.BlockSpec((tm, k), row), pl.BlockSpec((k, d), const),
                  pl.BlockSpec((tm, d), row), pl.BlockSpec((1, d), const),
                  pl.BlockSpec((1, d), const)],
        out_specs=[pl.BlockSpec((tm, d), row), pl.BlockSpec((tm, d), row)],
        out_shape=[jax.ShapeDtypeStruct((t, d), F32), jax.ShapeDtypeStruct((t, d), BF16)],
        compiler_params=_params("arbitrary"),
        name="proj_ln",
    )(a, w, x, g, b)


def _pool_kernel(x_ref, win_ref, wg_ref, sc_ref, o_ref, carry_ref, *, tm, tiles_per_seq):
    i = pl.program_id(0)
    seq_tile = i % tiles_per_seq

    @pl.when(seq_tile == 0)
    def _():
        carry_ref[...] = jnp.zeros_like(carry_ref)

    xb = x_ref[...]
    pos = lax.broadcasted_iota(jnp.int32, (tm, 1), 0) + seq_tile * tm
    gd = POOL_GROUP_DIM
    for g, w in enumerate(POOL_WINDOWS):
        cols = slice(g * gd, (g + 1) * gd)
        u = _dot(xb, win_ref[:, cols])
        s = jnp.concatenate([carry_ref[g], u], axis=0)
        sh = 1
        while sh < w:
            s = s + pltpu.roll(s, sh, axis=0)
            sh *= 2
        cnt = jnp.minimum(pos + 1, w).astype(F32)
        pooled = s[POOL_HALO:, :] / cnt - u
        carry_ref[g] = u[tm - POOL_HALO:, :]
        mixed = _dot(pooled.astype(BF16), wg_ref[g]) * sc_ref[:, cols]
        o_ref[:, cols] = mixed.astype(BF16)


def _pool_mix(xb, w_in, w_group, scale, seq_len, tm=256):
    t, d = xb.shape
    gd = POOL_GROUP_DIM
    ng = len(POOL_WINDOWS)
    return pl.pallas_call(
        functools.partial(_pool_kernel, tm=tm, tiles_per_seq=seq_len // tm),
        grid=(t // tm,),
        in_specs=[pl.BlockSpec((tm, d), lambda i: (i, 0)),
                  pl.BlockSpec((d, d), lambda i: (0, 0)),
                  pl.BlockSpec((ng, gd, gd), lambda i: (0, 0, 0)),
                  pl.BlockSpec((1, d), lambda i: (0, 0))],
        out_specs=pl.BlockSpec((tm, d), lambda i: (i, 0)),
        out_shape=jax.ShapeDtypeStruct((t, d), BF16),
        scratch_shapes=[pltpu.VMEM((ng, POOL_HALO, gd), F32)],
        compiler_params=_params("arbitrary"),
        name="pool_mix",
    )(xb, w_in, w_group, scale)


def _conv_kernel(x_ref, wb_ref, wc_ref, wh_ref, cw_ref, o_ref, carry_ref, *, tm, tiles_per_seq):
    i = pl.program_id(1)

    @pl.when(i % tiles_per_seq == 0)
    def _():
        carry_ref[...] = jnp.zeros_like(carry_ref)

    xb = x_ref[...]
    b_gate = _dot(xb, wb_ref[...])
    u = _dot(xb, wc_ref[...]) * _dot(xb, wh_ref[...])
    ext = jnp.concatenate([carry_ref[...], u], axis=0)
    y = cw_ref[CONV_WIDTH - 1:CONV_WIDTH, :] * u
    for back in range(1, CONV_WIDTH):
        tap = CONV_WIDTH - 1 - back
        y = y + cw_ref[tap:tap + 1, :] * pltpu.roll(ext, back, axis=0)[CONV_HALO:, :]
    carry_ref[...] = u[tm - CONV_HALO:, :]
    o_ref[...] = (b_gate * y).astype(BF16)


def _conv_mix(xb, w_in, conv_w, seq_len, tm=512, tn=512):
    t, d = xb.shape
    nj = d // tn
    return pl.pallas_call(
        functools.partial(_conv_kernel, tm=tm, tiles_per_seq=seq_len // tm),
        grid=(nj, t // tm),
        in_specs=[pl.BlockSpec((tm, d), lambda j, i: (i, 0)),
                  pl.BlockSpec((d, tn), lambda j, i: (0, j)),
                  pl.BlockSpec((d, tn), lambda j, i: (0, nj + j)),
                  pl.BlockSpec((d, tn), lambda j, i: (0, 2 * nj + j)),
                  pl.BlockSpec((CONV_WIDTH, tn), lambda j, i: (0, j))],
        out_specs=pl.BlockSpec((tm, tn), lambda j, i: (i, j)),
        out_shape=jax.ShapeDtypeStruct((t, d), BF16),
        scratch_shapes=[pltpu.VMEM((CONV_HALO, tn), F32)],
        compiler_params=_params("parallel", "arbitrary"),
        name="conv_mix",
    )(xb, w_in, w_in, w_in, conv_w)


def _t5_bucket(dist):
    max_exact = N_BUCKETS // 2
    is_small = dist < max_exact
    distf = jnp.maximum(dist, 1).astype(F32)
    large = max_exact + (jnp.log(distf / max_exact) / math.log(MAX_DISTANCE / max_exact)
                         * (N_BUCKETS - max_exact)).astype(jnp.int32)
    large = jnp.minimum(large, N_BUCKETS - 1)
    return jnp.where(is_small, dist, large)


def _attn_bias(bias_tab, window, dil, has_prev):
    steps = window // dil
    q_loc = jnp.arange(DIL_BLOCK)[:, None]
    k_loc = jnp.arange(2 * DIL_BLOCK)[None, :]
    rel = q_loc + DIL_BLOCK - k_loc
    band = (rel >= 0) & (rel <= steps)
    bias = bias_tab[_t5_bucket(jnp.maximum(rel, 0) * dil)]
    bias = jnp.where(band[None], jnp.transpose(bias, (2, 0, 1)).astype(F32), NEG)
    return bias if has_prev else bias[:, :, DIL_BLOCK:]


def _attn_kernel(*refs, has_prev):
    if has_prev:
        q_ref, kc_ref, kp_ref, vc_ref, vp_ref, bias_ref, o_ref, l_ref = refs
    else:
        q_ref, kc_ref, vc_ref, bias_ref, o_ref, l_ref = refs
    n = pl.program_id(1)
    q = q_ref[0]
    kc = kc_ref[0]
    vc = vc_ref[0]
    if has_prev:
        kp = kp_ref[0]
        vp = vp_ref[0]
        col = lax.broadcasted_iota(jnp.int32, (DIL_BLOCK, 2 * DIL_BLOCK), 1)
        keep = (col >= DIL_BLOCK) | (n > 0)
    outs, lses = [], []
    for h in range(DIL_HEADS):
        sl = slice(h * HEAD_DIM, (h + 1) * HEAD_DIM)
        if has_prev:
            kh = jnp.concatenate([kp[:, sl], kc[:, sl]], axis=0)
            vh = jnp.concatenate([vp[:, sl], vc[:, sl]], axis=0)
        else:
            kh, vh = kc[:, sl], vc[:, sl]
        s = lax.dot_general(q[:, sl], kh, _NT, preferred_element_type=F32) * (HEAD_DIM ** -0.5) + bias_ref[h]
        if has_prev:
            s = jnp.where(keep, s, NEG)
        m = jnp.max(s, axis=-1, keepdims=True)
        p = jnp.exp(s - m)
        l = jnp.sum(p, axis=-1, keepdims=True)
        outs.append(_dot(p.astype(BF16), vh) / l)
        lses.append(jnp.broadcast_to(m + jnp.log(l), (DIL_BLOCK, HEAD_DIM)))
    o_ref[0] = jnp.concatenate(outs, axis=1)
    l_ref[0] = jnp.concatenate(lses, axis=1)


def _dilated_group(qkv, bias_tab, g, batch, seq_len):
    window, dil = DIL_CONFIGS[g]
    span = DIL_BLOCK * dil
    nb = seq_len // span
    has_prev = nb > 1
    ncol = qkv.shape[1] // DIL_WIDTH
    view = qkv.reshape(batch, seq_len // dil, dil * qkv.shape[1])
    blk = (1, DIL_BLOCK, DIL_WIDTH)

    def cur(kind):
        return pl.BlockSpec(blk, lambda b, n, r: (b, n, r * ncol + 3 * g + kind))

    def prev(kind):
        return pl.BlockSpec(blk, lambda b, n, r: (b, jnp.maximum(n - 1, 0), r * ncol + 3 * g + kind))

    bias = _attn_bias(bias_tab, window, dil, has_prev)
    bias_spec = pl.BlockSpec(bias.shape, lambda b, n, r: (0, 0, 0))
    if has_prev:
        in_specs = [cur(0), cur(1), prev(1), cur(2), prev(2), bias_spec]
        args = (view,) * 5 + (bias,)
    else:
        in_specs = [cur(0), cur(1), cur(2), bias_spec]
        args = (view,) * 3 + (bias,)
    out_spec = pl.BlockSpec(blk, lambda b, n, r: (b, n, r))
    out_sds = jax.ShapeDtypeStruct((batch, seq_len // dil, dil * DIL_WIDTH), F32)
    o, l = pl.pallas_call(
        functools.partial(_attn_kernel, has_prev=has_prev),
        grid=(batch, nb, dil),
        in_specs=in_specs,
        out_specs=[out_spec, out_spec],
        out_shape=[out_sds, out_sds],
        compiler_params=_params("parallel", "arbitrary", "arbitrary"),
        name=f"dilated_attn_g{g}",
    )(*args)
    t = batch * seq_len
    return o.reshape(t, DIL_WIDTH), l.reshape(t, DIL_WIDTH)


def _attn_out_kernel(o0, o1, o2, l0, l1, l2, w_ref, x_ref, g_ref, b_ref, of_ref, ob_ref):
    a0, a1, a2 = l0[...], l1[...], l2[...]
    m = jnp.maximum(jnp.maximum(a0, a1), a2)
    e0, e1, e2 = jnp.exp(a0 - m), jnp.exp(a1 - m), jnp.exp(a2 - m)
    o = (e0 * o0[...] + e1 * o1[...] + e2 * o2[...]) / (e0 + e1 + e2)
    z = ALPHA * x_ref[...] + _dot(o.astype(BF16), w_ref[...])
    y = _layer_norm(z, g_ref[...], b_ref[...])
    of_ref[...] = y
    ob_ref[...] = y.astype(BF16)


def _attn_out(os_, ls_, w, x, g, b, tm=256):
    t, d = x.shape
    k = w.shape[0]
    row = lambda i: (i, 0)
    const = lambda i: (0, 0)
    part = pl.BlockSpec((tm, k), row)
    return pl.pallas_call(
        _attn_out_kernel,
        grid=(t // tm,),
        in_specs=[part] * 6 + [pl.BlockSpec((k, d), const), pl.BlockSpec((tm, d), row),
                               pl.BlockSpec((1, d), const), pl.BlockSpec((1, d), const)],
        out_specs=[pl.BlockSpec((tm, d), row), pl.BlockSpec((tm, d), row)],
        out_shape=[jax.ShapeDtypeStruct((t, d), F32), jax.ShapeDtypeStruct((t, d), BF16)],
        compiler_params=_params("arbitrary"),
        name="attn_out_ln",
    )(*os_, *ls_, w, x, g, b)


def _router_kernel(x_ref, wt_ref, b_ref, eidx_ref, gate_ref, pos_ref, cnt_ref, base_ref, *, tm):
    @pl.when(pl.program_id(0) == 0)
    def _():
        base_ref[...] = jnp.zeros_like(base_ref)

    x = x_ref[...]
    w = wt_ref[...]
    xh = x.astype(BF16)
    xl = (x - xh.astype(F32)).astype(BF16)
    wh = w.astype(BF16)
    wl = (w - wh.astype(F32)).astype(BF16)
    dg = functools.partial(lax.dot_general, dimension_numbers=_NT, preferred_element_type=F32)
    logits = dg(wh, xh) + (dg(wh, xl) + dg(wl, xh))
    scores = 1.0 / (1.0 + jnp.exp(-logits))
    choice = scores + b_ref[...]

    iota_g = lax.broadcasted_iota(jnp.int32, (GROUP_SIZE, tm), 0)
    gs = []
    for gi in range(N_EXPERT_GROUPS):
        cg = choice[gi * GROUP_SIZE:(gi + 1) * GROUP_SIZE, :]
        t1 = jnp.max(cg, axis=0, keepdims=True)
        i1 = jnp.min(jnp.where(cg == t1, iota_g, GROUP_SIZE), axis=0, keepdims=True)
        t2 = jnp.max(jnp.where(iota_g == i1, -jnp.inf, cg), axis=0, keepdims=True)
        gs.append(t1 + t2)
    masked = []
    for gi in range(N_EXPERT_GROUPS):
        rank = jnp.zeros((1, tm), jnp.int32)
        for gj in range(N_EXPERT_GROUPS):
            if gj != gi:
                beats = (gs[gj] >= gs[gi]) if gj < gi else (gs[gj] > gs[gi])
                rank = rank + beats.astype(jnp.int32)
        masked.append(jnp.where(rank < TOPK_GROUPS, choice[gi * GROUP_SIZE:(gi + 1) * GROUP_SIZE, :], -jnp.inf))
    c = jnp.concatenate(masked, axis=0)

    iota_e = lax.broadcasted_iota(jnp.int32, (N_EXPERTS, tm), 0)
    picks, gates = [], []
    sel = jnp.zeros((N_EXPERTS, tm), F32)
    for _ in range(TOP_K):
        m = jnp.max(c, axis=0, keepdims=True)
        idx = jnp.min(jnp.where(c == m, iota_e, N_EXPERTS), axis=0, keepdims=True)
        hit = iota_e == idx
        picks.append(hit)
        gates.append(jnp.sum(jnp.where(hit, scores, 0.0), axis=0, keepdims=True))
        sel = jnp.where(hit, 1.0, sel)
        c = jnp.where(hit, -jnp.inf, c)
    gsum = gates[0]
    for gk in gates[1:]:
        gsum = gsum + gk

    before = (lax.broadcasted_iota(jnp.int32, (tm, tm), 0) < lax.broadcasted_iota(jnp.int32, (tm, tm), 1))
    rank_in_tile = _dot(sel.astype(BF16), before.astype(BF16))
    base = base_ref[...]
    posfull = base[:, 0:1] + rank_in_tile
    efull = iota_e.astype(F32)
    for k in range(TOP_K):
        hit = picks[k]
        eidx_ref[k:k + 1, :] = jnp.sum(jnp.where(hit, efull, 0.0), axis=0, keepdims=True).astype(jnp.int32)
        pos_ref[k:k + 1, :] = jnp.sum(jnp.where(hit, posfull, 0.0), axis=0, keepdims=True).astype(jnp.int32)
        gate_ref[k:k + 1, :] = gates[k] / gsum * ROUTED_SCALE
    new_base = base + jnp.sum(sel, axis=1, keepdims=True)
    base_ref[...] = new_base
    cnt_ref[...] = new_base


def _route(x, router_wt, router_b, tm=512):
    t, d = x.shape
    tok = pl.BlockSpec((TOP_K, tm), lambda i: (0, i))
    cnt = pl.BlockSpec((N_EXPERTS, LANES), lambda i: (0, 0))
    return pl.pallas_call(
        functools.partial(_router_kernel, tm=tm),
        grid=(t // tm,),
        in_specs=[pl.BlockSpec((tm, d), lambda i: (i, 0)),
                  pl.BlockSpec((N_EXPERTS, d), lambda i: (0, 0)),
                  pl.BlockSpec((N_EXPERTS, 1), lambda i: (0, 0))],
        out_specs=[tok, tok, tok, cnt],
        out_shape=[jax.ShapeDtypeStruct((TOP_K, t), jnp.int32), jax.ShapeDtypeStruct((TOP_K, t), F32),
                   jax.ShapeDtypeStruct((TOP_K, t), jnp.int32), jax.ShapeDtypeStruct((N_EXPERTS, LANES), F32)],
        scratch_shapes=[pltpu.VMEM((N_EXPERTS, LANES), F32)],
        compiler_params=_params("arbitrary"),
        name="router",
    )(x, router_wt, router_b)


def _gather_rows(idx_ref, n_rows, src_hbm, dst, sem):
    unroll = 8

    def body(o, carry):
        for u in range(unroll):
            r = o * unroll + u
            src = pl.multiple_of(idx_ref[0, 0, r] * ROW_CHUNKS, ROW_CHUNKS)
            pltpu.make_async_copy(src_hbm.at[pl.ds(src, ROW_CHUNKS)],
                                  dst.at[pl.ds(r * ROW_CHUNKS, ROW_CHUNKS)], sem).start()
        return carry

    lax.fori_loop(0, n_rows // unroll, body, 0)


def _wait_rows(n_rows, src_hbm, dst, sem):
    pltpu.make_async_copy(src_hbm.at[pl.ds(0, n_rows * ROW_CHUNKS)], dst, sem).wait()


def _rows_from_chunks(buf, n_rows):
    return jnp.concatenate([buf[pl.ds(c, n_rows, stride=ROW_CHUNKS), :] for c in range(ROW_CHUNKS)], axis=1)


def _expert_kernel(blk_e_ref, nused_ref, tok_ref, tok_next_ref, gate_ref, x_hbm, wgu_ref, wd_ref, y_ref,
                   xbuf, sem):
    i = pl.program_id(0)
    nused = nused_ref[0]
    slot = i % 2

    @pl.when(i == 0)
    def _():
        _gather_rows(tok_ref, ROW_BLOCK, x_hbm, xbuf.at[0], sem.at[0])

    @pl.when(i + 1 < nused)
    def _():
        _gather_rows(tok_next_ref, ROW_BLOCK, x_hbm, xbuf.at[1 - slot], sem.at[1 - slot])

    @pl.when(i < nused)
    def _():
        _wait_rows(ROW_BLOCK, x_hbm, xbuf.at[slot], sem.at[slot])
        xs = _rows_from_chunks(xbuf.at[slot], ROW_BLOCK).astype(BF16)
        hgu = _dot(xs, wgu_ref[0])
        h = _silu(hgu[:, :EXPERT_DIM]) * hgu[:, EXPERT_DIM:]
        y_ref[...] = _dot(h.astype(BF16), wd_ref[0]) * gate_ref[...]

    @pl.when(i >= nused)
    def _():
        y_ref[...] = jnp.zeros_like(y_ref)


def _experts(x, row_tok, row_gate, blk_e, nused, w_gu, w_down):
    t, d = x.shape
    n_rows = row_tok.shape[0]
    n_blk = n_rows // ROW_BLOCK
    f2 = w_gu.shape[2]
    tok3 = row_tok.reshape(n_blk, 1, ROW_BLOCK)
    smem_blk = lambda imap: pl.BlockSpec((1, 1, ROW_BLOCK), imap, memory_space=pltpu.SMEM)
    grid_spec = pltpu.PrefetchScalarGridSpec(
        num_scalar_prefetch=2,
        grid=(n_blk,),
        in_specs=[smem_blk(lambda i, be, nu: (i, 0, 0)),
                  smem_blk(lambda i, be, nu: (jnp.minimum(i + 1, n_blk - 1), 0, 0)),
                  pl.BlockSpec((ROW_BLOCK, 1), lambda i, be, nu: (i, 0)),
                  pl.BlockSpec(memory_space=pl.ANY),
                  pl.BlockSpec((1, d, f2), lambda i, be, nu: (be[i], 0, 0)),
                  pl.BlockSpec((1, f2 // 2, d), lambda i, be, nu: (be[i], 0, 0))],
        out_specs=pl.BlockSpec((ROW_BLOCK, d), lambda i, be, nu: (i, 0)),
        scratch_shapes=[pltpu.VMEM((2, ROW_BLOCK * ROW_CHUNKS, LANES), F32),
                        pltpu.SemaphoreType.DMA((2,))],
    )
    return pl.pallas_call(
        _expert_kernel,
        grid_spec=grid_spec,
        out_shape=jax.ShapeDtypeStruct((n_rows, d), F32),
        compiler_params=_params("arbitrary"),
        name="experts",
    )(blk_e, nused, tok3, tok3, row_gate.reshape(n_rows, 1), x.reshape(t * ROW_CHUNKS, LANES), w_gu, w_down)


def _combine_kernel(dest_ref, dest_next_ref, y_hbm, xf_ref, xb_ref, sgu_ref, sd_ref, g_ref, b_ref,
                    of_ref, ob_ref, ybuf, sbuf, sem, *, tm, n_tiles):
    i = pl.program_id(0)
    slot = i % 2
    n_gather = TOP_K * tm

    @pl.when(i == 0)
    def _():
        _gather_rows(dest_ref, n_gather, y_hbm, ybuf.at[0], sem.at[0])

    @pl.when(i + 1 < n_tiles)
    def _():
        _gather_rows(dest_next_ref, n_gather, y_hbm, ybuf.at[1 - slot], sem.at[1 - slot])

    _wait_rows(n_gather, y_hbm, ybuf.at[slot], sem.at[slot])
    rows = tm * ROW_CHUNKS
    acc = ybuf[slot, 0:rows, :]
    for k in range(1, TOP_K):
        acc = acc + ybuf[slot, k * rows:(k + 1) * rows, :]
    sbuf[...] = acc
    routed = _rows_from_chunks(sbuf, tm)

    sgu = _dot(xb_ref[...], sgu_ref[...])
    sh = _silu(sgu[:, :SHARED_DIM]) * sgu[:, SHARED_DIM:]
    shared = _dot(sh.astype(BF16), sd_ref[...])
    z = ALPHA * xf_ref[...] + (routed + shared)
    y = _layer_norm(z, g_ref[...], b_ref[...])
    of_ref[...] = y
    ob_ref[...] = y.astype(BF16)


def _combine(y, dest, xf, xb, sw_gu, sw_down, g, b, tm=128):
    t, d = xf.shape
    n_rows = y.shape[0]
    n_tiles = t // tm
    n_gather = TOP_K * tm
    dest3 = dest.reshape(TOP_K, n_tiles, tm).transpose(1, 0, 2).reshape(n_tiles, 1, n_gather)
    row = lambda i: (i, 0)
    const = lambda i: (0, 0)
    smem_blk = lambda imap: pl.BlockSpec((1, 1, n_gather), imap, memory_space=pltpu.SMEM)
    return pl.pallas_call(
        functools.partial(_combine_kernel, tm=tm, n_tiles=n_tiles),
        grid=(n_tiles,),
        in_specs=[smem_blk(lambda i: (i, 0, 0)),
                  smem_blk(lambda i: (jnp.minimum(i + 1, n_tiles - 1), 0, 0)),
                  pl.BlockSpec(memory_space=pl.ANY),
                  pl.BlockSpec((tm, d), row), pl.BlockSpec((tm, d), row),
                  pl.BlockSpec(sw_gu.shape, const), pl.BlockSpec(sw_down.shape, const),
                  pl.BlockSpec((1, d), const), pl.BlockSpec((1, d), const)],
        out_specs=[pl.BlockSpec((tm, d), row), pl.BlockSpec((tm, d), row)],
        out_shape=[jax.ShapeDtypeStruct((t, d), F32), jax.ShapeDtypeStruct((t, d), BF16)],
        scratch_shapes=[pltpu.VMEM((2, n_gather * ROW_CHUNKS, LANES), F32),
                        pltpu.VMEM((tm * ROW_CHUNKS, LANES), F32),
                        pltpu.SemaphoreType.DMA((2,))],
        compiler_params=_params("arbitrary"),
        name="combine_ln",
    )(dest3, dest3, y.reshape(n_rows * ROW_CHUNKS, LANES), xf, xb, sw_gu, sw_down, g, b)


def _moe_layer(xf, xb, router_w, router_b, w_gu, w_down, sw_gu, sw_down, g, b):
    t, d = xf.shape
    eidx, gate, pos, cnt = _route(xf, router_w.T, router_b.reshape(N_EXPERTS, 1))

    sizes = cnt[:, 0].astype(jnp.int32)
    padded = (sizes + ROW_BLOCK - 1) // ROW_BLOCK * ROW_BLOCK
    pend = jnp.cumsum(padded)
    pstart = pend - padded
    tk = t * TOP_K
    n_rows = -(-tk // ROW_BLOCK) * ROW_BLOCK + N_EXPERTS * ROW_BLOCK
    n_blk = n_rows // ROW_BLOCK
    onehot = eidx[:, :, None] == jnp.arange(N_EXPERTS, dtype=jnp.int32)
    dest = jnp.sum(jnp.where(onehot, pstart, 0), axis=-1) + pos
    tok_ids = jnp.broadcast_to(jnp.arange(t, dtype=jnp.int32), (TOP_K, t))
    flat_dest = dest.reshape(-1)
    row_tok = jnp.zeros((n_rows,), jnp.int32).at[flat_dest].set(tok_ids.reshape(-1))
    row_gate = jnp.zeros((n_rows,), F32).at[flat_dest].set(gate.reshape(-1))
    blk_start = jnp.arange(n_blk, dtype=jnp.int32) * ROW_BLOCK
    blk_e = jnp.minimum(jnp.searchsorted(pend, blk_start, side='right'), N_EXPERTS - 1).astype(jnp.int32)
    nused = (pend[-1:] // ROW_BLOCK).astype(jnp.int32)

    y = _experts(xf, row_tok, row_gate, blk_e, nused, w_gu, w_down)
    return _combine(y, dest, xf, xb, sw_gu, sw_down, g, b)


def kernel(x, rel_bias, pool_w_in, pool_w_group, pool_scale, pool_w_out, dil_w_qkv, dil_w_out,
           conv_w_in, conv_w, conv_w_out, ln_gain, ln_bias, router_w, router_bias,
           expert_w_gu, expert_w_down, shared_w_gu, shared_w_down):
    bn, s, d = x.shape
    t = bn * s
    xf = x.reshape(t, d)
    xb = xf.astype(BF16)
    bf = lambda a: a.astype(BF16)
    ia = ib = ic = 0
    for i in range(DEPTH):
        kind = i % 3
        g1, b1 = ln_gain[i, 0].reshape(1, d), ln_bias[i, 0].reshape(1, d)
        g2, b2 = ln_gain[i, 1].reshape(1, d), ln_bias[i, 1].reshape(1, d)
        if kind == 0:
            mixed = _pool_mix(xb, bf(pool_w_in[ia]), bf(pool_w_group[ia]), pool_scale[ia].reshape(1, d), s)
            xf, xb = _proj_ln(mixed, bf(pool_w_out[ia]), xf, g1, b1)
            ia += 1
        elif kind == 1:
            qkv = _matmul(xb, bf(dil_w_qkv[ib]), tm=512, tn=3 * DIL_WIDTH)
            os_, ls_ = [], []
            for g in range(len(DIL_CONFIGS)):
                o, l = _dilated_group(qkv, rel_bias[:, g * DIL_HEADS:(g + 1) * DIL_HEADS], g, bn, s)
                os_.append(o)
                ls_.append(l)
            xf, xb = _attn_out(os_, ls_, bf(dil_w_out[ib]), xf, g1, b1)
            ib += 1
        else:
            v = _conv_mix(xb, bf(conv_w_in[ic]), conv_w[ic], s)
            xf, xb = _proj_ln(v, bf(conv_w_out[ic]), xf, g1, b1)
            ic += 1
        xf, xb = _moe_layer(xf, xb, router_w[i], router_bias[i], bf(expert_w_gu[i]), bf(expert_w_down[i]),
                            bf(shared_w_gu[i]), bf(shared_w_down[i]), g2, b2)
    return xf.reshape(bn, s, d)
```

```python
import functools
import math

import jax
import jax.numpy as jnp
from jax import lax
from jax.experimental import pallas as pl
from jax.experimental.pallas import tpu as pltpu

F32 = jnp.float32
BF16 = jnp.bfloat16

DEPTH = 4
D_MODEL = 2048
POOL_WINDOWS = (2, 4, 8, 16)
POOL_GROUP_DIM = D_MODEL // len(POOL_WINDOWS)
POOL_HALO = 16
DIL_CONFIGS = ((128, 1), (512, 4), (2048, 16))
HEAD_DIM = 64
DIL_HEADS = 8
DIL_WIDTH = DIL_HEADS * HEAD_DIM
DIL_BLOCK = 128
N_BUCKETS = 32
MAX_DISTANCE = 2048
CONV_WIDTH = 3
CONV_HALO = 8
N_EXPERTS = 64
TOP_K = 8
N_EXPERT_GROUPS = 8
GROUP_SIZE = N_EXPERTS // N_EXPERT_GROUPS
TOPK_GROUPS = 4
EXPERT_DIM = 128
SHARED_DIM = 256
ROUTED_SCALE = 2.5
ROW_BLOCK = 256
ALPHA = (2.0 * DEPTH) ** 0.25
LN_EPS = 1e-5
NEG = -1e30

LANES = 128
ROW_CHUNKS = D_MODEL // LANES
PACK_CHUNKS = ROW_CHUNKS // 2
HIGH_HALF = -65536
VMEM_LIMIT = 52 * 1024 * 1024

_NT = (((1,), (1,)), ((), ()))


def _dot(a, b):
    return jnp.dot(a, b, preferred_element_type=F32)


def _params(*sem):
    return pltpu.CompilerParams(dimension_semantics=sem, vmem_limit_bytes=VMEM_LIMIT)


def _layer_norm(z, g, b):
    mu = jnp.mean(z, axis=-1, keepdims=True)
    zc = z - mu
    var = jnp.mean(zc * zc, axis=-1, keepdims=True)
    return zc * lax.rsqrt(var + LN_EPS) * g + b


def _silu(v):
    return v / (1.0 + jnp.exp(-v))


def _pack_rows(y):
    half = D_MODEL // 2
    lo = lax.bitcast_convert_type(y[:, :half].astype(BF16).astype(F32), jnp.int32)
    hi = lax.bitcast_convert_type(y[:, half:].astype(BF16).astype(F32), jnp.int32)
    return (hi & HIGH_HALF) | lax.shift_right_logical(lo, 16)


def _unpack_rows(xs_ref, n_rows):
    lo, hi = [], []
    for c in range(PACK_CHUNKS):
        p = xs_ref[pl.ds(c, n_rows, stride=PACK_CHUNKS), :]
        lo.append(lax.bitcast_convert_type(lax.shift_left(p, 16), F32))
        hi.append(lax.bitcast_convert_type(p & HIGH_HALF, F32))
    return jnp.concatenate(lo + hi, axis=1).astype(BF16)


def _store_ln_outputs(y, of_ref, ob_ref, op_ref):
    tm = y.shape[0]
    of_ref[...] = y
    ob_ref[...] = y.astype(BF16)
    packed = _pack_rows(y)
    for c in range(PACK_CHUNKS):
        op_ref[pl.ds(c, tm, stride=PACK_CHUNKS), :] = packed[:, c * LANES:(c + 1) * LANES]


def _ln_out_specs(t, d, tm):
    row = lambda i: (i, 0)
    specs = [pl.BlockSpec((tm, d), row), pl.BlockSpec((tm, d), row), pl.BlockSpec((tm * PACK_CHUNKS, LANES), row)]
    shapes = [jax.ShapeDtypeStruct((t, d), F32), jax.ShapeDtypeStruct((t, d), BF16),
              jax.ShapeDtypeStruct((t * PACK_CHUNKS, LANES), jnp.int32)]
    return specs, shapes


def _mm_kernel(x_ref, w_ref, o_ref):
    o_ref[...] = _dot(x_ref[...], w_ref[...]).astype(o_ref.dtype)


def _matmul(x, w, tm, tn):
    t, k = x.shape
    n = w.shape[1]
    return pl.pallas_call(
        _mm_kernel,
        grid=(n // tn, t // tm),
        in_specs=[pl.BlockSpec((tm, k), lambda j, i: (i, 0)),
                  pl.BlockSpec((k, tn), lambda j, i: (0, j))],
        out_specs=pl.BlockSpec((tm, tn), lambda j, i: (i, j)),
        out_shape=jax.ShapeDtypeStruct((t, n), BF16),
        compiler_params=_params("parallel", "arbitrary"),
        name="matmul",
    )(x, w)


def _proj_ln_kernel(a_ref, w_ref, x_ref, g_ref, b_ref, of_ref, ob_ref, op_ref):
    z = ALPHA * x_ref[...] + _dot(a_ref[...], w_ref[...])
    _store_ln_outputs(_layer_norm(z, g_ref[...], b_ref[...]), of_ref, ob_ref, op_ref)


def _proj_ln(a, w, x, g, b, tm=256):
    t, k = a.shape
    d = w.shape[1]
    row = lambda i: (i, 0)
    const = lambda i: (0, 0)
    out_specs, out_shape = _ln_out_specs(t, d, tm)
    return pl.pallas_call(
        _proj_ln_kernel,
        grid=(t // tm,),
        in_specs=[pl.BlockSpec((tm, k), row), pl.BlockSpec((k, d), const),
                  pl.BlockSpec((tm, d), row), pl.BlockSpec((1, d), const),
                  pl.BlockSpec((1, d), const)],
        out_specs=out_specs,
        out_shape=out_shape,
        compiler_params=_params("arbitrary"),
        name="proj_ln",
    )(a, w, x, g, b)


def _pool_kernel(x_ref, win_ref, wg_ref, sc_ref, o_ref, carry_ref, *, tm, tiles_per_seq):
    i = pl.program_id(0)
    seq_tile = i % tiles_per_seq

    @pl.when(seq_tile == 0)
    def _():
        carry_ref[...] = jnp.zeros_like(carry_ref)

    xb = x_ref[...]
    pos = lax.broadcasted_iota(jnp.int32, (tm, 1), 0) + seq_tile * tm
    gd = POOL_GROUP_DIM
    for g, w in enumerate(POOL_WINDOWS):
        cols = slice(g * gd, (g + 1) * gd)
        u = _dot(xb, win_ref[:, cols])
        s = jnp.concatenate([carry_ref[g], u], axis=0)
        sh = 1
        while sh < w:
            s = s + pltpu.roll(s, sh, axis=0)
            sh *= 2
        cnt = jnp.minimum(pos + 1, w).astype(F32)
        pooled = s[POOL_HALO:, :] / cnt - u
        carry_ref[g] = u[tm - POOL_HALO:, :]
        mixed = _dot(pooled.astype(BF16), wg_ref[g]) * sc_ref[:, cols]
        o_ref[:, cols] = mixed.astype(BF16)


def _pool_mix(xb, w_in, w_group, scale, seq_len, tm=256):
    t, d = xb.shape
    gd = POOL_GROUP_DIM
    ng = len(POOL_WINDOWS)
    return pl.pallas_call(
        functools.partial(_pool_kernel, tm=tm, tiles_per_seq=seq_len // tm),
        grid=(t // tm,),
        in_specs=[pl.BlockSpec((tm, d), lambda i: (i, 0)),
                  pl.BlockSpec((d, d), lambda i: (0, 0)),
                  pl.BlockSpec((ng, gd, gd), lambda i: (0, 0, 0)),
                  pl.BlockSpec((1, d), lambda i: (0, 0))],
        out_specs=pl.BlockSpec((tm, d), lambda i: (i, 0)),
        out_shape=jax.ShapeDtypeStruct((t, d), BF16),
        scratch_shapes=[pltpu.VMEM((ng, POOL_HALO, gd), F32)],
        compiler_params=_params("arbitrary"),
        name="pool_mix",
    )(xb, w_in, w_group, scale)


def _conv_kernel(x_ref, wb_ref, wc_ref, wh_ref, cw_ref, o_ref, carry_ref, *, tm, tiles_per_seq):
    i = pl.program_id(1)

    @pl.when(i % tiles_per_seq == 0)
    def _():
        carry_ref[...] = jnp.zeros_like(carry_ref)

    xb = x_ref[...]
    b_gate = _dot(xb, wb_ref[...])
    u = _dot(xb, wc_ref[...]) * _dot(xb, wh_ref[...])
    ext = jnp.concatenate([carry_ref[...], u], axis=0)
    y = cw_ref[CONV_WIDTH - 1:CONV_WIDTH, :] * u
    for back in range(1, CONV_WIDTH):
        tap = CONV_WIDTH - 1 - back
        y = y + cw_ref[tap:tap + 1, :] * pltpu.roll(ext, back, axis=0)[CONV_HALO:, :]
    carry_ref[...] = u[tm - CONV_HALO:, :]
    o_ref[...] = (b_gate * y).astype(BF16)


def _conv_mix(xb, w_in, conv_w, seq_len, tm=512, tn=512):
    t, d = xb.shape
    nj = d // tn
    return pl.pallas_call(
        functools.partial(_conv_kernel, tm=tm, tiles_per_seq=seq_len // tm),
        grid=(nj, t // tm),
        in_specs=[pl.BlockSpec((tm, d), lambda j, i: (i, 0)),
                  pl.BlockSpec((d, tn), lambda j, i: (0, j)),
                  pl.BlockSpec((d, tn), lambda j, i: (0, nj + j)),
                  pl.BlockSpec((d, tn), lambda j, i: (0, 2 * nj + j)),
                  pl.BlockSpec((CONV_WIDTH, tn), lambda j, i: (0, j))],
        out_specs=pl.BlockSpec((tm, tn), lambda j, i: (i, j)),
        out_shape=jax.ShapeDtypeStruct((t, d), BF16),
        scratch_shapes=[pltpu.VMEM((CONV_HALO, tn), F32)],
        compiler_params=_params("parallel", "arbitrary"),
        name="conv_mix",
    )(xb, w_in, w_in, w_in, conv_w)


def _t5_bucket(dist):
    max_exact = N_BUCKETS // 2
    is_small = dist < max_exact
    distf = jnp.maximum(dist, 1).astype(F32)
    large = max_exact + (jnp.log(distf / max_exact) / math.log(MAX_DISTANCE / max_exact)
                         * (N_BUCKETS - max_exact)).astype(jnp.int32)
    large = jnp.minimum(large, N_BUCKETS - 1)
    return jnp.where(is_small, dist, large)


def _attn_bias(bias_tab, window, dil, has_prev):
    steps = window // dil
    q_loc = jnp.arange(DIL_BLOCK)[:, None]
    k_loc = jnp.arange(2 * DIL_BLOCK)[None, :]
    rel = q_loc + DIL_BLOCK - k_loc
    band = (rel >= 0) & (rel <= steps)
    bias = bias_tab[_t5_bucket(jnp.maximum(rel, 0) * dil)]
    bias = jnp.where(band[None], jnp.transpose(bias, (2, 0, 1)).astype(F32), NEG)
    return bias if has_prev else bias[:, :, DIL_BLOCK:]


def _attn_kernel(*refs, has_prev):
    if has_prev:
        q_ref, kc_ref, kp_ref, vc_ref, vp_ref, bias_ref, o_ref, l_ref = refs
    else:
        q_ref, kc_ref, vc_ref, bias_ref, o_ref, l_ref = refs
    n = pl.program_id(1)
    q = q_ref[0]
    kc = kc_ref[0]
    vc = vc_ref[0]
    if has_prev:
        kp = kp_ref[0]
        vp = vp_ref[0]
        col = lax.broadcasted_iota(jnp.int32, (DIL_BLOCK, 2 * DIL_BLOCK), 1)
        keep = (col >= DIL_BLOCK) | (n > 0)
    outs, lses = [], []
    for h in range(DIL_HEADS):
        sl = slice(h * HEAD_DIM, (h + 1) * HEAD_DIM)
        if has_prev:
            kh = jnp.concatenate([kp[:, sl], kc[:, sl]], axis=0)
            vh = jnp.concatenate([vp[:, sl], vc[:, sl]], axis=0)
        else:
            kh, vh = kc[:, sl], vc[:, sl]
        s = lax.dot_general(q[:, sl], kh, _NT, preferred_element_type=F32) * (HEAD_DIM ** -0.5) + bias_ref[h]
        if has_prev:
            s = jnp.where(keep, s, NEG)
        m = jnp.max(s, axis=-1, keepdims=True)
        p = jnp.exp(s - m)
        l = jnp.sum(p, axis=-1, keepdims=True)
        outs.append(_dot(p.astype(BF16), vh) / l)
        lses.append(jnp.broadcast_to(m + jnp.log(l), (DIL_BLOCK, HEAD_DIM)))
    o_ref[0] = jnp.concatenate(outs, axis=1)
    l_ref[0] = jnp.concatenate(lses, axis=1)


def _dilated_group(qkv, bias_tab, g, batch, seq_len):
    window, dil = DIL_CONFIGS[g]
    span = DIL_BLOCK * dil
    nb = seq_len // span
    has_prev = nb > 1
    ncol = qkv.shape[1] // DIL_WIDTH
    view = qkv.reshape(batch, seq_len // dil, dil * qkv.shape[1])
    blk = (1, DIL_BLOCK, DIL_WIDTH)

    def cur(kind):
        return pl.BlockSpec(blk, lambda b, n, r: (b, n, r * ncol + 3 * g + kind))

    def prev(kind):
        return pl.BlockSpec(blk, lambda b, n, r: (b, jnp.maximum(n - 1, 0), r * ncol + 3 * g + kind))

    bias = _attn_bias(bias_tab, window, dil, has_prev)
    bias_spec = pl.BlockSpec(bias.shape, lambda b, n, r: (0, 0, 0))
    if has_prev:
        in_specs = [cur(0), cur(1), prev(1), cur(2), prev(2), bias_spec]
        args = (view,) * 5 + (bias,)
    else:
        in_specs = [cur(0), cur(1), cur(2), bias_spec]
        args = (view,) * 3 + (bias,)
    out_spec = pl.BlockSpec(blk, lambda b, n, r: (b, n, r))
    out_sds = jax.ShapeDtypeStruct((batch, seq_len // dil, dil * DIL_WIDTH), F32)
    o, l = pl.pallas_call(
        functools.partial(_attn_kernel, has_prev=has_prev),
        grid=(batch, nb, dil),
        in_specs=in_specs,
        out_specs=[out_spec, out_spec],
        out_shape=[out_sds, out_sds],
        compiler_params=_params("parallel", "arbitrary", "arbitrary"),
        name=f"dilated_attn_g{g}",
    )(*args)
    t = batch * seq_len
    return o.reshape(t, DIL_WIDTH), l.reshape(t, DIL_WIDTH)


def _attn_out_kernel(o0, o1, o2, l0, l1, l2, w_ref, x_ref, g_ref, b_ref, of_ref, ob_ref, op_ref):
    a0, a1, a2 = l0[...], l1[...], l2[...]
    m = jnp.maximum(jnp.maximum(a0, a1), a2)
    e0, e1, e2 = jnp.exp(a0 - m), jnp.exp(a1 - m), jnp.exp(a2 - m)
    o = (e0 * o0[...] + e1 * o1[...] + e2 * o2[...]) / (e0 + e1 + e2)
    z = ALPHA * x_ref[...] + _dot(o.astype(BF16), w_ref[...])
    _store_ln_outputs(_layer_norm(z, g_ref[...], b_ref[...]), of_ref, ob_ref, op_ref)


def _attn_out(os_, ls_, w, x, g, b, tm=256):
    t, d = x.shape
    k = w.shape[0]
    row = lambda i: (i, 0)
    const = lambda i: (0, 0)
    part = pl.BlockSpec((tm, k), row)
    out_specs, out_shape = _ln_out_specs(t, d, tm)
    return pl.pallas_call(
        _attn_out_kernel,
        grid=(t // tm,),
        in_specs=[part] * 6 + [pl.BlockSpec((k, d), const), pl.BlockSpec((tm, d), row),
                               pl.BlockSpec((1, d), const), pl.BlockSpec((1, d), const)],
        out_specs=out_specs,
        out_shape=out_shape,
        compiler_params=_params("arbitrary"),
        name="attn_out_ln",
    )(*os_, *ls_, w, x, g, b)


def _router_kernel(x_ref, wt_ref, b_ref, eidx_ref, gate_ref, pos_ref, cnt_ref, base_ref, *, tm):
    @pl.when(pl.program_id(0) == 0)
    def _():
        base_ref[...] = jnp.zeros_like(base_ref)

    x = x_ref[...]
    w = wt_ref[...]
    xh = x.astype(BF16)
    xl = (x - xh.astype(F32)).astype(BF16)
    wh = w.astype(BF16)
    wl = (w - wh.astype(F32)).astype(BF16)
    dg = functools.partial(lax.dot_general, dimension_numbers=_NT, preferred_element_type=F32)
    logits = dg(wh, xh) + (dg(wh, xl) + dg(wl, xh))
    scores = 1.0 / (1.0 + jnp.exp(-logits))
    choice = scores + b_ref[...]

    iota_g = lax.broadcasted_iota(jnp.int32, (GROUP_SIZE, tm), 0)
    gs = []
    for gi in range(N_EXPERT_GROUPS):
        cg = choice[gi * GROUP_SIZE:(gi + 1) * GROUP_SIZE, :]
        t1 = jnp.max(cg, axis=0, keepdims=True)
        i1 = jnp.min(jnp.where(cg == t1, iota_g, GROUP_SIZE), axis=0, keepdims=True)
        t2 = jnp.max(jnp.where(iota_g == i1, -jnp.inf, cg), axis=0, keepdims=True)
        gs.append(t1 + t2)
    masked = []
    for gi in range(N_EXPERT_GROUPS):
        rank = jnp.zeros((1, tm), jnp.int32)
        for gj in range(N_EXPERT_GROUPS):
            if gj != gi:
                beats = (gs[gj] >= gs[gi]) if gj < gi else (gs[gj] > gs[gi])
                rank = rank + beats.astype(jnp.int32)
        masked.append(jnp.where(rank < TOPK_GROUPS, choice[gi * GROUP_SIZE:(gi + 1) * GROUP_SIZE, :], -jnp.inf))
    c = jnp.concatenate(masked, axis=0)

    iota_e = lax.broadcasted_iota(jnp.int32, (N_EXPERTS, tm), 0)
    picks, gates = [], []
    sel = jnp.zeros((N_EXPERTS, tm), F32)
    for _ in range(TOP_K):
        m = jnp.max(c, axis=0, keepdims=True)
        idx = jnp.min(jnp.where(c == m, iota_e, N_EXPERTS), axis=0, keepdims=True)
        hit = iota_e == idx
        picks.append(hit)
        gates.append(jnp.sum(jnp.where(hit, scores, 0.0), axis=0, keepdims=True))
        sel = jnp.where(hit, 1.0, sel)
        c = jnp.where(hit, -jnp.inf, c)
    gsum = gates[0]
    for gk in gates[1:]:
        gsum = gsum + gk

    before = (lax.broadcasted_iota(jnp.int32, (tm, tm), 0) < lax.broadcasted_iota(jnp.int32, (tm, tm), 1))
    rank_in_tile = _dot(sel.astype(BF16), before.astype(BF16))
    base = base_ref[...]
    posfull = base[:, 0:1] + rank_in_tile
    efull = iota_e.astype(F32)
    for k in range(TOP_K):
        hit = picks[k]
        eidx_ref[k:k + 1, :] = jnp.sum(jnp.where(hit, efull, 0.0), axis=0, keepdims=True).astype(jnp.int32)
        pos_ref[k:k + 1, :] = jnp.sum(jnp.where(hit, posfull, 0.0), axis=0, keepdims=True).astype(jnp.int32)
        gate_ref[k:k + 1, :] = gates[k] / gsum * ROUTED_SCALE
    new_base = base + jnp.sum(sel, axis=1, keepdims=True)
    base_ref[...] = new_base
    cnt_ref[...] = new_base


def _route(x, router_wt, router_b, tm=512):
    t, d = x.shape
    tok = pl.BlockSpec((TOP_K, tm), lambda i: (0, i))
    cnt = pl.BlockSpec((N_EXPERTS, LANES), lambda i: (0, 0))
    return pl.pallas_call(
        functools.partial(_router_kernel, tm=tm),
        grid=(t // tm,),
        in_specs=[pl.BlockSpec((tm, d), lambda i: (i, 0)),
                  pl.BlockSpec((N_EXPERTS, d), lambda i: (0, 0)),
                  pl.BlockSpec((N_EXPERTS, 1), lambda i: (0, 0))],
        out_specs=[tok, tok, tok, cnt],
        out_shape=[jax.ShapeDtypeStruct((TOP_K, t), jnp.int32), jax.ShapeDtypeStruct((TOP_K, t), F32),
                   jax.ShapeDtypeStruct((TOP_K, t), jnp.int32), jax.ShapeDtypeStruct((N_EXPERTS, LANES), F32)],
        scratch_shapes=[pltpu.VMEM((N_EXPERTS, LANES), F32)],
        compiler_params=_params("arbitrary"),
        name="router",
    )(x, router_wt, router_b)


def _gather_rows(idx_ref, n_rows, src_hbm, dst, sem):
    unroll = 8

    def body(o, carry):
        for u in range(unroll):
            r = o * unroll + u
            src = pl.multiple_of(idx_ref[0, 0, r] * ROW_CHUNKS, ROW_CHUNKS)
            pltpu.make_async_copy(src_hbm.at[pl.ds(src, ROW_CHUNKS)],
                                  dst.at[pl.ds(r * ROW_CHUNKS, ROW_CHUNKS)], sem).start()
        return carry

    lax.fori_loop(0, n_rows // unroll, body, 0)


def _wait_rows(n_rows, src_hbm, dst, sem):
    pltpu.make_async_copy(src_hbm.at[pl.ds(0, n_rows * ROW_CHUNKS)], dst, sem).wait()


def _rows_from_chunks(buf, n_rows):
    return jnp.concatenate([buf[pl.ds(c, n_rows, stride=ROW_CHUNKS), :] for c in range(ROW_CHUNKS)], axis=1)


def _dest_kernel(pstart_ref, eidx_ref, pos_ref, dest_ref):
    e = eidx_ref[...]
    acc = pos_ref[...]
    for k in range(N_EXPERTS):
        acc = acc + jnp.where(e == k, pstart_ref[k], 0)
    dest_ref[...] = acc


def _dest_rows(pstart, eidx, pos):
    full = pl.BlockSpec(eidx.shape, lambda i, ps: (0, 0))
    return pl.pallas_call(
        _dest_kernel,
        grid_spec=pltpu.PrefetchScalarGridSpec(num_scalar_prefetch=1, grid=(1,), in_specs=[full, full],
                                               out_specs=full),
        out_shape=jax.ShapeDtypeStruct(eidx.shape, jnp.int32),
        compiler_params=_params("arbitrary"),
        name="dest_rows",
    )(pstart, eidx, pos)


def _slab(ref, row):
    return ref.at[pl.ds(pl.multiple_of(row * PACK_CHUNKS, PACK_CHUNKS), PACK_CHUNKS)]


def _dispatch_kernel(pad_lo_ref, pad_hi_ref, dest_ref, x_hbm, xs_hbm, zero_ref, sem, pad_sem, *, tm, n_blk):
    i = pl.program_id(0)

    @pl.when(i == 0)
    def _():
        zero_ref[...] = jnp.zeros_like(zero_ref)
        zero_slab = zero_ref.at[pl.ds(0, PACK_CHUNKS)]
        blk_rows = ROW_BLOCK * PACK_CHUNKS

        def fill(lo, hi, copy):
            def start(r, c):
                copy(r).start()
                return c

            def wait(r, c):
                copy(r).wait()
                return c

            lax.fori_loop(lo, hi, start, 0)
            lax.fori_loop(lo, hi, wait, 0)

        def per_expert(e, carry):
            fill(pad_lo_ref[e], pad_hi_ref[e],
                 lambda r: pltpu.make_async_copy(zero_slab, _slab(xs_hbm, r), pad_sem))
            return carry

        lax.fori_loop(0, N_EXPERTS, per_expert, 0)
        fill(pad_hi_ref[N_EXPERTS - 1] // ROW_BLOCK, n_blk,
             lambda blk: pltpu.make_async_copy(
                 zero_ref, xs_hbm.at[pl.ds(pl.multiple_of(blk * blk_rows, blk_rows), blk_rows)], pad_sem))

    def per_token(j, carry):
        src = _slab(x_hbm, i * tm + j)
        for k in range(TOP_K):
            pltpu.make_async_copy(src, _slab(xs_hbm, dest_ref[0, 0, j * TOP_K + k]), sem).start()
        return carry

    lax.fori_loop(0, tm, per_token, 0)
    n_slab_rows = tm * TOP_K * PACK_CHUNKS
    pltpu.make_async_copy(x_hbm.at[pl.ds(0, n_slab_rows)], xs_hbm.at[pl.ds(0, n_slab_rows)], sem).wait()


def _dispatch(xp, dest_tok, pad_lo, pad_hi, n_rows, tm=512):
    t = xp.shape[0] // PACK_CHUNKS
    n_tiles = t // tm
    dest3 = dest_tok.reshape(n_tiles, 1, tm * TOP_K)
    grid_spec = pltpu.PrefetchScalarGridSpec(
        num_scalar_prefetch=2,
        grid=(n_tiles,),
        in_specs=[pl.BlockSpec((1, 1, tm * TOP_K), lambda i, lo, hi: (i, 0, 0), memory_space=pltpu.SMEM),
                  pl.BlockSpec(memory_space=pl.ANY)],
        out_specs=pl.BlockSpec(memory_space=pl.ANY),
        scratch_shapes=[pltpu.VMEM((ROW_BLOCK * PACK_CHUNKS, LANES), jnp.int32),
                        pltpu.SemaphoreType.DMA(()), pltpu.SemaphoreType.DMA(())],
    )
    return pl.pallas_call(
        functools.partial(_dispatch_kernel, tm=tm, n_blk=n_rows // ROW_BLOCK),
        grid_spec=grid_spec,
        out_shape=jax.ShapeDtypeStruct((n_rows * PACK_CHUNKS, LANES), jnp.int32),
        compiler_params=_params("arbitrary"),
        name="dispatch",
    )(pad_lo, pad_hi, dest3, xp)


def _expert_kernel(blk_e_ref, nused_ref, xs_ref, wgu_ref, wd_ref, y_ref):
    used = pl.program_id(0) < nused_ref[0]

    @pl.when(used)
    def _():
        xs = _unpack_rows(xs_ref, ROW_BLOCK)
        hgu = _dot(xs, wgu_ref[0])
        h = _silu(hgu[:, :EXPERT_DIM]) * hgu[:, EXPERT_DIM:]
        y = _dot(h.astype(BF16), wd_ref[0])
        for c in range(ROW_CHUNKS):
            y_ref[pl.ds(c, ROW_BLOCK, stride=ROW_CHUNKS), :] = y[:, c * LANES:(c + 1) * LANES]

    @pl.when(jnp.logical_not(used))
    def _():
        y_ref[...] = jnp.zeros_like(y_ref)


def _experts(xs, blk_e, nused, w_gu, w_down):
    n_blk = xs.shape[0] // (ROW_BLOCK * PACK_CHUNKS)
    d, f2 = w_gu.shape[1], w_gu.shape[2]
    blk = lambda i, be, nu: (jnp.minimum(i, nu[0] - 1), 0)
    wblk = lambda i, be, nu: (be[jnp.minimum(i, nu[0] - 1)], 0, 0)
    grid_spec = pltpu.PrefetchScalarGridSpec(
        num_scalar_prefetch=2,
        grid=(n_blk,),
        in_specs=[pl.BlockSpec((ROW_BLOCK * PACK_CHUNKS, LANES), blk),
                  pl.BlockSpec((1, d, f2), wblk),
                  pl.BlockSpec((1, f2 // 2, d), wblk)],
        out_specs=pl.BlockSpec((ROW_BLOCK * ROW_CHUNKS, LANES), lambda i, be, nu: (i, 0)),
    )
    return pl.pallas_call(
        _expert_kernel,
        grid_spec=grid_spec,
        out_shape=jax.ShapeDtypeStruct((n_blk * ROW_BLOCK * ROW_CHUNKS, LANES), F32),
        compiler_params=_params("arbitrary"),
        name="experts",
    )(blk_e, nused, xs, w_gu, w_down)


def _combine_kernel(dest_ref, dest_next_ref, y_hbm, gate_ref, xf_ref, xb_ref, sgu_ref, sd_ref, g_ref, b_ref,
                    of_ref, ob_ref, ybuf, sbuf, sem, *, tm, n_tiles):
    i = pl.program_id(0)
    slot = i % 2
    n_gather = TOP_K * tm

    @pl.when(i == 0)
    def _():
        _gather_rows(dest_ref, n_gather, y_hbm, ybuf.at[0], sem.at[0])

    @pl.when(i + 1 < n_tiles)
    def _():
        _gather_rows(dest_next_ref, n_gather, y_hbm, ybuf.at[1 - slot], sem.at[1 - slot])

    _wait_rows(n_gather, y_hbm, ybuf.at[slot], sem.at[slot])
    rows = tm * ROW_CHUNKS
    acc = gate_ref[:, 0:1] * ybuf[slot, 0:rows, :]
    for k in range(1, TOP_K):
        acc = acc + gate_ref[:, k:k + 1] * ybuf[slot, k * rows:(k + 1) * rows, :]
    sbuf[...] = acc
    routed = _rows_from_chunks(sbuf, tm)

    sgu = _dot(xb_ref[...], sgu_ref[...])
    sh = _silu(sgu[:, :SHARED_DIM]) * sgu[:, SHARED_DIM:]
    shared = _dot(sh.astype(BF16), sd_ref[...])
    z = ALPHA * xf_ref[...] + (routed + shared)
    y = _layer_norm(z, g_ref[...], b_ref[...])
    of_ref[...] = y
    ob_ref[...] = y.astype(BF16)


def _combine(y, dest, gate, xf, xb, sw_gu, sw_down, g, b, tm=128):
    t, d = xf.shape
    n_tiles = t // tm
    n_gather = TOP_K * tm
    dest3 = dest.reshape(TOP_K, n_tiles, tm).transpose(1, 0, 2).reshape(n_tiles, 1, n_gather)
    gate_slab = jnp.repeat(gate.T, ROW_CHUNKS, axis=0)
    row = lambda i: (i, 0)
    const = lambda i: (0, 0)
    smem_blk = lambda imap: pl.BlockSpec((1, 1, n_gather), imap, memory_space=pltpu.SMEM)
    return pl.pallas_call(
        functools.partial(_combine_kernel, tm=tm, n_tiles=n_tiles),
        grid=(n_tiles,),
        in_specs=[smem_blk(lambda i: (i, 0, 0)),
                  smem_blk(lambda i: (jnp.minimum(i + 1, n_tiles - 1), 0, 0)),
                  pl.BlockSpec(memory_space=pl.ANY),
                  pl.BlockSpec((tm * ROW_CHUNKS, TOP_K), row),
                  pl.BlockSpec((tm, d), row), pl.BlockSpec((tm, d), row),
                  pl.BlockSpec(sw_gu.shape, const), pl.BlockSpec(sw_down.shape, const),
                  pl.BlockSpec((1, d), const), pl.BlockSpec((1, d), const)],
        out_specs=[pl.BlockSpec((tm, d), row), pl.BlockSpec((tm, d), row)],
        out_shape=[jax.ShapeDtypeStruct((t, d), F32), jax.ShapeDtypeStruct((t, d), BF16)],
        scratch_shapes=[pltpu.VMEM((2, n_gather * ROW_CHUNKS, LANES), F32),
                        pltpu.VMEM((tm * ROW_CHUNKS, LANES), F32),
                        pltpu.SemaphoreType.DMA((2,))],
        compiler_params=_params("arbitrary"),
        name="combine_ln",
    )(dest3, dest3, y, gate_slab, xf, xb, sw_gu, sw_down, g, b)


def _moe_layer(xf, xb, xp, router_w, router_b, w_gu, w_down, sw_gu, sw_down, g, b):
    t, d = xf.shape
    eidx, gate, pos, cnt = _route(xf, router_w.T, router_b.reshape(N_EXPERTS, 1))

    sizes = cnt[:, 0].astype(jnp.int32)
    padded = (sizes + ROW_BLOCK - 1) // ROW_BLOCK * ROW_BLOCK
    pend = jnp.cumsum(padded)
    pstart = pend - padded
    tk = t * TOP_K
    n_rows = -(-tk // ROW_BLOCK) * ROW_BLOCK + N_EXPERTS * ROW_BLOCK
    blk_start = jnp.arange(n_rows // ROW_BLOCK, dtype=jnp.int32) * ROW_BLOCK
    blk_e = jnp.minimum(jnp.sum(pend[None, :] <= blk_start[:, None], axis=1), N_EXPERTS - 1).astype(jnp.int32)
    nused = (pend[-1:] // ROW_BLOCK).astype(jnp.int32)

    dest = _dest_rows(pstart, eidx, pos)
    xs = _dispatch(xp, dest.T, pstart + sizes, pend, n_rows)
    y = _experts(xs, blk_e, nused, w_gu, w_down)
    return _combine(y, dest, gate, xf, xb, sw_gu, sw_down, g, b)


def kernel(x, rel_bias, pool_w_in, pool_w_group, pool_scale, pool_w_out, dil_w_qkv, dil_w_out,
           conv_w_in, conv_w, conv_w_out, ln_gain, ln_bias, router_w, router_bias,
           expert_w_gu, expert_w_down, shared_w_gu, shared_w_down):
    bn, s, d = x.shape
    t = bn * s
    xf = x.reshape(t, d)
    xb = xf.astype(BF16)
    bf = lambda a: a.astype(BF16)
    ia = ib = ic = 0
    for i in range(DEPTH):
        kind = i % 3
        g1, b1 = ln_gain[i, 0].reshape(1, d), ln_bias[i, 0].reshape(1, d)
        g2, b2 = ln_gain[i, 1].reshape(1, d), ln_bias[i, 1].reshape(1, d)
        if kind == 0:
            mixed = _pool_mix(xb, bf(pool_w_in[ia]), bf(pool_w_group[ia]), pool_scale[ia].reshape(1, d), s)
            xf, xb, xp = _proj_ln(mixed, bf(pool_w_out[ia]), xf, g1, b1)
            ia += 1
        elif kind == 1:
            qkv = _matmul(xb, bf(dil_w_qkv[ib]), tm=512, tn=3 * DIL_WIDTH)
            os_, ls_ = [], []
            for g in range(len(DIL_CONFIGS)):
                o, l = _dilated_group(qkv, rel_bias[:, g * DIL_HEADS:(g + 1) * DIL_HEADS], g, bn, s)
                os_.append(o)
                ls_.append(l)
            xf, xb, xp = _attn_out(os_, ls_, bf(dil_w_out[ib]), xf, g1, b1)
            ib += 1
        else:
            v = _conv_mix(xb, bf(conv_w_in[ic]), conv_w[ic], s)
            xf, xb, xp = _proj_ln(v, bf(conv_w_out[ic]), xf, g1, b1)
            ic += 1
        xf, xb = _moe_layer(xf, xb, xp, router_w[i], router_bias[i], bf(expert_w_gu[i]), bf(expert_w_down[i]),
                            bf(shared_w_gu[i]), bf(shared_w_down[i]), g2, b2)
    return xf.reshape(bn, s, d)
```

```python
import functools
import math

import jax
import jax.numpy as jnp
from jax import lax
from jax.experimental import pallas as pl
from jax.experimental.pallas import tpu as pltpu

F32 = jnp.float32
BF16 = jnp.bfloat16

DEPTH = 4
D_MODEL = 2048
POOL_WINDOWS = (2, 4, 8, 16)
POOL_GROUP_DIM = D_MODEL // len(POOL_WINDOWS)
POOL_HALO = 16
DIL_CONFIGS = ((128, 1), (512, 4), (2048, 16))
HEAD_DIM = 64
DIL_HEADS = 8
DIL_WIDTH = DIL_HEADS * HEAD_DIM
DIL_BLOCK = 128
N_BUCKETS = 32
MAX_DISTANCE = 2048
CONV_WIDTH = 3
CONV_HALO = 8
N_EXPERTS = 64
TOP_K = 8
N_EXPERT_GROUPS = 8
GROUP_SIZE = N_EXPERTS // N_EXPERT_GROUPS
TOPK_GROUPS = 4
EXPERT_DIM = 128
SHARED_DIM = 256
ROUTED_SCALE = 2.5
ROW_BLOCK = 256
ALPHA = (2.0 * DEPTH) ** 0.25
LN_EPS = 1e-5
NEG = -1e30

LANES = 128
HIGH_HALF = -65536
VMEM_LIMIT = 52 * 1024 * 1024

_NT = (((1,), (1,)), ((), ()))


def _dot(a, b):
    return jnp.dot(a, b, preferred_element_type=F32)


def _params(*sem):
    return pltpu.CompilerParams(dimension_semantics=sem, vmem_limit_bytes=VMEM_LIMIT)


def _layer_norm(z, g, b):
    mu = jnp.mean(z, axis=-1, keepdims=True)
    zc = z - mu
    var = jnp.mean(zc * zc, axis=-1, keepdims=True)
    return zc * lax.rsqrt(var + LN_EPS) * g + b


def _silu(v):
    return v / (1.0 + jnp.exp(-v))


def _pack_rows(y):
    half = D_MODEL // 2
    lo = lax.bitcast_convert_type(y[:, :half].astype(BF16).astype(F32), jnp.int32)
    hi = lax.bitcast_convert_type(y[:, half:].astype(BF16).astype(F32), jnp.int32)
    return (hi & HIGH_HALF) | lax.shift_right_logical(lo, 16)


def _unpack_rows(p):
    lo = lax.bitcast_convert_type(lax.shift_left(p, 16), F32)
    hi = lax.bitcast_convert_type(p & HIGH_HALF, F32)
    return jnp.concatenate([lo, hi], axis=1).astype(BF16)


def _store_ln_outputs(y, of_ref, ob_ref, op_ref):
    of_ref[...] = y
    ob_ref[...] = y.astype(BF16)
    op_ref[...] = _pack_rows(y)


def _ln_out_specs(t, d, tm):
    row = lambda i: (i, 0)
    specs = [pl.BlockSpec((tm, d), row), pl.BlockSpec((tm, d), row), pl.BlockSpec((tm, d // 2), row)]
    shapes = [jax.ShapeDtypeStruct((t, d), F32), jax.ShapeDtypeStruct((t, d), BF16),
              jax.ShapeDtypeStruct((t, d // 2), jnp.int32)]
    return specs, shapes


def _mm_kernel(x_ref, w_ref, o_ref):
    o_ref[...] = _dot(x_ref[...], w_ref[...]).astype(o_ref.dtype)


def _matmul(x, w, tm, tn):
    t, k = x.shape
    n = w.shape[1]
    return pl.pallas_call(
        _mm_kernel,
        grid=(n // tn, t // tm),
        in_specs=[pl.BlockSpec((tm, k), lambda j, i: (i, 0)),
                  pl.BlockSpec((k, tn), lambda j, i: (0, j))],
        out_specs=pl.BlockSpec((tm, tn), lambda j, i: (i, j)),
        out_shape=jax.ShapeDtypeStruct((t, n), BF16),
        compiler_params=_params("parallel", "arbitrary"),
        name="matmul",
    )(x, w)


def _proj_ln_kernel(a_ref, w_ref, x_ref, g_ref, b_ref, of_ref, ob_ref, op_ref):
    z = ALPHA * x_ref[...] + _dot(a_ref[...], w_ref[...])
    _store_ln_outputs(_layer_norm(z, g_ref[...], b_ref[...]), of_ref, ob_ref, op_ref)


def _proj_ln(a, w, x, g, b, tm=256):
    t, k = a.shape
    d = w.shape[1]
    row = lambda i: (i, 0)
    const = lambda i: (0, 0)
    out_specs, out_shape = _ln_out_specs(t, d, tm)
    return pl.pallas_call(
        _proj_ln_kernel,
        grid=(t // tm,),
        in_specs=[pl.BlockSpec((tm, k), row), pl.BlockSpec((k, d), const),
                  pl.BlockSpec((tm, d), row), pl.BlockSpec((1, d), const),
                  pl.BlockSpec((1, d), const)],
        out_specs=out_specs,
        out_shape=out_shape,
        compiler_params=_params("arbitrary"),
        name="proj_ln",
    )(a, w, x, g, b)


def _pool_kernel(x_ref, win_ref, wg_ref, sc_ref, o_ref, carry_ref, *, tm, tiles_per_seq):
    i = pl.program_id(0)
    seq_tile = i % tiles_per_seq

    @pl.when(seq_tile == 0)
    def _():
        carry_ref[...] = jnp.zeros_like(carry_ref)

    xb = x_ref[...]
    pos = lax.broadcasted_iota(jnp.int32, (tm, 1), 0) + seq_tile * tm
    gd = POOL_GROUP_DIM
    for g, w in enumerate(POOL_WINDOWS):
        cols = slice(g * gd, (g + 1) * gd)
        u = _dot(xb, win_ref[:, cols])
        s = jnp.concatenate([carry_ref[g], u], axis=0)
        sh = 1
        while sh < w:
            s = s + pltpu.roll(s, sh, axis=0)
            sh *= 2
        cnt = jnp.minimum(pos + 1, w).astype(F32)
        pooled = s[POOL_HALO:, :] / cnt - u
        carry_ref[g] = u[tm - POOL_HALO:, :]
        mixed = _dot(pooled.astype(BF16), wg_ref[g]) * sc_ref[:, cols]
        o_ref[:, cols] = mixed.astype(BF16)


def _pool_mix(xb, w_in, w_group, scale, seq_len, tm=256):
    t, d = xb.shape
    gd = POOL_GROUP_DIM
    ng = len(POOL_WINDOWS)
    return pl.pallas_call(
        functools.partial(_pool_kernel, tm=tm, tiles_per_seq=seq_len // tm),
        grid=(t // tm,),
        in_specs=[pl.BlockSpec((tm, d), lambda i: (i, 0)),
                  pl.BlockSpec((d, d), lambda i: (0, 0)),
                  pl.BlockSpec((ng, gd, gd), lambda i: (0, 0, 0)),
                  pl.BlockSpec((1, d), lambda i: (0, 0))],
        out_specs=pl.BlockSpec((tm, d), lambda i: (i, 0)),
        out_shape=jax.ShapeDtypeStruct((t, d), BF16),
        scratch_shapes=[pltpu.VMEM((ng, POOL_HALO, gd), F32)],
        compiler_params=_params("arbitrary"),
        name="pool_mix",
    )(xb, w_in, w_group, scale)


def _conv_kernel(x_ref, wb_ref, wc_ref, wh_ref, cw_ref, o_ref, carry_ref, *, tm, tiles_per_seq):
    i = pl.program_id(1)

    @pl.when(i % tiles_per_seq == 0)
    def _():
        carry_ref[...] = jnp.zeros_like(carry_ref)

    xb = x_ref[...]
    b_gate = _dot(xb, wb_ref[...])
    u = _dot(xb, wc_ref[...]) * _dot(xb, wh_ref[...])
    ext = jnp.concatenate([carry_ref[...], u], axis=0)
    y = cw_ref[CONV_WIDTH - 1:CONV_WIDTH, :] * u
    for back in range(1, CONV_WIDTH):
        tap = CONV_WIDTH - 1 - back
        y = y + cw_ref[tap:tap + 1, :] * pltpu.roll(ext, back, axis=0)[CONV_HALO:, :]
    carry_ref[...] = u[tm - CONV_HALO:, :]
    o_ref[...] = (b_gate * y).astype(BF16)


def _conv_mix(xb, w_in, conv_w, seq_len, tm=512, tn=512):
    t, d = xb.shape
    nj = d // tn
    return pl.pallas_call(
        functools.partial(_conv_kernel, tm=tm, tiles_per_seq=seq_len // tm),
        grid=(nj, t // tm),
        in_specs=[pl.BlockSpec((tm, d), lambda j, i: (i, 0)),
                  pl.BlockSpec((d, tn), lambda j, i: (0, j)),
                  pl.BlockSpec((d, tn), lambda j, i: (0, nj + j)),
                  pl.BlockSpec((d, tn), lambda j, i: (0, 2 * nj + j)),
                  pl.BlockSpec((CONV_WIDTH, tn), lambda j, i: (0, j))],
        out_specs=pl.BlockSpec((tm, tn), lambda j, i: (i, j)),
        out_shape=jax.ShapeDtypeStruct((t, d), BF16),
        scratch_shapes=[pltpu.VMEM((CONV_HALO, tn), F32)],
        compiler_params=_params("parallel", "arbitrary"),
        name="conv_mix",
    )(xb, w_in, w_in, w_in, conv_w)


def _t5_bucket(dist):
    max_exact = N_BUCKETS // 2
    is_small = dist < max_exact
    distf = jnp.maximum(dist, 1).astype(F32)
    large = max_exact + (jnp.log(distf / max_exact) / math.log(MAX_DISTANCE / max_exact)
                         * (N_BUCKETS - max_exact)).astype(jnp.int32)
    large = jnp.minimum(large, N_BUCKETS - 1)
    return jnp.where(is_small, dist, large)


def _attn_bias(bias_tab, window, dil, has_prev):
    steps = window // dil
    q_loc = jnp.arange(DIL_BLOCK)[:, None]
    k_loc = jnp.arange(2 * DIL_BLOCK)[None, :]
    rel = q_loc + DIL_BLOCK - k_loc
    band = (rel >= 0) & (rel <= steps)
    bucket = _t5_bucket(jnp.maximum(rel, 0) * dil)
    tab = bias_tab.astype(F32).T[:, :, None, None]
    bias = sum(jnp.where(bucket == b, tab[:, b], 0.0) for b in range(N_BUCKETS))
    bias = jnp.where(band[None], bias, NEG)
    return bias if has_prev else bias[:, :, DIL_BLOCK:]


def _attn_kernel(*refs, has_prev):
    if has_prev:
        q_ref, kc_ref, kp_ref, vc_ref, vp_ref, bias_ref, o_ref, l_ref = refs
    else:
        q_ref, kc_ref, vc_ref, bias_ref, o_ref, l_ref = refs
    n = pl.program_id(1)
    q = q_ref[0]
    kc = kc_ref[0]
    vc = vc_ref[0]
    if has_prev:
        kp = kp_ref[0]
        vp = vp_ref[0]
        col = lax.broadcasted_iota(jnp.int32, (DIL_BLOCK, 2 * DIL_BLOCK), 1)
        keep = (col >= DIL_BLOCK) | (n > 0)
    outs, lses = [], []
    for h in range(DIL_HEADS):
        sl = slice(h * HEAD_DIM, (h + 1) * HEAD_DIM)
        if has_prev:
            kh = jnp.concatenate([kp[:, sl], kc[:, sl]], axis=0)
            vh = jnp.concatenate([vp[:, sl], vc[:, sl]], axis=0)
        else:
            kh, vh = kc[:, sl], vc[:, sl]
        s = lax.dot_general(q[:, sl], kh, _NT, preferred_element_type=F32) * (HEAD_DIM ** -0.5) + bias_ref[h]
        if has_prev:
            s = jnp.where(keep, s, NEG)
        m = jnp.max(s, axis=-1, keepdims=True)
        p = jnp.exp(s - m)
        l = jnp.sum(p, axis=-1, keepdims=True)
        outs.append(_dot(p.astype(BF16), vh) / l)
        lses.append(jnp.broadcast_to(m + jnp.log(l), (DIL_BLOCK, HEAD_DIM)))
    o_ref[0] = jnp.concatenate(outs, axis=1)
    l_ref[0] = jnp.concatenate(lses, axis=1)


def _dilated_group(qkv, bias_tab, g, batch, seq_len):
    window, dil = DIL_CONFIGS[g]
    span = DIL_BLOCK * dil
    nb = seq_len // span
    has_prev = nb > 1
    ncol = qkv.shape[1] // DIL_WIDTH
    view = qkv.reshape(batch, seq_len // dil, dil * qkv.shape[1])
    blk = (1, DIL_BLOCK, DIL_WIDTH)

    def cur(kind):
        return pl.BlockSpec(blk, lambda b, n, r: (b, n, r * ncol + 3 * g + kind))

    def prev(kind):
        return pl.BlockSpec(blk, lambda b, n, r: (b, jnp.maximum(n - 1, 0), r * ncol + 3 * g + kind))

    bias = _attn_bias(bias_tab, window, dil, has_prev)
    bias_spec = pl.BlockSpec(bias.shape, lambda b, n, r: (0, 0, 0))
    if has_prev:
        in_specs = [cur(0), cur(1), prev(1), cur(2), prev(2), bias_spec]
        args = (view,) * 5 + (bias,)
    else:
        in_specs = [cur(0), cur(1), cur(2), bias_spec]
        args = (view,) * 3 + (bias,)
    out_spec = pl.BlockSpec(blk, lambda b, n, r: (b, n, r))
    out_sds = jax.ShapeDtypeStruct((batch, seq_len // dil, dil * DIL_WIDTH), F32)
    o, l = pl.pallas_call(
        functools.partial(_attn_kernel, has_prev=has_prev),
        grid=(batch, nb, dil),
        in_specs=in_specs,
        out_specs=[out_spec, out_spec],
        out_shape=[out_sds, out_sds],
        compiler_params=_params("parallel", "arbitrary", "arbitrary"),
        name=f"dilated_attn_g{g}",
    )(*args)
    t = batch * seq_len
    return o.reshape(t, DIL_WIDTH), l.reshape(t, DIL_WIDTH)


def _attn_out_kernel(o0, o1, o2, l0, l1, l2, w_ref, x_ref, g_ref, b_ref, of_ref, ob_ref, op_ref):
    a0, a1, a2 = l0[...], l1[...], l2[...]
    m = jnp.maximum(jnp.maximum(a0, a1), a2)
    e0, e1, e2 = jnp.exp(a0 - m), jnp.exp(a1 - m), jnp.exp(a2 - m)
    o = (e0 * o0[...] + e1 * o1[...] + e2 * o2[...]) / (e0 + e1 + e2)
    z = ALPHA * x_ref[...] + _dot(o.astype(BF16), w_ref[...])
    _store_ln_outputs(_layer_norm(z, g_ref[...], b_ref[...]), of_ref, ob_ref, op_ref)


def _attn_out(os_, ls_, w, x, g, b, tm=256):
    t, d = x.shape
    k = w.shape[0]
    row = lambda i: (i, 0)
    const = lambda i: (0, 0)
    part = pl.BlockSpec((tm, k), row)
    out_specs, out_shape = _ln_out_specs(t, d, tm)
    return pl.pallas_call(
        _attn_out_kernel,
        grid=(t // tm,),
        in_specs=[part] * 6 + [pl.BlockSpec((k, d), const), pl.BlockSpec((tm, d), row),
                               pl.BlockSpec((1, d), const), pl.BlockSpec((1, d), const)],
        out_specs=out_specs,
        out_shape=out_shape,
        compiler_params=_params("arbitrary"),
        name="attn_out_ln",
    )(*os_, *ls_, w, x, g, b)


def _router_kernel(x_ref, wt_ref, b_ref, eidx_ref, gate_ref, pos_ref, cnt_ref, base_ref, *, tm):
    @pl.when(pl.program_id(0) == 0)
    def _():
        base_ref[...] = jnp.zeros_like(base_ref)

    x = x_ref[...]
    w = wt_ref[...]
    xh = x.astype(BF16)
    xl = (x - xh.astype(F32)).astype(BF16)
    wh = w.astype(BF16)
    wl = (w - wh.astype(F32)).astype(BF16)
    dg = functools.partial(lax.dot_general, dimension_numbers=_NT, preferred_element_type=F32)
    logits = dg(wh, xh) + (dg(wh, xl) + dg(wl, xh))
    scores = 1.0 / (1.0 + jnp.exp(-logits))
    choice = scores + b_ref[...]

    iota_g = lax.broadcasted_iota(jnp.int32, (GROUP_SIZE, tm), 0)
    gs = []
    for gi in range(N_EXPERT_GROUPS):
        cg = choice[gi * GROUP_SIZE:(gi + 1) * GROUP_SIZE, :]
        t1 = jnp.max(cg, axis=0, keepdims=True)
        i1 = jnp.min(jnp.where(cg == t1, iota_g, GROUP_SIZE), axis=0, keepdims=True)
        t2 = jnp.max(jnp.where(iota_g == i1, -jnp.inf, cg), axis=0, keepdims=True)
        gs.append(t1 + t2)
    masked = []
    for gi in range(N_EXPERT_GROUPS):
        rank = jnp.zeros((1, tm), jnp.int32)
        for gj in range(N_EXPERT_GROUPS):
            if gj != gi:
                beats = (gs[gj] >= gs[gi]) if gj < gi else (gs[gj] > gs[gi])
                rank = rank + beats.astype(jnp.int32)
        masked.append(jnp.where(rank < TOPK_GROUPS, choice[gi * GROUP_SIZE:(gi + 1) * GROUP_SIZE, :], -jnp.inf))
    c = jnp.concatenate(masked, axis=0)

    iota_e = lax.broadcasted_iota(jnp.int32, (N_EXPERTS, tm), 0)
    picks, gates = [], []
    sel = jnp.zeros((N_EXPERTS, tm), F32)
    for _ in range(TOP_K):
        m = jnp.max(c, axis=0, keepdims=True)
        idx = jnp.min(jnp.where(c == m, iota_e, N_EXPERTS), axis=0, keepdims=True)
        hit = iota_e == idx
        picks.append(hit)
        gates.append(jnp.sum(jnp.where(hit, scores, 0.0), axis=0, keepdims=True))
        sel = jnp.where(hit, 1.0, sel)
        c = jnp.where(hit, -jnp.inf, c)
    gsum = gates[0]
    for gk in gates[1:]:
        gsum = gsum + gk

    before = (lax.broadcasted_iota(jnp.int32, (tm, tm), 0) < lax.broadcasted_iota(jnp.int32, (tm, tm), 1))
    rank_in_tile = _dot(sel.astype(BF16), before.astype(BF16))
    base = base_ref[...]
    posfull = base[:, 0:1] + rank_in_tile
    efull = iota_e.astype(F32)
    for k in range(TOP_K):
        hit = picks[k]
        eidx_ref[k:k + 1, :] = jnp.sum(jnp.where(hit, efull, 0.0), axis=0, keepdims=True).astype(jnp.int32)
        pos_ref[k:k + 1, :] = jnp.sum(jnp.where(hit, posfull, 0.0), axis=0, keepdims=True).astype(jnp.int32)
        gate_ref[k:k + 1, :] = gates[k] / gsum * ROUTED_SCALE
    new_base = base + jnp.sum(sel, axis=1, keepdims=True)
    base_ref[...] = new_base
    cnt_ref[...] = new_base


def _route(x, router_wt, router_b, tm=512):
    t, d = x.shape
    tok = pl.BlockSpec((TOP_K, tm), lambda i: (0, i))
    cnt = pl.BlockSpec((N_EXPERTS, LANES), lambda i: (0, 0))
    return pl.pallas_call(
        functools.partial(_router_kernel, tm=tm),
        grid=(t // tm,),
        in_specs=[pl.BlockSpec((tm, d), lambda i: (i, 0)),
                  pl.BlockSpec((N_EXPERTS, d), lambda i: (0, 0)),
                  pl.BlockSpec((N_EXPERTS, 1), lambda i: (0, 0))],
        out_specs=[tok, tok, tok, cnt],
        out_shape=[jax.ShapeDtypeStruct((TOP_K, t), jnp.int32), jax.ShapeDtypeStruct((TOP_K, t), F32),
                   jax.ShapeDtypeStruct((TOP_K, t), jnp.int32), jax.ShapeDtypeStruct((N_EXPERTS, LANES), F32)],
        scratch_shapes=[pltpu.VMEM((N_EXPERTS, LANES), F32)],
        compiler_params=_params("arbitrary"),
        name="router",
    )(x, router_wt, router_b)


def _row(ref, r):
    return ref.at[pl.ds(r, 1)]


def _dest_kernel(pstart_ref, eidx_ref, pos_ref, dest_ref):
    e = eidx_ref[...]
    acc = pos_ref[...]
    for k in range(N_EXPERTS):
        acc = acc + jnp.where(e == k, pstart_ref[k], 0)
    dest_ref[...] = acc


def _dest_rows(pstart, eidx, pos):
    full = pl.BlockSpec(eidx.shape, lambda i, ps: (0, 0))
    return pl.pallas_call(
        _dest_kernel,
        grid_spec=pltpu.PrefetchScalarGridSpec(num_scalar_prefetch=1, grid=(1,), in_specs=[full, full],
                                               out_specs=full),
        out_shape=jax.ShapeDtypeStruct(eidx.shape, jnp.int32),
        compiler_params=_params("arbitrary"),
        name="dest_rows",
    )(pstart, eidx, pos)


def _dispatch_kernel(pad_lo_ref, pad_hi_ref, dest_ref, x_ref, xs_hbm, zero_ref, sem, pad_sem, *, tm, n_blk):
    i = pl.program_id(0)

    @pl.when(i == 0)
    def _():
        zero_ref[...] = jnp.zeros_like(zero_ref)

        def fill(lo, hi, copy):
            def start(r, c):
                copy(r).start()
                return c

            def wait(r, c):
                copy(r).wait()
                return c

            lax.fori_loop(lo, hi, start, 0)
            lax.fori_loop(lo, hi, wait, 0)

        def per_expert(e, carry):
            fill(pad_lo_ref[e], pad_hi_ref[e],
                 lambda r: pltpu.make_async_copy(_row(zero_ref, 0), _row(xs_hbm, r), pad_sem))
            return carry

        lax.fori_loop(0, N_EXPERTS, per_expert, 0)
        fill(pad_hi_ref[N_EXPERTS - 1] // ROW_BLOCK, n_blk,
             lambda blk: pltpu.make_async_copy(
                 zero_ref, xs_hbm.at[pl.ds(pl.multiple_of(blk * ROW_BLOCK, ROW_BLOCK), ROW_BLOCK)], pad_sem))

    def per_token(j, carry):
        src = _row(x_ref, j)
        for k in range(TOP_K):
            pltpu.make_async_copy(src, _row(xs_hbm, dest_ref[0, 0, j * TOP_K + k]), sem).start()
        return carry

    lax.fori_loop(0, tm, per_token, 0)
    for k in range(TOP_K):
        pltpu.make_async_copy(x_ref, xs_hbm.at[pl.ds(0, tm)], sem).wait()


def _dispatch(xp, dest_tok, pad_lo, pad_hi, n_rows, tm=512):
    t, half = xp.shape
    n_tiles = t // tm
    dest3 = dest_tok.reshape(n_tiles, 1, tm * TOP_K)
    grid_spec = pltpu.PrefetchScalarGridSpec(
        num_scalar_prefetch=2,
        grid=(n_tiles,),
        in_specs=[pl.BlockSpec((1, 1, tm * TOP_K), lambda i, lo, hi: (i, 0, 0), memory_space=pltpu.SMEM),
                  pl.BlockSpec((tm, half), lambda i, lo, hi: (i, 0))],
        out_specs=pl.BlockSpec(memory_space=pl.ANY),
        scratch_shapes=[pltpu.VMEM((ROW_BLOCK, half), jnp.int32),
                        pltpu.SemaphoreType.DMA(()), pltpu.SemaphoreType.DMA(())],
    )
    return pl.pallas_call(
        functools.partial(_dispatch_kernel, tm=tm, n_blk=n_rows // ROW_BLOCK),
        grid_spec=grid_spec,
        out_shape=jax.ShapeDtypeStruct((n_rows, half), jnp.int32),
        compiler_params=_params("arbitrary"),
        name="dispatch",
    )(pad_lo, pad_hi, dest3, xp)


def _expert_kernel(blk_e_ref, nused_ref, xs_ref, wgu_ref, wd_ref, y_ref):
    used = pl.program_id(0) < nused_ref[0]

    @pl.when(used)
    def _():
        hgu = _dot(_unpack_rows(xs_ref[...]), wgu_ref[0])
        h = _silu(hgu[:, :EXPERT_DIM]) * hgu[:, EXPERT_DIM:]
        y_ref[...] = _dot(h.astype(BF16), wd_ref[0])

    @pl.when(jnp.logical_not(used))
    def _():
        y_ref[...] = jnp.zeros_like(y_ref)


def _experts(xs, blk_e, nused, w_gu, w_down):
    n_rows, half = xs.shape
    d, f2 = w_gu.shape[1], w_gu.shape[2]
    blk = lambda i, be, nu: (jnp.minimum(i, nu[0] - 1), 0)
    wblk = lambda i, be, nu: (be[jnp.minimum(i, nu[0] - 1)], 0, 0)
    grid_spec = pltpu.PrefetchScalarGridSpec(
        num_scalar_prefetch=2,
        grid=(n_rows // ROW_BLOCK,),
        in_specs=[pl.BlockSpec((ROW_BLOCK, half), blk),
                  pl.BlockSpec((1, d, f2), wblk),
                  pl.BlockSpec((1, f2 // 2, d), wblk)],
        out_specs=pl.BlockSpec((ROW_BLOCK, d), lambda i, be, nu: (i, 0)),
    )
    return pl.pallas_call(
        _expert_kernel,
        grid_spec=grid_spec,
        out_shape=jax.ShapeDtypeStruct((n_rows, d), F32),
        compiler_params=_params("arbitrary"),
        name="experts",
    )(blk_e, nused, xs, w_gu, w_down)


def _combine_kernel(dest_ref, dest_next_ref, y_hbm, gate_ref, xf_ref, xb_ref, sgu_ref, sd_ref, g_ref, b_ref,
                    of_ref, ob_ref, ybuf, sem, *, tm, n_tiles):
    i = pl.program_id(0)
    slot = i % 2

    def gather(idx_ref, s):
        def per_token(j, carry):
            for k in range(TOP_K):
                pltpu.make_async_copy(_row(y_hbm, idx_ref[0, 0, j * TOP_K + k]),
                                      ybuf.at[s, k, pl.ds(j, 1)], sem.at[s]).start()
            return carry

        lax.fori_loop(0, tm, per_token, 0)

    @pl.when(i == 0)
    def _():
        gather(dest_ref, 0)

    @pl.when(i + 1 < n_tiles)
    def _():
        gather(dest_next_ref, 1 - slot)

    for k in range(TOP_K):
        pltpu.make_async_copy(y_hbm.at[pl.ds(0, tm)], ybuf.at[slot, k], sem.at[slot]).wait()
    gates = gate_ref[...]
    routed = gates[:, 0:1] * ybuf[slot, 0]
    for k in range(1, TOP_K):
        routed = routed + gates[:, k:k + 1] * ybuf[slot, k]

    sgu = _dot(xb_ref[...], sgu_ref[...])
    sh = _silu(sgu[:, :SHARED_DIM]) * sgu[:, SHARED_DIM:]
    shared = _dot(sh.astype(BF16), sd_ref[...])
    z = ALPHA * xf_ref[...] + (routed + shared)
    y = _layer_norm(z, g_ref[...], b_ref[...])
    of_ref[...] = y
    ob_ref[...] = y.astype(BF16)


def _combine(y, dest_tok, gate_tok, xf, xb, sw_gu, sw_down, g, b, tm=128):
    t, d = xf.shape
    n_tiles = t // tm
    n_gather = TOP_K * tm
    dest3 = dest_tok.reshape(n_tiles, 1, n_gather)
    row = lambda i: (i, 0)
    const = lambda i: (0, 0)
    smem_blk = lambda imap: pl.BlockSpec((1, 1, n_gather), imap, memory_space=pltpu.SMEM)
    return pl.pallas_call(
        functools.partial(_combine_kernel, tm=tm, n_tiles=n_tiles),
        grid=(n_tiles,),
        in_specs=[smem_blk(lambda i: (i, 0, 0)),
                  smem_blk(lambda i: (jnp.minimum(i + 1, n_tiles - 1), 0, 0)),
                  pl.BlockSpec(memory_space=pl.ANY),
                  pl.BlockSpec((tm, TOP_K), row),
                  pl.BlockSpec((tm, d), row), pl.BlockSpec((tm, d), row),
                  pl.BlockSpec(sw_gu.shape, const), pl.BlockSpec(sw_down.shape, const),
                  pl.BlockSpec((1, d), const), pl.BlockSpec((1, d), const)],
        out_specs=[pl.BlockSpec((tm, d), row), pl.BlockSpec((tm, d), row)],
        out_shape=[jax.ShapeDtypeStruct((t, d), F32), jax.ShapeDtypeStruct((t, d), BF16)],
        scratch_shapes=[pltpu.VMEM((2, TOP_K, tm, d), F32),
                        pltpu.SemaphoreType.DMA((2,))],
        compiler_params=_params("arbitrary"),
        name="combine_ln",
    )(dest3, dest3, y, gate_tok, xf, xb, sw_gu, sw_down, g, b)


def _moe_layer(xf, xb, xp, router_w, router_b, w_gu, w_down, sw_gu, sw_down, g, b):
    t, d = xf.shape
    eidx, gate, pos, cnt = _route(xf, router_w.T, router_b.reshape(N_EXPERTS, 1))

    sizes = cnt[:, 0].astype(jnp.int32)
    padded = (sizes + ROW_BLOCK - 1) // ROW_BLOCK * ROW_BLOCK
    pend = jnp.cumsum(padded)
    pstart = pend - padded
    tk = t * TOP_K
    n_rows = -(-tk // ROW_BLOCK) * ROW_BLOCK + N_EXPERTS * ROW_BLOCK
    blk_start = jnp.arange(n_rows // ROW_BLOCK, dtype=jnp.int32) * ROW_BLOCK
    blk_e = jnp.minimum(jnp.sum(pend[None, :] <= blk_start[:, None], axis=1), N_EXPERTS - 1).astype(jnp.int32)
    nused = (pend[-1:] // ROW_BLOCK).astype(jnp.int32)

    dest_tok = _dest_rows(pstart, eidx, pos).T
    xs = _dispatch(xp, dest_tok, pstart + sizes, pend, n_rows)
    y = _experts(xs, blk_e, nused, w_gu, w_down)
    return _combine(y, dest_tok, gate.T, xf, xb, sw_gu, sw_down, g, b)


def kernel(x, rel_bias, pool_w_in, pool_w_group, pool_scale, pool_w_out, dil_w_qkv, dil_w_out,
           conv_w_in, conv_w, conv_w_out, ln_gain, ln_bias, router_w, router_bias,
           expert_w_gu, expert_w_down, shared_w_gu, shared_w_down):
    bn, s, d = x.shape
    t = bn * s
    xf = x.reshape(t, d)
    xb = xf.astype(BF16)
    bf = lambda a: a.astype(BF16)
    ia = ib = ic = 0
    for i in range(DEPTH):
        kind = i % 3
        g1, b1 = ln_gain[i, 0].reshape(1, d), ln_bias[i, 0].reshape(1, d)
        g2, b2 = ln_gain[i, 1].reshape(1, d), ln_bias[i, 1].reshape(1, d)
        if kind == 0:
            mixed = _pool_mix(xb, bf(pool_w_in[ia]), bf(pool_w_group[ia]), pool_scale[ia].reshape(1, d), s)
            xf, xb, xp = _proj_ln(mixed, bf(pool_w_out[ia]), xf, g1, b1)
            ia += 1
        elif kind == 1:
            qkv = _matmul(xb, bf(dil_w_qkv[ib]), tm=512, tn=3 * DIL_WIDTH)
            os_, ls_ = [], []
            for g in range(len(DIL_CONFIGS)):
                o, l = _dilated_group(qkv, rel_bias[:, g * DIL_HEADS:(g + 1) * DIL_HEADS], g, bn, s)
                os_.append(o)
                ls_.append(l)
            xf, xb, xp = _attn_out(os_, ls_, bf(dil_w_out[ib]), xf, g1, b1)
            ib += 1
        else:
            v = _conv_mix(xb, bf(conv_w_in[ic]), conv_w[ic], s)
            xf, xb, xp = _proj_ln(v, bf(conv_w_out[ic]), xf, g1, b1)
            ic += 1
        xf, xb = _moe_layer(xf, xb, xp, router_w[i], router_bias[i], bf(expert_w_gu[i]), bf(expert_w_down[i]),
                            bf(shared_w_gu[i]), bf(shared_w_down[i]), g2, b2)
    return xf.reshape(bn, s, d)
```

```python
import functools
import math

import jax
import jax.numpy as jnp
from jax import lax
from jax.experimental import pallas as pl
from jax.experimental.pallas import tpu as pltpu

F32 = jnp.float32
BF16 = jnp.bfloat16

DEPTH = 4
D_MODEL = 2048
POOL_WINDOWS = (2, 4, 8, 16)
POOL_GROUP_DIM = D_MODEL // len(POOL_WINDOWS)
POOL_HALO = 16
DIL_CONFIGS = ((128, 1), (512, 4), (2048, 16))
HEAD_DIM = 64
DIL_HEADS = 8
DIL_WIDTH = DIL_HEADS * HEAD_DIM
DIL_BLOCK = 128
N_BUCKETS = 32
MAX_DISTANCE = 2048
CONV_WIDTH = 3
CONV_HALO = 8
N_EXPERTS = 64
TOP_K = 8
N_EXPERT_GROUPS = 8
GROUP_SIZE = N_EXPERTS // N_EXPERT_GROUPS
TOPK_GROUPS = 4
EXPERT_DIM = 128
SHARED_DIM = 256
ROUTED_SCALE = 2.5
ROW_BLOCK = 256
ALPHA = (2.0 * DEPTH) ** 0.25
LN_EPS = 1e-5
NEG = -1e30

LANES = 128
HIGH_HALF = -65536
VMEM_LIMIT = 52 * 1024 * 1024

_NT = (((1,), (1,)), ((), ()))


def _dot(a, b):
    return jnp.dot(a, b, preferred_element_type=F32)


def _params(*sem):
    return pltpu.CompilerParams(dimension_semantics=sem, vmem_limit_bytes=VMEM_LIMIT)


def _layer_norm(z, g, b):
    mu = jnp.mean(z, axis=-1, keepdims=True)
    zc = z - mu
    var = jnp.mean(zc * zc, axis=-1, keepdims=True)
    return zc * lax.rsqrt(var + LN_EPS) * g + b


def _silu(v):
    return v / (1.0 + jnp.exp(-v))


def _pack_rows(y):
    half = D_MODEL // 2
    lo = lax.bitcast_convert_type(y[:, :half].astype(BF16).astype(F32), jnp.int32)
    hi = lax.bitcast_convert_type(y[:, half:].astype(BF16).astype(F32), jnp.int32)
    return (hi & HIGH_HALF) | lax.shift_right_logical(lo, 16)


def _unpack_rows(p):
    lo = lax.bitcast_convert_type(lax.shift_left(p, 16), F32)
    hi = lax.bitcast_convert_type(p & HIGH_HALF, F32)
    return jnp.concatenate([lo, hi], axis=1).astype(BF16)


def _store_ln_outputs(y, of_ref, ob_ref, op_ref):
    of_ref[...] = y
    ob_ref[...] = y.astype(BF16)
    op_ref[...] = _pack_rows(y)


def _ln_out_specs(t, d, tm):
    row = lambda i: (i, 0)
    specs = [pl.BlockSpec((tm, d), row), pl.BlockSpec((tm, d), row), pl.BlockSpec((tm, d // 2), row)]
    shapes = [jax.ShapeDtypeStruct((t, d), F32), jax.ShapeDtypeStruct((t, d), BF16),
              jax.ShapeDtypeStruct((t, d // 2), jnp.int32)]
    return specs, shapes


def _mm_kernel(x_ref, w_ref, o_ref):
    o_ref[...] = _dot(x_ref[...], w_ref[...]).astype(o_ref.dtype)


def _matmul(x, w, tm, tn):
    t, k = x.shape
    n = w.shape[1]
    return pl.pallas_call(
        _mm_kernel,
        grid=(n // tn, t // tm),
        in_specs=[pl.BlockSpec((tm, k), lambda j, i: (i, 0)),
                  pl.BlockSpec((k, tn), lambda j, i: (0, j))],
        out_specs=pl.BlockSpec((tm, tn), lambda j, i: (i, j)),
        out_shape=jax.ShapeDtypeStruct((t, n), BF16),
        compiler_params=_params("parallel", "arbitrary"),
        name="matmul",
    )(x, w)


def _proj_ln_kernel(a_ref, w_ref, x_ref, g_ref, b_ref, of_ref, ob_ref, op_ref):
    z = ALPHA * x_ref[...] + _dot(a_ref[...], w_ref[...])
    _store_ln_outputs(_layer_norm(z, g_ref[...], b_ref[...]), of_ref, ob_ref, op_ref)


def _proj_ln(a, w, x, g, b, tm=256):
    t, k = a.shape
    d = w.shape[1]
    row = lambda i: (i, 0)
    const = lambda i: (0, 0)
    out_specs, out_shape = _ln_out_specs(t, d, tm)
    return pl.pallas_call(
        _proj_ln_kernel,
        grid=(t // tm,),
        in_specs=[pl.BlockSpec((tm, k), row), pl.BlockSpec((k, d), const),
                  pl.BlockSpec((tm, d), row), pl.BlockSpec((1, d), const),
                  pl.BlockSpec((1, d), const)],
        out_specs=out_specs,
        out_shape=out_shape,
        compiler_params=_params("arbitrary"),
        name="proj_ln",
    )(a, w, x, g, b)


def _pool_kernel(x_ref, win_ref, wg_ref, sc_ref, o_ref, carry_ref, *, tm, tiles_per_seq):
    i = pl.program_id(0)
    seq_tile = i % tiles_per_seq

    @pl.when(seq_tile == 0)
    def _():
        carry_ref[...] = jnp.zeros_like(carry_ref)

    xb = x_ref[...]
    pos = lax.broadcasted_iota(jnp.int32, (tm, 1), 0) + seq_tile * tm
    gd = POOL_GROUP_DIM
    for g, w in enumerate(POOL_WINDOWS):
        cols = slice(g * gd, (g + 1) * gd)
        u = _dot(xb, win_ref[:, cols])
        s = jnp.concatenate([carry_ref[g], u], axis=0)
        sh = 1
        while sh < w:
            s = s + pltpu.roll(s, sh, axis=0)
            sh *= 2
        cnt = jnp.minimum(pos + 1, w).astype(F32)
        pooled = s[POOL_HALO:, :] / cnt - u
        carry_ref[g] = u[tm - POOL_HALO:, :]
        mixed = _dot(pooled.astype(BF16), wg_ref[g]) * sc_ref[:, cols]
        o_ref[:, cols] = mixed.astype(BF16)


def _pool_mix(xb, w_in, w_group, scale, seq_len, tm=256):
    t, d = xb.shape
    gd = POOL_GROUP_DIM
    ng = len(POOL_WINDOWS)
    return pl.pallas_call(
        functools.partial(_pool_kernel, tm=tm, tiles_per_seq=seq_len // tm),
        grid=(t // tm,),
        in_specs=[pl.BlockSpec((tm, d), lambda i: (i, 0)),
                  pl.BlockSpec((d, d), lambda i: (0, 0)),
                  pl.BlockSpec((ng, gd, gd), lambda i: (0, 0, 0)),
                  pl.BlockSpec((1, d), lambda i: (0, 0))],
        out_specs=pl.BlockSpec((tm, d), lambda i: (i, 0)),
        out_shape=jax.ShapeDtypeStruct((t, d), BF16),
        scratch_shapes=[pltpu.VMEM((ng, POOL_HALO, gd), F32)],
        compiler_params=_params("arbitrary"),
        name="pool_mix",
    )(xb, w_in, w_group, scale)


def _conv_kernel(x_ref, wb_ref, wc_ref, wh_ref, cw_ref, o_ref, carry_ref, *, tm, tiles_per_seq):
    i = pl.program_id(1)

    @pl.when(i % tiles_per_seq == 0)
    def _():
        carry_ref[...] = jnp.zeros_like(carry_ref)

    xb = x_ref[...]
    b_gate = _dot(xb, wb_ref[...])
    u = _dot(xb, wc_ref[...]) * _dot(xb, wh_ref[...])
    ext = jnp.concatenate([carry_ref[...], u], axis=0)
    y = cw_ref[CONV_WIDTH - 1:CONV_WIDTH, :] * u
    for back in range(1, CONV_WIDTH):
        tap = CONV_WIDTH - 1 - back
        y = y + cw_ref[tap:tap + 1, :] * pltpu.roll(ext, back, axis=0)[CONV_HALO:, :]
    carry_ref[...] = u[tm - CONV_HALO:, :]
    o_ref[...] = (b_gate * y).astype(BF16)


def _conv_mix(xb, w_in, conv_w, seq_len, tm=512, tn=512):
    t, d = xb.shape
    nj = d // tn
    return pl.pallas_call(
        functools.partial(_conv_kernel, tm=tm, tiles_per_seq=seq_len // tm),
        grid=(nj, t // tm),
        in_specs=[pl.BlockSpec((tm, d), lambda j, i: (i, 0)),
                  pl.BlockSpec((d, tn), lambda j, i: (0, j)),
                  pl.BlockSpec((d, tn), lambda j, i: (0, nj + j)),
                  pl.BlockSpec((d, tn), lambda j, i: (0, 2 * nj + j)),
                  pl.BlockSpec((CONV_WIDTH, tn), lambda j, i: (0, j))],
        out_specs=pl.BlockSpec((tm, tn), lambda j, i: (i, j)),
        out_shape=jax.ShapeDtypeStruct((t, d), BF16),
        scratch_shapes=[pltpu.VMEM((CONV_HALO, tn), F32)],
        compiler_params=_params("parallel", "arbitrary"),
        name="conv_mix",
    )(xb, w_in, w_in, w_in, conv_w)


def _t5_bucket(dist):
    max_exact = N_BUCKETS // 2
    is_small = dist < max_exact
    distf = jnp.maximum(dist, 1).astype(F32)
    large = max_exact + (jnp.log(distf / max_exact) / math.log(MAX_DISTANCE / max_exact)
                         * (N_BUCKETS - max_exact)).astype(jnp.int32)
    large = jnp.minimum(large, N_BUCKETS - 1)
    return jnp.where(is_small, dist, large)


def _attn_bias(bias_tab, window, dil, has_prev):
    steps = window // dil
    q_loc = jnp.arange(DIL_BLOCK)[:, None]
    k_loc = jnp.arange(2 * DIL_BLOCK)[None, :]
    rel = q_loc + DIL_BLOCK - k_loc
    band = (rel >= 0) & (rel <= steps)
    bucket = _t5_bucket(jnp.maximum(rel, 0) * dil)
    tab = bias_tab.astype(F32).T[:, :, None, None]
    bias = sum(jnp.where(bucket == b, tab[:, b], 0.0) for b in range(N_BUCKETS))
    bias = jnp.where(band[None], bias, NEG)
    return bias if has_prev else bias[:, :, DIL_BLOCK:]


def _attn_kernel(*refs, has_prev):
    if has_prev:
        q_ref, kc_ref, kp_ref, vc_ref, vp_ref, bias_ref, o_ref, l_ref = refs
    else:
        q_ref, kc_ref, vc_ref, bias_ref, o_ref, l_ref = refs
    n = pl.program_id(1)
    q = q_ref[0]
    kc = kc_ref[0]
    vc = vc_ref[0]
    if has_prev:
        kp = kp_ref[0]
        vp = vp_ref[0]
        col = lax.broadcasted_iota(jnp.int32, (DIL_BLOCK, 2 * DIL_BLOCK), 1)
        keep = (col >= DIL_BLOCK) | (n > 0)
    outs, lses = [], []
    for h in range(DIL_HEADS):
        sl = slice(h * HEAD_DIM, (h + 1) * HEAD_DIM)
        if has_prev:
            kh = jnp.concatenate([kp[:, sl], kc[:, sl]], axis=0)
            vh = jnp.concatenate([vp[:, sl], vc[:, sl]], axis=0)
        else:
            kh, vh = kc[:, sl], vc[:, sl]
        s = lax.dot_general(q[:, sl], kh, _NT, preferred_element_type=F32) * (HEAD_DIM ** -0.5) + bias_ref[h]
        if has_prev:
            s = jnp.where(keep, s, NEG)
        m = jnp.max(s, axis=-1, keepdims=True)
        p = jnp.exp(s - m)
        l = jnp.sum(p, axis=-1, keepdims=True)
        outs.append(_dot(p.astype(BF16), vh) / l)
        lses.append(jnp.broadcast_to(m + jnp.log(l), (DIL_BLOCK, HEAD_DIM)))
    o_ref[0] = jnp.concatenate(outs, axis=1)
    l_ref[0] = jnp.concatenate(lses, axis=1)


def _dilated_group(qkv, bias_tab, g, batch, seq_len):
    window, dil = DIL_CONFIGS[g]
    span = DIL_BLOCK * dil
    nb = seq_len // span
    has_prev = nb > 1
    ncol = qkv.shape[1] // DIL_WIDTH
    view = qkv.reshape(batch, seq_len // dil, dil * qkv.shape[1])
    blk = (1, DIL_BLOCK, DIL_WIDTH)

    def cur(kind):
        return pl.BlockSpec(blk, lambda b, n, r: (b, n, r * ncol + 3 * g + kind))

    def prev(kind):
        return pl.BlockSpec(blk, lambda b, n, r: (b, jnp.maximum(n - 1, 0), r * ncol + 3 * g + kind))

    bias = _attn_bias(bias_tab, window, dil, has_prev)
    bias_spec = pl.BlockSpec(bias.shape, lambda b, n, r: (0, 0, 0))
    if has_prev:
        in_specs = [cur(0), cur(1), prev(1), cur(2), prev(2), bias_spec]
        args = (view,) * 5 + (bias,)
    else:
        in_specs = [cur(0), cur(1), cur(2), bias_spec]
        args = (view,) * 3 + (bias,)
    out_spec = pl.BlockSpec(blk, lambda b, n, r: (b, n, r))
    out_sds = jax.ShapeDtypeStruct((batch, seq_len // dil, dil * DIL_WIDTH), F32)
    o, l = pl.pallas_call(
        functools.partial(_attn_kernel, has_prev=has_prev),
        grid=(batch, nb, dil),
        in_specs=in_specs,
        out_specs=[out_spec, out_spec],
        out_shape=[out_sds, out_sds],
        compiler_params=_params("parallel", "arbitrary", "arbitrary"),
        name=f"dilated_attn_g{g}",
    )(*args)
    t = batch * seq_len
    return o.reshape(t, DIL_WIDTH), l.reshape(t, DIL_WIDTH)


def _attn_out_kernel(o0, o1, o2, l0, l1, l2, w_ref, x_ref, g_ref, b_ref, of_ref, ob_ref, op_ref):
    a0, a1, a2 = l0[...], l1[...], l2[...]
    m = jnp.maximum(jnp.maximum(a0, a1), a2)
    e0, e1, e2 = jnp.exp(a0 - m), jnp.exp(a1 - m), jnp.exp(a2 - m)
    o = (e0 * o0[...] + e1 * o1[...] + e2 * o2[...]) / (e0 + e1 + e2)
    z = ALPHA * x_ref[...] + _dot(o.astype(BF16), w_ref[...])
    _store_ln_outputs(_layer_norm(z, g_ref[...], b_ref[...]), of_ref, ob_ref, op_ref)


def _attn_out(os_, ls_, w, x, g, b, tm=256):
    t, d = x.shape
    k = w.shape[0]
    row = lambda i: (i, 0)
    const = lambda i: (0, 0)
    part = pl.BlockSpec((tm, k), row)
    out_specs, out_shape = _ln_out_specs(t, d, tm)
    return pl.pallas_call(
        _attn_out_kernel,
        grid=(t // tm,),
        in_specs=[part] * 6 + [pl.BlockSpec((k, d), const), pl.BlockSpec((tm, d), row),
                               pl.BlockSpec((1, d), const), pl.BlockSpec((1, d), const)],
        out_specs=out_specs,
        out_shape=out_shape,
        compiler_params=_params("arbitrary"),
        name="attn_out_ln",
    )(*os_, *ls_, w, x, g, b)


def _router_kernel(x_ref, wt_ref, b_ref, eidx_ref, gate_ref, pos_ref, cnt_ref, base_ref, *, tm):
    @pl.when(pl.program_id(0) == 0)
    def _():
        base_ref[...] = jnp.zeros_like(base_ref)

    x = x_ref[...]
    w = wt_ref[...]
    xh = x.astype(BF16)
    xl = (x - xh.astype(F32)).astype(BF16)
    wh = w.astype(BF16)
    wl = (w - wh.astype(F32)).astype(BF16)
    dg = functools.partial(lax.dot_general, dimension_numbers=_NT, preferred_element_type=F32)
    logits = dg(wh, xh) + (dg(wh, xl) + dg(wl, xh))
    scores = 1.0 / (1.0 + jnp.exp(-logits))
    choice = scores + b_ref[...]

    iota_g = lax.broadcasted_iota(jnp.int32, (GROUP_SIZE, tm), 0)
    gs = []
    for gi in range(N_EXPERT_GROUPS):
        cg = choice[gi * GROUP_SIZE:(gi + 1) * GROUP_SIZE, :]
        t1 = jnp.max(cg, axis=0, keepdims=True)
        i1 = jnp.min(jnp.where(cg == t1, iota_g, GROUP_SIZE), axis=0, keepdims=True)
        t2 = jnp.max(jnp.where(iota_g == i1, -jnp.inf, cg), axis=0, keepdims=True)
        gs.append(t1 + t2)
    masked = []
    for gi in range(N_EXPERT_GROUPS):
        rank = jnp.zeros((1, tm), jnp.int32)
        for gj in range(N_EXPERT_GROUPS):
            if gj != gi:
                beats = (gs[gj] >= gs[gi]) if gj < gi else (gs[gj] > gs[gi])
                rank = rank + beats.astype(jnp.int32)
        masked.append(jnp.where(rank < TOPK_GROUPS, choice[gi * GROUP_SIZE:(gi + 1) * GROUP_SIZE, :], -jnp.inf))
    c = jnp.concatenate(masked, axis=0)

    iota_e = lax.broadcasted_iota(jnp.int32, (N_EXPERTS, tm), 0)
    picks, gates = [], []
    sel = jnp.zeros((N_EXPERTS, tm), F32)
    for _ in range(TOP_K):
        m = jnp.max(c, axis=0, keepdims=True)
        idx = jnp.min(jnp.where(c == m, iota_e, N_EXPERTS), axis=0, keepdims=True)
        hit = iota_e == idx
        picks.append(hit)
        gates.append(jnp.sum(jnp.where(hit, scores, 0.0), axis=0, keepdims=True))
        sel = jnp.where(hit, 1.0, sel)
        c = jnp.where(hit, -jnp.inf, c)
    gsum = gates[0]
    for gk in gates[1:]:
        gsum = gsum + gk

    before = (lax.broadcasted_iota(jnp.int32, (tm, tm), 0) < lax.broadcasted_iota(jnp.int32, (tm, tm), 1))
    rank_in_tile = _dot(sel.astype(BF16), before.astype(BF16))
    base = base_ref[...]
    posfull = base[:, 0:1] + rank_in_tile
    efull = iota_e.astype(F32)
    for k in range(TOP_K):
        hit = picks[k]
        eidx_ref[k:k + 1, :] = jnp.sum(jnp.where(hit, efull, 0.0), axis=0, keepdims=True).astype(jnp.int32)
        pos_ref[k:k + 1, :] = jnp.sum(jnp.where(hit, posfull, 0.0), axis=0, keepdims=True).astype(jnp.int32)
        gate_ref[k:k + 1, :] = gates[k] / gsum * ROUTED_SCALE
    new_base = base + jnp.sum(sel, axis=1, keepdims=True)
    base_ref[...] = new_base
    cnt_ref[...] = new_base


def _route(x, router_wt, router_b, tm=512):
    t, d = x.shape
    tok = pl.BlockSpec((TOP_K, tm), lambda i: (0, i))
    cnt = pl.BlockSpec((N_EXPERTS, LANES), lambda i: (0, 0))
    return pl.pallas_call(
        functools.partial(_router_kernel, tm=tm),
        grid=(t // tm,),
        in_specs=[pl.BlockSpec((tm, d), lambda i: (i, 0)),
                  pl.BlockSpec((N_EXPERTS, d), lambda i: (0, 0)),
                  pl.BlockSpec((N_EXPERTS, 1), lambda i: (0, 0))],
        out_specs=[tok, tok, tok, cnt],
        out_shape=[jax.ShapeDtypeStruct((TOP_K, t), jnp.int32), jax.ShapeDtypeStruct((TOP_K, t), F32),
                   jax.ShapeDtypeStruct((TOP_K, t), jnp.int32), jax.ShapeDtypeStruct((N_EXPERTS, LANES), F32)],
        scratch_shapes=[pltpu.VMEM((N_EXPERTS, LANES), F32)],
        compiler_params=_params("arbitrary"),
        name="router",
    )(x, router_wt, router_b)


def _row(ref, r):
    return ref.at[pl.ds(r, 1)]


def _dest_kernel(pstart_ref, eidx_ref, pos_ref, dest_ref):
    e = eidx_ref[...]
    acc = pos_ref[...]
    for k in range(N_EXPERTS):
        acc = acc + jnp.where(e == k, pstart_ref[k], 0)
    dest_ref[...] = acc


def _dest_rows(pstart, eidx, pos):
    full = pl.BlockSpec(eidx.shape, lambda i, ps: (0, 0))
    return pl.pallas_call(
        _dest_kernel,
        grid_spec=pltpu.PrefetchScalarGridSpec(num_scalar_prefetch=1, grid=(1,), in_specs=[full, full],
                                               out_specs=full),
        out_shape=jax.ShapeDtypeStruct(eidx.shape, jnp.int32),
        compiler_params=_params("arbitrary"),
        name="dest_rows",
    )(pstart, eidx, pos)


def _dispatch_kernel(pad_lo_ref, pad_hi_ref, dest_ref, x_ref, xs_hbm, zero_ref, sem, pad_sem, *, tm, n_blk):
    i = pl.program_id(0)

    @pl.when(i == 0)
    def _():
        zero_ref[...] = jnp.zeros_like(zero_ref)

        def fill(lo, hi, copy):
            def start(r, c):
                copy(r).start()
                return c

            def wait(r, c):
                copy(r).wait()
                return c

            lax.fori_loop(lo, hi, start, 0)
            lax.fori_loop(lo, hi, wait, 0)

        def per_expert(e, carry):
            fill(pad_lo_ref[e], pad_hi_ref[e],
                 lambda r: pltpu.make_async_copy(_row(zero_ref, 0), _row(xs_hbm, r), pad_sem))
            return carry

        lax.fori_loop(0, N_EXPERTS, per_expert, 0)
        fill(pad_hi_ref[N_EXPERTS - 1] // ROW_BLOCK, n_blk,
             lambda blk: pltpu.make_async_copy(
                 zero_ref, xs_hbm.at[pl.ds(pl.multiple_of(blk * ROW_BLOCK, ROW_BLOCK), ROW_BLOCK)], pad_sem))

    def per_token(j, carry):
        src = _row(x_ref, j)
        for k in range(TOP_K):
            pltpu.make_async_copy(src, _row(xs_hbm, dest_ref[0, 0, j * TOP_K + k]), sem).start()
        return carry

    lax.fori_loop(0, tm, per_token, 0)
    for k in range(TOP_K):
        pltpu.make_async_copy(x_ref, xs_hbm.at[pl.ds(0, tm)], sem).wait()


def _dispatch(xp, dest_tok, pad_lo, pad_hi, n_rows, tm=512):
    t, half = xp.shape
    n_tiles = t // tm
    dest3 = dest_tok.reshape(n_tiles, 1, tm * TOP_K)
    grid_spec = pltpu.PrefetchScalarGridSpec(
        num_scalar_prefetch=2,
        grid=(n_tiles,),
        in_specs=[pl.BlockSpec((1, 1, tm * TOP_K), lambda i, lo, hi: (i, 0, 0), memory_space=pltpu.SMEM),
                  pl.BlockSpec((tm, half), lambda i, lo, hi: (i, 0))],
        out_specs=pl.BlockSpec(memory_space=pl.ANY),
        scratch_shapes=[pltpu.VMEM((ROW_BLOCK, half), jnp.int32),
                        pltpu.SemaphoreType.DMA(()), pltpu.SemaphoreType.DMA(())],
    )
    return pl.pallas_call(
        functools.partial(_dispatch_kernel, tm=tm, n_blk=n_rows // ROW_BLOCK),
        grid_spec=grid_spec,
        out_shape=jax.ShapeDtypeStruct((n_rows, half), jnp.int32),
        compiler_params=_params("arbitrary"),
        name="dispatch",
    )(pad_lo, pad_hi, dest3, xp)


def _expert_kernel(blk_e_ref, nused_ref, xs_ref, wgu_ref, wd_ref, y_ref, wgu_bf, wd_bf):
    i = pl.program_id(0)
    used = i < nused_ref[0]

    @pl.when(used)
    def _():
        @pl.when((i == 0) | (blk_e_ref[i] != blk_e_ref[jnp.maximum(i - 1, 0)]))
        def _():
            wgu_bf[...] = wgu_ref[0, 0].astype(BF16)
            wd_bf[...] = wd_ref[0, 0].astype(BF16)

        hgu = _dot(_unpack_rows(xs_ref[...]), wgu_bf[...])
        h = _silu(hgu[:, :EXPERT_DIM]) * hgu[:, EXPERT_DIM:]
        y_ref[...] = _pack_rows(_dot(h.astype(BF16), wd_bf[...]))

    @pl.when(jnp.logical_not(used))
    def _():
        y_ref[...] = jnp.zeros_like(y_ref)


def _experts(xs, blk_e, nused, w_gu, w_down, layer):
    n_rows, half = xs.shape
    d, f2 = w_gu.shape[2], w_gu.shape[3]
    blk = lambda i, be, nu: (jnp.minimum(i, nu[0] - 1), 0)
    wblk = lambda i, be, nu: (layer, be[jnp.minimum(i, nu[0] - 1)], 0, 0)
    grid_spec = pltpu.PrefetchScalarGridSpec(
        num_scalar_prefetch=2,
        grid=(n_rows // ROW_BLOCK,),
        in_specs=[pl.BlockSpec((ROW_BLOCK, half), blk),
                  pl.BlockSpec((1, 1, d, f2), wblk),
                  pl.BlockSpec((1, 1, f2 // 2, d), wblk)],
        out_specs=pl.BlockSpec((ROW_BLOCK, half), lambda i, be, nu: (i, 0)),
        scratch_shapes=[pltpu.VMEM((d, f2), BF16), pltpu.VMEM((f2 // 2, d), BF16)],
    )
    return pl.pallas_call(
        _expert_kernel,
        grid_spec=grid_spec,
        out_shape=jax.ShapeDtypeStruct((n_rows, half), jnp.int32),
        compiler_params=_params("arbitrary"),
        name="experts",
    )(blk_e, nused, xs, w_gu, w_down)


def _combine_kernel(dest_ref, dest_next_ref, y_hbm, gate_ref, xf_ref, xb_ref, sgu_ref, sd_ref, g_ref, b_ref,
                    of_ref, ob_ref, ybuf, sem, *, tm, n_tiles):
    i = pl.program_id(0)
    slot = i % 2

    def gather(idx_ref, s):
        def per_token(j, carry):
            for k in range(TOP_K):
                pltpu.make_async_copy(_row(y_hbm, idx_ref[0, 0, j * TOP_K + k]),
                                      ybuf.at[s, k, pl.ds(j, 1)], sem.at[s]).start()
            return carry

        lax.fori_loop(0, tm, per_token, 0)

    @pl.when(i == 0)
    def _():
        gather(dest_ref, 0)

    @pl.when(i + 1 < n_tiles)
    def _():
        gather(dest_next_ref, 1 - slot)

    for k in range(TOP_K):
        pltpu.make_async_copy(y_hbm.at[pl.ds(0, tm)], ybuf.at[slot, k], sem.at[slot]).wait()
    gates = gate_ref[...]
    lo = hi = None
    for k in range(TOP_K):
        p = ybuf[slot, k]
        gk = gates[:, k:k + 1]
        lo_k = gk * lax.bitcast_convert_type(lax.shift_left(p, 16), F32)
        hi_k = gk * lax.bitcast_convert_type(p & HIGH_HALF, F32)
        lo = lo_k if lo is None else lo + lo_k
        hi = hi_k if hi is None else hi + hi_k
    routed = jnp.concatenate([lo, hi], axis=1)

    sgu = _dot(xb_ref[...], sgu_ref[...])
    sh = _silu(sgu[:, :SHARED_DIM]) * sgu[:, SHARED_DIM:]
    shared = _dot(sh.astype(BF16), sd_ref[...])
    z = ALPHA * xf_ref[...] + (routed + shared)
    y = _layer_norm(z, g_ref[...], b_ref[...])
    of_ref[...] = y
    ob_ref[...] = y.astype(BF16)


def _combine(y, dest_tok, gate_tok, xf, xb, sw_gu, sw_down, g, b, tm=256):
    t, d = xf.shape
    n_tiles = t // tm
    n_gather = TOP_K * tm
    dest3 = dest_tok.reshape(n_tiles, 1, n_gather)
    row = lambda i: (i, 0)
    const = lambda i: (0, 0)
    smem_blk = lambda imap: pl.BlockSpec((1, 1, n_gather), imap, memory_space=pltpu.SMEM)
    return pl.pallas_call(
        functools.partial(_combine_kernel, tm=tm, n_tiles=n_tiles),
        grid=(n_tiles,),
        in_specs=[smem_blk(lambda i: (i, 0, 0)),
                  smem_blk(lambda i: (jnp.minimum(i + 1, n_tiles - 1), 0, 0)),
                  pl.BlockSpec(memory_space=pl.ANY),
                  pl.BlockSpec((tm, TOP_K), row),
                  pl.BlockSpec((tm, d), row), pl.BlockSpec((tm, d), row),
                  pl.BlockSpec(sw_gu.shape, const), pl.BlockSpec(sw_down.shape, const),
                  pl.BlockSpec((1, d), const), pl.BlockSpec((1, d), const)],
        out_specs=[pl.BlockSpec((tm, d), row), pl.BlockSpec((tm, d), row)],
        out_shape=[jax.ShapeDtypeStruct((t, d), F32), jax.ShapeDtypeStruct((t, d), BF16)],
        scratch_shapes=[pltpu.VMEM((2, TOP_K, tm, d // 2), jnp.int32),
                        pltpu.SemaphoreType.DMA((2,))],
        compiler_params=_params("arbitrary"),
        name="combine_ln",
    )(dest3, dest3, y, gate_tok, xf, xb, sw_gu, sw_down, g, b)


def _moe_layer(xf, xb, xp, router_w, router_b, w_gu, w_down, layer, sw_gu, sw_down, g, b):
    t, d = xf.shape
    eidx, gate, pos, cnt = _route(xf, router_w.T, router_b.reshape(N_EXPERTS, 1))

    sizes = cnt[:, 0].astype(jnp.int32)
    padded = (sizes + ROW_BLOCK - 1) // ROW_BLOCK * ROW_BLOCK
    pend = jnp.cumsum(padded)
    pstart = pend - padded
    tk = t * TOP_K
    n_rows = -(-tk // ROW_BLOCK) * ROW_BLOCK + N_EXPERTS * ROW_BLOCK
    blk_start = jnp.arange(n_rows // ROW_BLOCK, dtype=jnp.int32) * ROW_BLOCK
    blk_e = jnp.minimum(jnp.sum(pend[None, :] <= blk_start[:, None], axis=1), N_EXPERTS - 1).astype(jnp.int32)
    nused = (pend[-1:] // ROW_BLOCK).astype(jnp.int32)

    dest_tok = _dest_rows(pstart, eidx, pos).T
    xs = _dispatch(xp, dest_tok, pstart + sizes, pend, n_rows)
    y = _experts(xs, blk_e, nused, w_gu, w_down, layer)
    return _combine(y, dest_tok, gate.T, xf, xb, sw_gu, sw_down, g, b)


def kernel(x, rel_bias, pool_w_in, pool_w_group, pool_scale, pool_w_out, dil_w_qkv, dil_w_out,
           conv_w_in, conv_w, conv_w_out, ln_gain, ln_bias, router_w, router_bias,
           expert_w_gu, expert_w_down, shared_w_gu, shared_w_down):
    bn, s, d = x.shape
    t = bn * s
    xf = x.reshape(t, d)
    xb = xf.astype(BF16)
    bf = lambda a: a.astype(BF16)
    ia = ib = ic = 0
    for i in range(DEPTH):
        kind = i % 3
        g1, b1 = ln_gain[i, 0].reshape(1, d), ln_bias[i, 0].reshape(1, d)
        g2, b2 = ln_gain[i, 1].reshape(1, d), ln_bias[i, 1].reshape(1, d)
        if kind == 0:
            mixed = _pool_mix(xb, bf(pool_w_in[ia]), bf(pool_w_group[ia]), pool_scale[ia].reshape(1, d), s)
            xf, xb, xp = _proj_ln(mixed, bf(pool_w_out[ia]), xf, g1, b1)
            ia += 1
        elif kind == 1:
            qkv = _matmul(xb, bf(dil_w_qkv[ib]), tm=512, tn=3 * DIL_WIDTH)
            os_, ls_ = [], []
            for g in range(len(DIL_CONFIGS)):
                o, l = _dilated_group(qkv, rel_bias[:, g * DIL_HEADS:(g + 1) * DIL_HEADS], g, bn, s)
                os_.append(o)
                ls_.append(l)
            xf, xb, xp = _attn_out(os_, ls_, bf(dil_w_out[ib]), xf, g1, b1)
            ib += 1
        else:
            v = _conv_mix(xb, bf(conv_w_in[ic]), conv_w[ic], s)
            xf, xb, xp = _proj_ln(v, bf(conv_w_out[ic]), xf, g1, b1)
            ic += 1
        xf, xb = _moe_layer(xf, xb, xp, router_w[i], router_bias[i], expert_w_gu, expert_w_down, i,
                            bf(shared_w_gu[i]), bf(shared_w_down[i]), g2, b2)
    return xf.reshape(bn, s, d)
```

```python
import functools
import math

import jax
import jax.numpy as jnp
from jax import lax
from jax.experimental import pallas as pl
from jax.experimental.pallas import tpu as pltpu

F32 = jnp.float32
BF16 = jnp.bfloat16

DEPTH = 4
D_MODEL = 2048
POOL_WINDOWS = (2, 4, 8, 16)
POOL_GROUP_DIM = D_MODEL // len(POOL_WINDOWS)
POOL_HALO = 16
DIL_CONFIGS = ((128, 1), (512, 4), (2048, 16))
HEAD_DIM = 64
DIL_HEADS = 8
DIL_WIDTH = DIL_HEADS * HEAD_DIM
DIL_BLOCK = 128
N_BUCKETS = 32
MAX_DISTANCE = 2048
CONV_WIDTH = 3
CONV_HALO = 8
N_EXPERTS = 64
TOP_K = 8
N_EXPERT_GROUPS = 8
GROUP_SIZE = N_EXPERTS // N_EXPERT_GROUPS
TOPK_GROUPS = 4
EXPERT_DIM = 128
SHARED_DIM = 256
ROUTED_SCALE = 2.5
ROW_BLOCK = 256
ALPHA = (2.0 * DEPTH) ** 0.25
LN_EPS = 1e-5
NEG = -1e30

LANES = 128
HIGH_HALF = -65536
VMEM_LIMIT = 52 * 1024 * 1024

_NT = (((1,), (1,)), ((), ()))


def _dot(a, b):
    return jnp.dot(a, b, preferred_element_type=F32)


def _params(*sem):
    return pltpu.CompilerParams(dimension_semantics=sem, vmem_limit_bytes=VMEM_LIMIT)


def _layer_norm(z, g, b):
    mu = jnp.mean(z, axis=-1, keepdims=True)
    zc = z - mu
    var = jnp.mean(zc * zc, axis=-1, keepdims=True)
    return zc * lax.rsqrt(var + LN_EPS) * g + b


def _silu(v):
    return v / (1.0 + jnp.exp(-v))


def _pack_rows(y):
    half = D_MODEL // 2
    lo = lax.bitcast_convert_type(y[:, :half].astype(BF16).astype(F32), jnp.int32)
    hi = lax.bitcast_convert_type(y[:, half:].astype(BF16).astype(F32), jnp.int32)
    return (hi & HIGH_HALF) | lax.shift_right_logical(lo, 16)


def _unpack_rows(p):
    lo = lax.bitcast_convert_type(lax.shift_left(p, 16), F32)
    hi = lax.bitcast_convert_type(p & HIGH_HALF, F32)
    return jnp.concatenate([lo, hi], axis=1).astype(BF16)


def _store_ln_outputs(y, of_ref, ob_ref, op_ref):
    of_ref[...] = y
    ob_ref[...] = y.astype(BF16)
    op_ref[...] = _pack_rows(y)


def _ln_out_specs(t, d, tm):
    row = lambda i: (i, 0)
    specs = [pl.BlockSpec((tm, d), row), pl.BlockSpec((tm, d), row), pl.BlockSpec((tm, d // 2), row)]
    shapes = [jax.ShapeDtypeStruct((t, d), F32), jax.ShapeDtypeStruct((t, d), BF16),
              jax.ShapeDtypeStruct((t, d // 2), jnp.int32)]
    return specs, shapes


def _mm_kernel(x_ref, w_ref, o_ref):
    o_ref[...] = _dot(x_ref[...], w_ref[...]).astype(o_ref.dtype)


def _matmul(x, w, tm, tn):
    t, k = x.shape
    n = w.shape[1]
    return pl.pallas_call(
        _mm_kernel,
        grid=(n // tn, t // tm),
        in_specs=[pl.BlockSpec((tm, k), lambda j, i: (i, 0)),
                  pl.BlockSpec((k, tn), lambda j, i: (0, j))],
        out_specs=pl.BlockSpec((tm, tn), lambda j, i: (i, j)),
        out_shape=jax.ShapeDtypeStruct((t, n), BF16),
        compiler_params=_params("parallel", "arbitrary"),
        name="matmul",
    )(x, w)


def _proj_ln_kernel(a_ref, w_ref, x_ref, g_ref, b_ref, of_ref, ob_ref, op_ref):
    z = ALPHA * x_ref[...] + _dot(a_ref[...], w_ref[...])
    _store_ln_outputs(_layer_norm(z, g_ref[...], b_ref[...]), of_ref, ob_ref, op_ref)


def _proj_ln(a, w, x, g, b, tm=256):
    t, k = a.shape
    d = w.shape[1]
    row = lambda i: (i, 0)
    const = lambda i: (0, 0)
    out_specs, out_shape = _ln_out_specs(t, d, tm)
    return pl.pallas_call(
        _proj_ln_kernel,
        grid=(t // tm,),
        in_specs=[pl.BlockSpec((tm, k), row), pl.BlockSpec((k, d), const),
                  pl.BlockSpec((tm, d), row), pl.BlockSpec((1, d), const),
                  pl.BlockSpec((1, d), const)],
        out_specs=out_specs,
        out_shape=out_shape,
        compiler_params=_params("arbitrary"),
        name="proj_ln",
    )(a, w, x, g, b)


def _pool_kernel(x_ref, win_ref, wg_ref, sc_ref, o_ref, carry_ref, *, tm, tiles_per_seq):
    i = pl.program_id(0)
    seq_tile = i % tiles_per_seq

    @pl.when(seq_tile == 0)
    def _():
        carry_ref[...] = jnp.zeros_like(carry_ref)

    xb = x_ref[...]
    pos = lax.broadcasted_iota(jnp.int32, (tm, 1), 0) + seq_tile * tm
    gd = POOL_GROUP_DIM
    for g, w in enumerate(POOL_WINDOWS):
        cols = slice(g * gd, (g + 1) * gd)
        u = _dot(xb, win_ref[:, cols])
        s = jnp.concatenate([carry_ref[g], u], axis=0)
        sh = 1
        while sh < w:
            s = s + pltpu.roll(s, sh, axis=0)
            sh *= 2
        cnt = jnp.minimum(pos + 1, w).astype(F32)
        pooled = s[POOL_HALO:, :] / cnt - u
        carry_ref[g] = u[tm - POOL_HALO:, :]
        mixed = _dot(pooled.astype(BF16), wg_ref[g]) * sc_ref[:, cols]
        o_ref[:, cols] = mixed.astype(BF16)


def _pool_mix(xb, w_in, w_group, scale, seq_len, tm=256):
    t, d = xb.shape
    gd = POOL_GROUP_DIM
    ng = len(POOL_WINDOWS)
    return pl.pallas_call(
        functools.partial(_pool_kernel, tm=tm, tiles_per_seq=seq_len // tm),
        grid=(t // tm,),
        in_specs=[pl.BlockSpec((tm, d), lambda i: (i, 0)),
                  pl.BlockSpec((d, d), lambda i: (0, 0)),
                  pl.BlockSpec((ng, gd, gd), lambda i: (0, 0, 0)),
                  pl.BlockSpec((1, d), lambda i: (0, 0))],
        out_specs=pl.BlockSpec((tm, d), lambda i: (i, 0)),
        out_shape=jax.ShapeDtypeStruct((t, d), BF16),
        scratch_shapes=[pltpu.VMEM((ng, POOL_HALO, gd), F32)],
        compiler_params=_params("arbitrary"),
        name="pool_mix",
    )(xb, w_in, w_group, scale)


def _conv_kernel(x_ref, wb_ref, wc_ref, wh_ref, cw_ref, o_ref, carry_ref, *, tm, tiles_per_seq):
    i = pl.program_id(1)

    @pl.when(i % tiles_per_seq == 0)
    def _():
        carry_ref[...] = jnp.zeros_like(carry_ref)

    xb = x_ref[...]
    b_gate = _dot(xb, wb_ref[...])
    u = _dot(xb, wc_ref[...]) * _dot(xb, wh_ref[...])
    ext = jnp.concatenate([carry_ref[...], u], axis=0)
    y = cw_ref[CONV_WIDTH - 1:CONV_WIDTH, :] * u
    for back in range(1, CONV_WIDTH):
        tap = CONV_WIDTH - 1 - back
        y = y + cw_ref[tap:tap + 1, :] * pltpu.roll(ext, back, axis=0)[CONV_HALO:, :]
    carry_ref[...] = u[tm - CONV_HALO:, :]
    o_ref[...] = (b_gate * y).astype(BF16)


def _conv_mix(xb, w_in, conv_w, seq_len, tm=512, tn=512):
    t, d = xb.shape
    nj = d // tn
    return pl.pallas_call(
        functools.partial(_conv_kernel, tm=tm, tiles_per_seq=seq_len // tm),
        grid=(nj, t // tm),
        in_specs=[pl.BlockSpec((tm, d), lambda j, i: (i, 0)),
                  pl.BlockSpec((d, tn), lambda j, i: (0, j)),
                  pl.BlockSpec((d, tn), lambda j, i: (0, nj + j)),
                  pl.BlockSpec((d, tn), lambda j, i: (0, 2 * nj + j)),
                  pl.BlockSpec((CONV_WIDTH, tn), lambda j, i: (0, j))],
        out_specs=pl.BlockSpec((tm, tn), lambda j, i: (i, j)),
        out_shape=jax.ShapeDtypeStruct((t, d), BF16),
        scratch_shapes=[pltpu.VMEM((CONV_HALO, tn), F32)],
        compiler_params=_params("parallel", "arbitrary"),
        name="conv_mix",
    )(xb, w_in, w_in, w_in, conv_w)


def _t5_bucket(dist):
    max_exact = N_BUCKETS // 2
    is_small = dist < max_exact
    distf = jnp.maximum(dist, 1).astype(F32)
    large = max_exact + (jnp.log(distf / max_exact) / math.log(MAX_DISTANCE / max_exact)
                         * (N_BUCKETS - max_exact)).astype(jnp.int32)
    large = jnp.minimum(large, N_BUCKETS - 1)
    return jnp.where(is_small, dist, large)


def _attn_bias(bias_tab, window, dil, has_prev):
    steps = window // dil
    q_loc = jnp.arange(DIL_BLOCK)[:, None]
    k_loc = jnp.arange(2 * DIL_BLOCK)[None, :]
    rel = q_loc + DIL_BLOCK - k_loc
    band = (rel >= 0) & (rel <= steps)
    bucket = _t5_bucket(jnp.maximum(rel, 0) * dil)
    tab = bias_tab.astype(F32).T[:, :, None, None]
    bias = sum(jnp.where(bucket == b, tab[:, b], 0.0) for b in range(N_BUCKETS))
    bias = jnp.where(band[None], bias, NEG)
    return bias if has_prev else bias[:, :, DIL_BLOCK:]


def _attn_kernel(*refs, has_prev):
    if has_prev:
        q_ref, kc_ref, kp_ref, vc_ref, vp_ref, bias_ref, o_ref, l_ref = refs
    else:
        q_ref, kc_ref, vc_ref, bias_ref, o_ref, l_ref = refs
    n = pl.program_id(1)
    q = q_ref[0]
    kc = kc_ref[0]
    vc = vc_ref[0]
    if has_prev:
        kp = kp_ref[0]
        vp = vp_ref[0]
        col = lax.broadcasted_iota(jnp.int32, (DIL_BLOCK, 2 * DIL_BLOCK), 1)
        keep = (col >= DIL_BLOCK) | (n > 0)
    outs, lses = [], []
    for h in range(DIL_HEADS):
        sl = slice(h * HEAD_DIM, (h + 1) * HEAD_DIM)
        if has_prev:
            kh = jnp.concatenate([kp[:, sl], kc[:, sl]], axis=0)
            vh = jnp.concatenate([vp[:, sl], vc[:, sl]], axis=0)
        else:
            kh, vh = kc[:, sl], vc[:, sl]
        s = lax.dot_general(q[:, sl], kh, _NT, preferred_element_type=F32) * (HEAD_DIM ** -0.5) + bias_ref[h]
        if has_prev:
            s = jnp.where(keep, s, NEG)
        m = jnp.max(s, axis=-1, keepdims=True)
        p = jnp.exp(s - m)
        l = jnp.sum(p, axis=-1, keepdims=True)
        outs.append(_dot(p.astype(BF16), vh) / l)
        lses.append(jnp.broadcast_to(m + jnp.log(l), (DIL_BLOCK, HEAD_DIM)))
    o_ref[0] = jnp.concatenate(outs, axis=1)
    l_ref[0] = jnp.concatenate(lses, axis=1)


def _dilated_group(qkv, bias_tab, g, batch, seq_len):
    window, dil = DIL_CONFIGS[g]
    span = DIL_BLOCK * dil
    nb = seq_len // span
    has_prev = nb > 1
    ncol = qkv.shape[1] // DIL_WIDTH
    view = qkv.reshape(batch, seq_len // dil, dil * qkv.shape[1])
    blk = (1, DIL_BLOCK, DIL_WIDTH)

    def cur(kind):
        return pl.BlockSpec(blk, lambda b, n, r: (b, n, r * ncol + 3 * g + kind))

    def prev(kind):
        return pl.BlockSpec(blk, lambda b, n, r: (b, jnp.maximum(n - 1, 0), r * ncol + 3 * g + kind))

    bias = _attn_bias(bias_tab, window, dil, has_prev)
    bias_spec = pl.BlockSpec(bias.shape, lambda b, n, r: (0, 0, 0))
    if has_prev:
        in_specs = [cur(0), cur(1), prev(1), cur(2), prev(2), bias_spec]
        args = (view,) * 5 + (bias,)
    else:
        in_specs = [cur(0), cur(1), cur(2), bias_spec]
        args = (view,) * 3 + (bias,)
    out_spec = pl.BlockSpec(blk, lambda b, n, r: (b, n, r))
    out_sds = jax.ShapeDtypeStruct((batch, seq_len // dil, dil * DIL_WIDTH), F32)
    o, l = pl.pallas_call(
        functools.partial(_attn_kernel, has_prev=has_prev),
        grid=(batch, nb, dil),
        in_specs=in_specs,
        out_specs=[out_spec, out_spec],
        out_shape=[out_sds, out_sds],
        compiler_params=_params("parallel", "arbitrary", "arbitrary"),
        name=f"dilated_attn_g{g}",
    )(*args)
    t = batch * seq_len
    return o.reshape(t, DIL_WIDTH), l.reshape(t, DIL_WIDTH)


def _attn_out_kernel(o0, o1, o2, l0, l1, l2, w_ref, x_ref, g_ref, b_ref, of_ref, ob_ref, op_ref):
    a0, a1, a2 = l0[...], l1[...], l2[...]
    m = jnp.maximum(jnp.maximum(a0, a1), a2)
    e0, e1, e2 = jnp.exp(a0 - m), jnp.exp(a1 - m), jnp.exp(a2 - m)
    o = (e0 * o0[...] + e1 * o1[...] + e2 * o2[...]) / (e0 + e1 + e2)
    z = ALPHA * x_ref[...] + _dot(o.astype(BF16), w_ref[...])
    _store_ln_outputs(_layer_norm(z, g_ref[...], b_ref[...]), of_ref, ob_ref, op_ref)


def _attn_out(os_, ls_, w, x, g, b, tm=256):
    t, d = x.shape
    k = w.shape[0]
    row = lambda i: (i, 0)
    const = lambda i: (0, 0)
    part = pl.BlockSpec((tm, k), row)
    out_specs, out_shape = _ln_out_specs(t, d, tm)
    return pl.pallas_call(
        _attn_out_kernel,
        grid=(t // tm,),
        in_specs=[part] * 6 + [pl.BlockSpec((k, d), const), pl.BlockSpec((tm, d), row),
                               pl.BlockSpec((1, d), const), pl.BlockSpec((1, d), const)],
        out_specs=out_specs,
        out_shape=out_shape,
        compiler_params=_params("arbitrary"),
        name="attn_out_ln",
    )(*os_, *ls_, w, x, g, b)


def _router_kernel(x_ref, wt_ref, b_ref, eidx_ref, gate_ref, pos_ref, cnt_ref, base_ref, *, tm):
    @pl.when(pl.program_id(0) == 0)
    def _():
        base_ref[...] = jnp.zeros_like(base_ref)

    x = x_ref[...]
    w = wt_ref[...]
    xh = x.astype(BF16)
    xl = (x - xh.astype(F32)).astype(BF16)
    wh = w.astype(BF16)
    wl = (w - wh.astype(F32)).astype(BF16)
    dg = functools.partial(lax.dot_general, dimension_numbers=_NT, preferred_element_type=F32)
    logits = dg(wh, xh) + (dg(wh, xl) + dg(wl, xh))
    scores = 1.0 / (1.0 + jnp.exp(-logits))
    choice = scores + b_ref[...]

    iota_g = lax.broadcasted_iota(jnp.int32, (GROUP_SIZE, tm), 0)
    gs = []
    for gi in range(N_EXPERT_GROUPS):
        cg = choice[gi * GROUP_SIZE:(gi + 1) * GROUP_SIZE, :]
        t1 = jnp.max(cg, axis=0, keepdims=True)
        i1 = jnp.min(jnp.where(cg == t1, iota_g, GROUP_SIZE), axis=0, keepdims=True)
        t2 = jnp.max(jnp.where(iota_g == i1, -jnp.inf, cg), axis=0, keepdims=True)
        gs.append(t1 + t2)
    masked = []
    for gi in range(N_EXPERT_GROUPS):
        rank = jnp.zeros((1, tm), jnp.int32)
        for gj in range(N_EXPERT_GROUPS):
            if gj != gi:
                beats = (gs[gj] >= gs[gi]) if gj < gi else (gs[gj] > gs[gi])
                rank = rank + beats.astype(jnp.int32)
        masked.append(jnp.where(rank < TOPK_GROUPS, choice[gi * GROUP_SIZE:(gi + 1) * GROUP_SIZE, :], -jnp.inf))
    c = jnp.concatenate(masked, axis=0)

    iota_e = lax.broadcasted_iota(jnp.int32, (N_EXPERTS, tm), 0)
    picks, gates = [], []
    sel = jnp.zeros((N_EXPERTS, tm), F32)
    for _ in range(TOP_K):
        m = jnp.max(c, axis=0, keepdims=True)
        idx = jnp.min(jnp.where(c == m, iota_e, N_EXPERTS), axis=0, keepdims=True)
        hit = iota_e == idx
        picks.append(hit)
        gates.append(jnp.sum(jnp.where(hit, scores, 0.0), axis=0, keepdims=True))
        sel = jnp.where(hit, 1.0, sel)
        c = jnp.where(hit, -jnp.inf, c)
    gsum = gates[0]
    for gk in gates[1:]:
        gsum = gsum + gk

    before = (lax.broadcasted_iota(jnp.int32, (tm, tm), 0) < lax.broadcasted_iota(jnp.int32, (tm, tm), 1))
    rank_in_tile = _dot(sel.astype(BF16), before.astype(BF16))
    base = base_ref[...]
    posfull = base[:, 0:1] + rank_in_tile
    efull = iota_e.astype(F32)
    for k in range(TOP_K):
        hit = picks[k]
        eidx_ref[k:k + 1, :] = jnp.sum(jnp.where(hit, efull, 0.0), axis=0, keepdims=True).astype(jnp.int32)
        pos_ref[k:k + 1, :] = jnp.sum(jnp.where(hit, posfull, 0.0), axis=0, keepdims=True).astype(jnp.int32)
        gate_ref[k:k + 1, :] = gates[k] / gsum * ROUTED_SCALE
    new_base = base + jnp.sum(sel, axis=1, keepdims=True)
    base_ref[...] = new_base
    cnt_ref[...] = new_base


def _route(x, router_wt, router_b, tm=512):
    t, d = x.shape
    tok = pl.BlockSpec((TOP_K, tm), lambda i: (0, i))
    cnt = pl.BlockSpec((N_EXPERTS, LANES), lambda i: (0, 0))
    return pl.pallas_call(
        functools.partial(_router_kernel, tm=tm),
        grid=(t // tm,),
        in_specs=[pl.BlockSpec((tm, d), lambda i: (i, 0)),
                  pl.BlockSpec((N_EXPERTS, d), lambda i: (0, 0)),
                  pl.BlockSpec((N_EXPERTS, 1), lambda i: (0, 0))],
        out_specs=[tok, tok, tok, cnt],
        out_shape=[jax.ShapeDtypeStruct((TOP_K, t), jnp.int32), jax.ShapeDtypeStruct((TOP_K, t), F32),
                   jax.ShapeDtypeStruct((TOP_K, t), jnp.int32), jax.ShapeDtypeStruct((N_EXPERTS, LANES), F32)],
        scratch_shapes=[pltpu.VMEM((N_EXPERTS, LANES), F32)],
        compiler_params=_params("arbitrary"),
        name="router",
    )(x, router_wt, router_b)


def _row(ref, r):
    return ref.at[pl.ds(r, 1)]


def _dest_kernel(pstart_ref, eidx_ref, pos_ref, dest_ref):
    e = eidx_ref[...]
    acc = pos_ref[...]
    for k in range(N_EXPERTS):
        acc = acc + jnp.where(e == k, pstart_ref[k], 0)
    dest_ref[...] = acc


def _dest_rows(pstart, eidx, pos):
    full = pl.BlockSpec(eidx.shape, lambda i, ps: (0, 0))
    return pl.pallas_call(
        _dest_kernel,
        grid_spec=pltpu.PrefetchScalarGridSpec(num_scalar_prefetch=1, grid=(1,), in_specs=[full, full],
                                               out_specs=full),
        out_shape=jax.ShapeDtypeStruct(eidx.shape, jnp.int32),
        compiler_params=_params("arbitrary"),
        name="dest_rows",
    )(pstart, eidx, pos)


def _dispatch_kernel(pad_lo_ref, pad_hi_ref, dest_ref, x_ref, gate_ref, xs_hbm, src_ref, zero_ref, sem, pad_sem,
                     *, tm, n_blk, n_tok):
    i = pl.program_id(0)

    @pl.when(i == 0)
    def _():
        zero_ref[...] = jnp.zeros_like(zero_ref)

        def fill(lo, hi, copy):
            def start(r, c):
                copy(r).start()
                return c

            def wait(r, c):
                copy(r).wait()
                return c

            lax.fori_loop(lo, hi, start, 0)
            lax.fori_loop(lo, hi, wait, 0)

        def per_expert(e, carry):
            fill(pad_lo_ref[e], pad_hi_ref[e],
                 lambda r: pltpu.make_async_copy(_row(zero_ref, 0), _row(xs_hbm, r), pad_sem))
            return carry

        lax.fori_loop(0, N_EXPERTS, per_expert, 0)
        fill(pad_hi_ref[N_EXPERTS - 1] // ROW_BLOCK, n_blk,
             lambda blk: pltpu.make_async_copy(
                 zero_ref, xs_hbm.at[pl.ds(pl.multiple_of(blk * ROW_BLOCK, ROW_BLOCK), ROW_BLOCK)], pad_sem))

    x = x_ref[...]
    gate_bits = lax.bitcast_convert_type(gate_ref[...], jnp.int32)
    lane = lax.broadcasted_iota(jnp.int32, (tm, LANES), 1)
    token = lax.broadcasted_iota(jnp.int32, (tm, LANES), 0) + i * tm
    half = x.shape[1]
    for k in range(TOP_K):
        src_ref[k, :, 0:half] = x
        src_ref[k, :, half:] = jnp.where(lane == 0, token + k * n_tok,
                                         jnp.where(lane == 1, gate_bits[:, k:k + 1], 0))

    def per_token(j, carry):
        for k in range(TOP_K):
            pltpu.make_async_copy(src_ref.at[k, pl.ds(j, 1)], _row(xs_hbm, dest_ref[0, 0, j * TOP_K + k]),
                                  sem).start()
        return carry

    lax.fori_loop(0, tm, per_token, 0)
    for k in range(TOP_K):
        pltpu.make_async_copy(src_ref.at[k], xs_hbm.at[pl.ds(0, tm)], sem).wait()


def _dispatch(xp, dest_tok, gate_tok, pad_lo, pad_hi, n_rows, tm=512):
    t, half = xp.shape
    width = half + LANES
    n_tiles = t // tm
    dest3 = dest_tok.reshape(n_tiles, 1, tm * TOP_K)
    grid_spec = pltpu.PrefetchScalarGridSpec(
        num_scalar_prefetch=2,
        grid=(n_tiles,),
        in_specs=[pl.BlockSpec((1, 1, tm * TOP_K), lambda i, lo, hi: (i, 0, 0), memory_space=pltpu.SMEM),
                  pl.BlockSpec((tm, half), lambda i, lo, hi: (i, 0)),
                  pl.BlockSpec((tm, TOP_K), lambda i, lo, hi: (i, 0))],
        out_specs=pl.BlockSpec(memory_space=pl.ANY),
        scratch_shapes=[pltpu.VMEM((TOP_K, tm, width), jnp.int32),
                        pltpu.VMEM((ROW_BLOCK, width), jnp.int32),
                        pltpu.SemaphoreType.DMA(()), pltpu.SemaphoreType.DMA(())],
    )
    return pl.pallas_call(
        functools.partial(_dispatch_kernel, tm=tm, n_blk=n_rows // ROW_BLOCK, n_tok=t),
        grid_spec=grid_spec,
        out_shape=jax.ShapeDtypeStruct((n_rows, width), jnp.int32),
        compiler_params=_params("arbitrary"),
        name="dispatch",
    )(pad_lo, pad_hi, dest3, xp, gate_tok)


def _expert_kernel(blk_e_ref, nused_ref, nvalid_ref, xs_ref, wgu_ref, wd_ref, out_hbm,
                   wgu_bf, wd_bf, ybuf, idv, ids_smem, sem, id_sem, *, n_blk, n_slots):
    i = pl.program_id(0)
    nused = nused_ref[0]
    s = i % 2
    half = ybuf.shape[2]

    def rows_copy(parity):
        return pltpu.make_async_copy(ybuf.at[parity], out_hbm.at[pl.ds(0, ROW_BLOCK)], sem.at[parity])

    @pl.when(i == 0)
    def _():
        ybuf[...] = jnp.zeros_like(ybuf)
        for p in range(2):
            spill = pltpu.make_async_copy(ybuf.at[p], out_hbm.at[pl.ds(n_slots + p * ROW_BLOCK, ROW_BLOCK)],
                                          sem.at[p])
            spill.start()
            spill.wait()

    @pl.when((i >= 2) & (i - 2 < nused))
    def _():
        rows_copy(s).wait()

    @pl.when(i < nused)
    def _():
        @pl.when((i == 0) | (blk_e_ref[i] != blk_e_ref[jnp.maximum(i - 1, 0)]))
        def _():
            wgu_bf[...] = wgu_ref[0, 0].astype(BF16)
            wd_bf[...] = wd_ref[0, 0].astype(BF16)

        meta = xs_ref[:, half:]
        gate = lax.bitcast_convert_type(meta[:, 1:2], F32)
        hgu = _dot(_unpack_rows(xs_ref[:, 0:half]), wgu_bf[...])
        h = _silu(hgu[:, :EXPERT_DIM]) * hgu[:, EXPERT_DIM:]
        ybuf[s] = _pack_rows(_dot(h.astype(BF16), wd_bf[...]) * gate)
        row = lax.broadcasted_iota(jnp.int32, meta.shape, 0)
        dst = jnp.where(row < nvalid_ref[i], meta, n_slots + s * ROW_BLOCK + row)
        idv[s] = jnp.transpose(dst)[0:idv.shape[1], :]
        pltpu.make_async_copy(idv.at[s], ids_smem.at[s], id_sem.at[s]).start()

    for p in range(2):
        @pl.when((i >= 1) & (i - 1 < nused) & (s == 1 - p))
        def _(p=p):
            pltpu.make_async_copy(idv.at[p], ids_smem.at[p], id_sem.at[p]).wait()
            for r in range(ROW_BLOCK):
                pltpu.make_async_copy(ybuf.at[p, pl.ds(r, 1)], _row(out_hbm, ids_smem[p, 0, r]),
                                      sem.at[p]).start()

            @pl.when(i == n_blk - 1)
            def _():
                rows_copy(p).wait()


def _experts(xs, blk_e, nused, nvalid, w_gu, w_down, layer, n_slots):
    n_rows, width = xs.shape
    half = width - LANES
    n_blk = n_rows // ROW_BLOCK
    d, f2 = w_gu.shape[2], w_gu.shape[3]
    blk = lambda i, be, nu, nv: (jnp.minimum(i, nu[0] - 1), 0)
    wblk = lambda i, be, nu, nv: (layer, be[jnp.minimum(i, nu[0] - 1)], 0, 0)
    grid_spec = pltpu.PrefetchScalarGridSpec(
        num_scalar_prefetch=3,
        grid=(n_blk,),
        in_specs=[pl.BlockSpec((ROW_BLOCK, width), blk),
                  pl.BlockSpec((1, 1, d, f2), wblk),
                  pl.BlockSpec((1, 1, f2 // 2, d), wblk)],
        out_specs=pl.BlockSpec(memory_space=pl.ANY),
        scratch_shapes=[pltpu.VMEM((d, f2), BF16), pltpu.VMEM((f2 // 2, d), BF16),
                        pltpu.VMEM((2, ROW_BLOCK, half), jnp.int32),
                        pltpu.VMEM((2, 8, ROW_BLOCK), jnp.int32),
                        pltpu.SMEM((2, 8, ROW_BLOCK), jnp.int32),
                        pltpu.SemaphoreType.DMA((2,)), pltpu.SemaphoreType.DMA((2,))],
    )
    return pl.pallas_call(
        functools.partial(_expert_kernel, n_blk=n_blk, n_slots=n_slots),
        grid_spec=grid_spec,
        out_shape=jax.ShapeDtypeStruct((n_slots + 2 * ROW_BLOCK, half), jnp.int32),
        compiler_params=_params("arbitrary"),
        name="experts",
    )(blk_e, nused, nvalid, xs, w_gu, w_down)


def _combine_kernel(*refs):
    y_refs = refs[:TOP_K]
    xf_ref, xb_ref, sgu_ref, sd_ref, g_ref, b_ref, of_ref, ob_ref = refs[TOP_K:]
    lo = hi = None
    for y_ref in y_refs:
        p = y_ref[...]
        lo_k = lax.bitcast_convert_type(lax.shift_left(p, 16), F32)
        hi_k = lax.bitcast_convert_type(p & HIGH_HALF, F32)
        lo = lo_k if lo is None else lo + lo_k
        hi = hi_k if hi is None else hi + hi_k
    routed = jnp.concatenate([lo, hi], axis=1)

    sgu = _dot(xb_ref[...], sgu_ref[...])
    sh = _silu(sgu[:, :SHARED_DIM]) * sgu[:, SHARED_DIM:]
    shared = _dot(sh.astype(BF16), sd_ref[...])
    z = ALPHA * xf_ref[...] + (routed + shared)
    y = _layer_norm(z, g_ref[...], b_ref[...])
    of_ref[...] = y
    ob_ref[...] = y.astype(BF16)


def _combine(y, xf, xb, sw_gu, sw_down, g, b, tm=256):
    t, d = xf.shape
    n_tiles = t // tm
    row = lambda i: (i, 0)
    const = lambda i: (0, 0)
    y_specs = [pl.BlockSpec((tm, d // 2), functools.partial(lambda i, k: (k * n_tiles + i, 0), k=k))
               for k in range(TOP_K)]
    return pl.pallas_call(
        _combine_kernel,
        grid=(n_tiles,),
        in_specs=y_specs + [pl.BlockSpec((tm, d), row), pl.BlockSpec((tm, d), row),
                            pl.BlockSpec(sw_gu.shape, const), pl.BlockSpec(sw_down.shape, const),
                            pl.BlockSpec((1, d), const), pl.BlockSpec((1, d), const)],
        out_specs=[pl.BlockSpec((tm, d), row), pl.BlockSpec((tm, d), row)],
        out_shape=[jax.ShapeDtypeStruct((t, d), F32), jax.ShapeDtypeStruct((t, d), BF16)],
        compiler_params=_params("arbitrary"),
        name="combine_ln",
    )(*([y] * TOP_K), xf, xb, sw_gu, sw_down, g, b)


def _moe_layer(xf, xb, xp, router_w, router_b, w_gu, w_down, layer, sw_gu, sw_down, g, b):
    t, d = xf.shape
    eidx, gate, pos, cnt = _route(xf, router_w.T, router_b.reshape(N_EXPERTS, 1))

    sizes = cnt[:, 0].astype(jnp.int32)
    padded = (sizes + ROW_BLOCK - 1) // ROW_BLOCK * ROW_BLOCK
    pend = jnp.cumsum(padded)
    pstart = pend - padded
    tk = t * TOP_K
    n_rows = -(-tk // ROW_BLOCK) * ROW_BLOCK + N_EXPERTS * ROW_BLOCK
    blk_start = jnp.arange(n_rows // ROW_BLOCK, dtype=jnp.int32) * ROW_BLOCK
    blk_e = jnp.minimum(jnp.sum(pend[None, :] <= blk_start[:, None], axis=1), N_EXPERTS - 1).astype(jnp.int32)
    nused = (pend[-1:] // ROW_BLOCK).astype(jnp.int32)

    real_end = pstart + sizes
    nvalid = jnp.clip(real_end[blk_e] - blk_start, 0, ROW_BLOCK).astype(jnp.int32)

    dest_tok = _dest_rows(pstart, eidx, pos).T
    xs = _dispatch(xp, dest_tok, gate.T, real_end, pend, n_rows)
    y = _experts(xs, blk_e, nused, nvalid, w_gu, w_down, layer, tk)
    return _combine(y, xf, xb, sw_gu, sw_down, g, b)


def kernel(x, rel_bias, pool_w_in, pool_w_group, pool_scale, pool_w_out, dil_w_qkv, dil_w_out,
           conv_w_in, conv_w, conv_w_out, ln_gain, ln_bias, router_w, router_bias,
           expert_w_gu, expert_w_down, shared_w_gu, shared_w_down):
    bn, s, d = x.shape
    t = bn * s
    xf = x.reshape(t, d)
    xb = xf.astype(BF16)
    bf = lambda a: a.astype(BF16)
    ia = ib = ic = 0
    for i in range(DEPTH):
        kind = i % 3
        g1, b1 = ln_gain[i, 0].reshape(1, d), ln_bias[i, 0].reshape(1, d)
        g2, b2 = ln_gain[i, 1].reshape(1, d), ln_bias[i, 1].reshape(1, d)
        if kind == 0:
            mixed = _pool_mix(xb, bf(pool_w_in[ia]), bf(pool_w_group[ia]), pool_scale[ia].reshape(1, d), s)
            xf, xb, xp = _proj_ln(mixed, bf(pool_w_out[ia]), xf, g1, b1)
            ia += 1
        elif kind == 1:
            qkv = _matmul(xb, bf(dil_w_qkv[ib]), tm=512, tn=3 * DIL_WIDTH)
            os_, ls_ = [], []
            for g in range(len(DIL_CONFIGS)):
                o, l = _dilated_group(qkv, rel_bias[:, g * DIL_HEADS:(g + 1) * DIL_HEADS], g, bn, s)
                os_.append(o)
                ls_.append(l)
            xf, xb, xp = _attn_out(os_, ls_, bf(dil_w_out[ib]), xf, g1, b1)
            ib += 1
        else:
            v = _conv_mix(xb, bf(conv_w_in[ic]), conv_w[ic], s)
            xf, xb, xp = _proj_ln(v, bf(conv_w_out[ic]), xf, g1, b1)
            ic += 1
        xf, xb = _moe_layer(xf, xb, xp, router_w[i], router_bias[i], expert_w_gu, expert_w_down, i,
                            bf(shared_w_gu[i]), bf(shared_w_down[i]), g2, b2)
    return xf.reshape(bn, s, d)
```

```python
import functools
import math

import jax
import jax.numpy as jnp
from jax import lax
from jax.experimental import pallas as pl
from jax.experimental.pallas import tpu as pltpu

F32 = jnp.float32
BF16 = jnp.bfloat16

DEPTH = 4
D_MODEL = 2048
POOL_WINDOWS = (2, 4, 8, 16)
POOL_GROUP_DIM = D_MODEL // len(POOL_WINDOWS)
POOL_HALO = 16
DIL_CONFIGS = ((128, 1), (512, 4), (2048, 16))
HEAD_DIM = 64
DIL_HEADS = 8
DIL_WIDTH = DIL_HEADS * HEAD_DIM
DIL_BLOCK = 128
N_BUCKETS = 32
MAX_DISTANCE = 2048
CONV_WIDTH = 3
CONV_HALO = 8
N_EXPERTS = 64
TOP_K = 8
N_EXPERT_GROUPS = 8
GROUP_SIZE = N_EXPERTS // N_EXPERT_GROUPS
TOPK_GROUPS = 4
EXPERT_DIM = 128
SHARED_DIM = 256
ROUTED_SCALE = 2.5
ROW_BLOCK = 256
ALPHA = (2.0 * DEPTH) ** 0.25
LN_EPS = 1e-5
NEG = -1e30

LANES = 128
HIGH_HALF = -65536
SCATTER_CHUNKS = 8
VMEM_LIMIT = 52 * 1024 * 1024

_NT = (((1,), (1,)), ((), ()))


def _dot(a, b):
    return jnp.dot(a, b, preferred_element_type=F32)


def _params(*sem):
    return pltpu.CompilerParams(dimension_semantics=sem, vmem_limit_bytes=VMEM_LIMIT)


def _layer_norm(z, g, b):
    mu = jnp.mean(z, axis=-1, keepdims=True)
    zc = z - mu
    var = jnp.mean(zc * zc, axis=-1, keepdims=True)
    return zc * lax.rsqrt(var + LN_EPS) * g + b


def _silu(v):
    return v / (1.0 + jnp.exp(-v))


def _pack_rows(y):
    half = D_MODEL // 2
    lo = lax.bitcast_convert_type(y[:, :half].astype(BF16).astype(F32), jnp.int32)
    hi = lax.bitcast_convert_type(y[:, half:].astype(BF16).astype(F32), jnp.int32)
    return (hi & HIGH_HALF) | lax.shift_right_logical(lo, 16)


def _unpack_rows(p):
    lo = lax.bitcast_convert_type(lax.shift_left(p, 16), F32)
    hi = lax.bitcast_convert_type(p & HIGH_HALF, F32)
    return jnp.concatenate([lo, hi], axis=1).astype(BF16)


def _store_ln_outputs(y, of_ref, ob_ref, op_ref):
    of_ref[...] = y
    ob_ref[...] = y.astype(BF16)
    op_ref[...] = _pack_rows(y)


def _ln_out_specs(t, d, tm):
    row = lambda i: (i, 0)
    specs = [pl.BlockSpec((tm, d), row), pl.BlockSpec((tm, d), row), pl.BlockSpec((tm, d // 2), row)]
    shapes = [jax.ShapeDtypeStruct((t, d), F32), jax.ShapeDtypeStruct((t, d), BF16),
              jax.ShapeDtypeStruct((t, d // 2), jnp.int32)]
    return specs, shapes


def _mm_kernel(x_ref, w_ref, o_ref):
    o_ref[...] = _dot(x_ref[...], w_ref[...]).astype(o_ref.dtype)


def _matmul(x, w, tm, tn):
    t, k = x.shape
    n = w.shape[1]
    return pl.pallas_call(
        _mm_kernel,
        grid=(n // tn, t // tm),
        in_specs=[pl.BlockSpec((tm, k), lambda j, i: (i, 0)),
                  pl.BlockSpec((k, tn), lambda j, i: (0, j))],
        out_specs=pl.BlockSpec((tm, tn), lambda j, i: (i, j)),
        out_shape=jax.ShapeDtypeStruct((t, n), BF16),
        compiler_params=_params("parallel", "arbitrary"),
        name="matmul",
    )(x, w)


def _proj_ln_kernel(a_ref, w_ref, x_ref, g_ref, b_ref, of_ref, ob_ref, op_ref):
    z = ALPHA * x_ref[...] + _dot(a_ref[...], w_ref[...])
    _store_ln_outputs(_layer_norm(z, g_ref[...], b_ref[...]), of_ref, ob_ref, op_ref)


def _proj_ln(a, w, x, g, b, tm=256):
    t, k = a.shape
    d = w.shape[1]
    row = lambda i: (i, 0)
    const = lambda i: (0, 0)
    out_specs, out_shape = _ln_out_specs(t, d, tm)
    return pl.pallas_call(
        _proj_ln_kernel,
        grid=(t // tm,),
        in_specs=[pl.BlockSpec((tm, k), row), pl.BlockSpec((k, d), const),
                  pl.BlockSpec((tm, d), row), pl.BlockSpec((1, d), const),
                  pl.BlockSpec((1, d), const)],
        out_specs=out_specs,
        out_shape=out_shape,
        compiler_params=_params("arbitrary"),
        name="proj_ln",
    )(a, w, x, g, b)


def _pool_kernel(x_ref, win_ref, wg_ref, sc_ref, o_ref, carry_ref, *, tm, tiles_per_seq):
    i = pl.program_id(0)
    seq_tile = i % tiles_per_seq

    @pl.when(seq_tile == 0)
    def _():
        carry_ref[...] = jnp.zeros_like(carry_ref)

    xb = x_ref[...]
    pos = lax.broadcasted_iota(jnp.int32, (tm, 1), 0) + seq_tile * tm
    gd = POOL_GROUP_DIM
    for g, w in enumerate(POOL_WINDOWS):
        cols = slice(g * gd, (g + 1) * gd)
        u = _dot(xb, win_ref[:, cols])
        s = jnp.concatenate([carry_ref[g], u], axis=0)
        sh = 1
        while sh < w:
            s = s + pltpu.roll(s, sh, axis=0)
            sh *= 2
        cnt = jnp.minimum(pos + 1, w).astype(F32)
        pooled = s[POOL_HALO:, :] / cnt - u
        carry_ref[g] = u[tm - POOL_HALO:, :]
        mixed = _dot(pooled.astype(BF16), wg_ref[g]) * sc_ref[:, cols]
        o_ref[:, cols] = mixed.astype(BF16)


def _pool_mix(xb, w_in, w_group, scale, seq_len, tm=256):
    t, d = xb.shape
    gd = POOL_GROUP_DIM
    ng = len(POOL_WINDOWS)
    return pl.pallas_call(
        functools.partial(_pool_kernel, tm=tm, tiles_per_seq=seq_len // tm),
        grid=(t // tm,),
        in_specs=[pl.BlockSpec((tm, d), lambda i: (i, 0)),
                  pl.BlockSpec((d, d), lambda i: (0, 0)),
                  pl.BlockSpec((ng, gd, gd), lambda i: (0, 0, 0)),
                  pl.BlockSpec((1, d), lambda i: (0, 0))],
        out_specs=pl.BlockSpec((tm, d), lambda i: (i, 0)),
        out_shape=jax.ShapeDtypeStruct((t, d), BF16),
        scratch_shapes=[pltpu.VMEM((ng, POOL_HALO, gd), F32)],
        compiler_params=_params("arbitrary"),
        name="pool_mix",
    )(xb, w_in, w_group, scale)


def _conv_kernel(x_ref, wb_ref, wc_ref, wh_ref, cw_ref, o_ref, carry_ref, *, tm, tiles_per_seq):
    i = pl.program_id(1)

    @pl.when(i % tiles_per_seq == 0)
    def _():
        carry_ref[...] = jnp.zeros_like(carry_ref)

    xb = x_ref[...]
    b_gate = _dot(xb, wb_ref[...])
    u = _dot(xb, wc_ref[...]) * _dot(xb, wh_ref[...])
    ext = jnp.concatenate([carry_ref[...], u], axis=0)
    y = cw_ref[CONV_WIDTH - 1:CONV_WIDTH, :] * u
    for back in range(1, CONV_WIDTH):
        tap = CONV_WIDTH - 1 - back
        y = y + cw_ref[tap:tap + 1, :] * pltpu.roll(ext, back, axis=0)[CONV_HALO:, :]
    carry_ref[...] = u[tm - CONV_HALO:, :]
    o_ref[...] = (b_gate * y).astype(BF16)


def _conv_mix(xb, w_in, conv_w, seq_len, tm=512, tn=512):
    t, d = xb.shape
    nj = d // tn
    return pl.pallas_call(
        functools.partial(_conv_kernel, tm=tm, tiles_per_seq=seq_len // tm),
        grid=(nj, t // tm),
        in_specs=[pl.BlockSpec((tm, d), lambda j, i: (i, 0)),
                  pl.BlockSpec((d, tn), lambda j, i: (0, j)),
                  pl.BlockSpec((d, tn), lambda j, i: (0, nj + j)),
                  pl.BlockSpec((d, tn), lambda j, i: (0, 2 * nj + j)),
                  pl.BlockSpec((CONV_WIDTH, tn), lambda j, i: (0, j))],
        out_specs=pl.BlockSpec((tm, tn), lambda j, i: (i, j)),
        out_shape=jax.ShapeDtypeStruct((t, d), BF16),
        scratch_shapes=[pltpu.VMEM((CONV_HALO, tn), F32)],
        compiler_params=_params("parallel", "arbitrary"),
        name="conv_mix",
    )(xb, w_in, w_in, w_in, conv_w)


def _t5_bucket(dist):
    max_exact = N_BUCKETS // 2
    is_small = dist < max_exact
    distf = jnp.maximum(dist, 1).astype(F32)
    large = max_exact + (jnp.log(distf / max_exact) / math.log(MAX_DISTANCE / max_exact)
                         * (N_BUCKETS - max_exact)).astype(jnp.int32)
    large = jnp.minimum(large, N_BUCKETS - 1)
    return jnp.where(is_small, dist, large)


def _attn_bias(bias_tab, window, dil, has_prev):
    steps = window // dil
    q_loc = jnp.arange(DIL_BLOCK)[:, None]
    k_loc = jnp.arange(2 * DIL_BLOCK)[None, :]
    rel = q_loc + DIL_BLOCK - k_loc
    band = (rel >= 0) & (rel <= steps)
    bucket = _t5_bucket(jnp.maximum(rel, 0) * dil)
    tab = bias_tab.astype(F32).T[:, :, None, None]
    bias = sum(jnp.where(bucket == b, tab[:, b], 0.0) for b in range(N_BUCKETS))
    bias = jnp.where(band[None], bias, NEG)
    return bias if has_prev else bias[:, :, DIL_BLOCK:]


def _attn_kernel(*refs, has_prev):
    if has_prev:
        q_ref, kc_ref, kp_ref, vc_ref, vp_ref, bias_ref, o_ref, l_ref = refs
    else:
        q_ref, kc_ref, vc_ref, bias_ref, o_ref, l_ref = refs
    n = pl.program_id(1)
    q = q_ref[0]
    kc = kc_ref[0]
    vc = vc_ref[0]
    if has_prev:
        kp = kp_ref[0]
        vp = vp_ref[0]
        col = lax.broadcasted_iota(jnp.int32, (DIL_BLOCK, 2 * DIL_BLOCK), 1)
        keep = (col >= DIL_BLOCK) | (n > 0)
    outs, lses = [], []
    for h in range(DIL_HEADS):
        sl = slice(h * HEAD_DIM, (h + 1) * HEAD_DIM)
        if has_prev:
            kh = jnp.concatenate([kp[:, sl], kc[:, sl]], axis=0)
            vh = jnp.concatenate([vp[:, sl], vc[:, sl]], axis=0)
        else:
            kh, vh = kc[:, sl], vc[:, sl]
        s = lax.dot_general(q[:, sl], kh, _NT, preferred_element_type=F32) * (HEAD_DIM ** -0.5) + bias_ref[h]
        if has_prev:
            s = jnp.where(keep, s, NEG)
        m = jnp.max(s, axis=-1, keepdims=True)
        p = jnp.exp(s - m)
        l = jnp.sum(p, axis=-1, keepdims=True)
        outs.append(_dot(p.astype(BF16), vh) / l)
        lses.append(jnp.broadcast_to(m + jnp.log(l), (DIL_BLOCK, HEAD_DIM)))
    o_ref[0] = jnp.concatenate(outs, axis=1)
    l_ref[0] = jnp.concatenate(lses, axis=1)


def _dilated_group(qkv, bias_tab, g, batch, seq_len):
    window, dil = DIL_CONFIGS[g]
    span = DIL_BLOCK * dil
    nb = seq_len // span
    has_prev = nb > 1
    ncol = qkv.shape[1] // DIL_WIDTH
    view = qkv.reshape(batch, seq_len // dil, dil * qkv.shape[1])
    blk = (1, DIL_BLOCK, DIL_WIDTH)

    def cur(kind):
        return pl.BlockSpec(blk, lambda b, n, r: (b, n, r * ncol + 3 * g + kind))

    def prev(kind):
        return pl.BlockSpec(blk, lambda b, n, r: (b, jnp.maximum(n - 1, 0), r * ncol + 3 * g + kind))

    bias = _attn_bias(bias_tab, window, dil, has_prev)
    bias_spec = pl.BlockSpec(bias.shape, lambda b, n, r: (0, 0, 0))
    if has_prev:
        in_specs = [cur(0), cur(1), prev(1), cur(2), prev(2), bias_spec]
        args = (view,) * 5 + (bias,)
    else:
        in_specs = [cur(0), cur(1), cur(2), bias_spec]
        args = (view,) * 3 + (bias,)
    out_spec = pl.BlockSpec(blk, lambda b, n, r: (b, n, r))
    out_sds = jax.ShapeDtypeStruct((batch, seq_len // dil, dil * DIL_WIDTH), F32)
    o, l = pl.pallas_call(
        functools.partial(_attn_kernel, has_prev=has_prev),
        grid=(batch, nb, dil),
        in_specs=in_specs,
        out_specs=[out_spec, out_spec],
        out_shape=[out_sds, out_sds],
        compiler_params=_params("parallel", "arbitrary", "arbitrary"),
        name=f"dilated_attn_g{g}",
    )(*args)
    t = batch * seq_len
    return o.reshape(t, DIL_WIDTH), l.reshape(t, DIL_WIDTH)


def _attn_out_kernel(o0, o1, o2, l0, l1, l2, w_ref, x_ref, g_ref, b_ref, of_ref, ob_ref, op_ref):
    a0, a1, a2 = l0[...], l1[...], l2[...]
    m = jnp.maximum(jnp.maximum(a0, a1), a2)
    e0, e1, e2 = jnp.exp(a0 - m), jnp.exp(a1 - m), jnp.exp(a2 - m)
    o = (e0 * o0[...] + e1 * o1[...] + e2 * o2[...]) / (e0 + e1 + e2)
    z = ALPHA * x_ref[...] + _dot(o.astype(BF16), w_ref[...])
    _store_ln_outputs(_layer_norm(z, g_ref[...], b_ref[...]), of_ref, ob_ref, op_ref)


def _attn_out(os_, ls_, w, x, g, b, tm=256):
    t, d = x.shape
    k = w.shape[0]
    row = lambda i: (i, 0)
    const = lambda i: (0, 0)
    part = pl.BlockSpec((tm, k), row)
    out_specs, out_shape = _ln_out_specs(t, d, tm)
    return pl.pallas_call(
        _attn_out_kernel,
        grid=(t // tm,),
        in_specs=[part] * 6 + [pl.BlockSpec((k, d), const), pl.BlockSpec((tm, d), row),
                               pl.BlockSpec((1, d), const), pl.BlockSpec((1, d), const)],
        out_specs=out_specs,
        out_shape=out_shape,
        compiler_params=_params("arbitrary"),
        name="attn_out_ln",
    )(*os_, *ls_, w, x, g, b)


def _router_kernel(x_ref, wt_ref, b_ref, eidx_ref, gate_ref, pos_ref, cnt_ref, base_ref, *, tm):
    @pl.when(pl.program_id(0) == 0)
    def _():
        base_ref[...] = jnp.zeros_like(base_ref)

    x = x_ref[...]
    w = wt_ref[...]
    xh = x.astype(BF16)
    xl = (x - xh.astype(F32)).astype(BF16)
    wh = w.astype(BF16)
    wl = (w - wh.astype(F32)).astype(BF16)
    dg = functools.partial(lax.dot_general, dimension_numbers=_NT, preferred_element_type=F32)
    logits = dg(wh, xh) + (dg(wh, xl) + dg(wl, xh))
    scores = 1.0 / (1.0 + jnp.exp(-logits))
    choice = scores + b_ref[...]

    iota_g = lax.broadcasted_iota(jnp.int32, (GROUP_SIZE, tm), 0)
    gs = []
    for gi in range(N_EXPERT_GROUPS):
        cg = choice[gi * GROUP_SIZE:(gi + 1) * GROUP_SIZE, :]
        t1 = jnp.max(cg, axis=0, keepdims=True)
        i1 = jnp.min(jnp.where(cg == t1, iota_g, GROUP_SIZE), axis=0, keepdims=True)
        t2 = jnp.max(jnp.where(iota_g == i1, -jnp.inf, cg), axis=0, keepdims=True)
        gs.append(t1 + t2)
    masked = []
    for gi in range(N_EXPERT_GROUPS):
        rank = jnp.zeros((1, tm), jnp.int32)
        for gj in range(N_EXPERT_GROUPS):
            if gj != gi:
                beats = (gs[gj] >= gs[gi]) if gj < gi else (gs[gj] > gs[gi])
                rank = rank + beats.astype(jnp.int32)
        masked.append(jnp.where(rank < TOPK_GROUPS, choice[gi * GROUP_SIZE:(gi + 1) * GROUP_SIZE, :], -jnp.inf))
    c = jnp.concatenate(masked, axis=0)

    iota_e = lax.broadcasted_iota(jnp.int32, (N_EXPERTS, tm), 0)
    picks, gates = [], []
    sel = jnp.zeros((N_EXPERTS, tm), F32)
    for _ in range(TOP_K):
        m = jnp.max(c, axis=0, keepdims=True)
        idx = jnp.min(jnp.where(c == m, iota_e, N_EXPERTS), axis=0, keepdims=True)
        hit = iota_e == idx
        picks.append(hit)
        gates.append(jnp.sum(jnp.where(hit, scores, 0.0), axis=0, keepdims=True))
        sel = jnp.where(hit, 1.0, sel)
        c = jnp.where(hit, -jnp.inf, c)
    gsum = gates[0]
    for gk in gates[1:]:
        gsum = gsum + gk

    before = (lax.broadcasted_iota(jnp.int32, (tm, tm), 0) < lax.broadcasted_iota(jnp.int32, (tm, tm), 1))
    rank_in_tile = _dot(sel.astype(BF16), before.astype(BF16))
    base = base_ref[...]
    posfull = base[:, 0:1] + rank_in_tile
    efull = iota_e.astype(F32)
    for k in range(TOP_K):
        hit = picks[k]
        eidx_ref[k:k + 1, :] = jnp.sum(jnp.where(hit, efull, 0.0), axis=0, keepdims=True).astype(jnp.int32)
        pos_ref[k:k + 1, :] = jnp.sum(jnp.where(hit, posfull, 0.0), axis=0, keepdims=True).astype(jnp.int32)
        gate_ref[k:k + 1, :] = gates[k] / gsum * ROUTED_SCALE
    new_base = base + jnp.sum(sel, axis=1, keepdims=True)
    base_ref[...] = new_base
    cnt_ref[...] = new_base


def _route(x, router_wt, router_b, tm=512):
    t, d = x.shape
    tok = pl.BlockSpec((TOP_K, tm), lambda i: (0, i))
    cnt = pl.BlockSpec((N_EXPERTS, LANES), lambda i: (0, 0))
    return pl.pallas_call(
        functools.partial(_router_kernel, tm=tm),
        grid=(t // tm,),
        in_specs=[pl.BlockSpec((tm, d), lambda i: (i, 0)),
                  pl.BlockSpec((N_EXPERTS, d), lambda i: (0, 0)),
                  pl.BlockSpec((N_EXPERTS, 1), lambda i: (0, 0))],
        out_specs=[tok, tok, tok, cnt],
        out_shape=[jax.ShapeDtypeStruct((TOP_K, t), jnp.int32), jax.ShapeDtypeStruct((TOP_K, t), F32),
                   jax.ShapeDtypeStruct((TOP_K, t), jnp.int32), jax.ShapeDtypeStruct((N_EXPERTS, LANES), F32)],
        scratch_shapes=[pltpu.VMEM((N_EXPERTS, LANES), F32)],
        compiler_params=_params("arbitrary"),
        name="router",
    )(x, router_wt, router_b)


def _row(ref, r):
    return ref.at[pl.ds(r, 1)]


def _dest_kernel(pstart_ref, eidx_ref, pos_ref, dest_ref):
    e = eidx_ref[...]
    acc = pos_ref[...]
    for k in range(N_EXPERTS):
        acc = acc + jnp.where(e == k, pstart_ref[k], 0)
    dest_ref[...] = acc


def _dest_rows(pstart, eidx, pos):
    full = pl.BlockSpec(eidx.shape, lambda i, ps: (0, 0))
    return pl.pallas_call(
        _dest_kernel,
        grid_spec=pltpu.PrefetchScalarGridSpec(num_scalar_prefetch=1, grid=(1,), in_specs=[full, full],
                                               out_specs=full),
        out_shape=jax.ShapeDtypeStruct(eidx.shape, jnp.int32),
        compiler_params=_params("arbitrary"),
        name="dest_rows",
    )(pstart, eidx, pos)


def _dispatch_kernel(pad_lo_ref, pad_hi_ref, dest_ref, x_ref, gate_ref, xs_hbm, src_ref, zero_ref, sem, pad_sem,
                     *, tm, n_blk, n_tok):
    i = pl.program_id(0)

    @pl.when(i == 0)
    def _():
        zero_ref[...] = jnp.zeros_like(zero_ref)

        def fill(lo, hi, copy):
            def start(r, c):
                copy(r).start()
                return c

            def wait(r, c):
                copy(r).wait()
                return c

            lax.fori_loop(lo, hi, start, 0)
            lax.fori_loop(lo, hi, wait, 0)

        def per_expert(e, carry):
            fill(pad_lo_ref[e], pad_hi_ref[e],
                 lambda r: pltpu.make_async_copy(_row(zero_ref, 0), _row(xs_hbm, r), pad_sem))
            return carry

        lax.fori_loop(0, N_EXPERTS, per_expert, 0)
        fill(pad_hi_ref[N_EXPERTS - 1] // ROW_BLOCK, n_blk,
             lambda blk: pltpu.make_async_copy(
                 zero_ref, xs_hbm.at[pl.ds(pl.multiple_of(blk * ROW_BLOCK, ROW_BLOCK), ROW_BLOCK)], pad_sem))

    x = x_ref[...]
    gate_bits = lax.bitcast_convert_type(gate_ref[...], jnp.int32)
    lane = lax.broadcasted_iota(jnp.int32, (tm, LANES), 1)
    token = lax.broadcasted_iota(jnp.int32, (tm, LANES), 0) + i * tm
    half = x.shape[1]
    for k in range(TOP_K):
        src_ref[k, :, 0:half] = x
        src_ref[k, :, half:] = jnp.where(lane == 0, token + k * n_tok,
                                         jnp.where(lane == 1, gate_bits[:, k:k + 1], 0))

    def per_token(j, carry):
        for k in range(TOP_K):
            pltpu.make_async_copy(src_ref.at[k, pl.ds(j, 1)], _row(xs_hbm, dest_ref[0, 0, j * TOP_K + k]),
                                  sem).start()
        return carry

    lax.fori_loop(0, tm, per_token, 0)
    for k in range(TOP_K):
        pltpu.make_async_copy(src_ref.at[k], xs_hbm.at[pl.ds(0, tm)], sem).wait()


def _dispatch(xp, dest_tok, gate_tok, pad_lo, pad_hi, n_rows, tm=512):
    t, half = xp.shape
    width = half + LANES
    n_tiles = t // tm
    dest3 = dest_tok.reshape(n_tiles, 1, tm * TOP_K)
    grid_spec = pltpu.PrefetchScalarGridSpec(
        num_scalar_prefetch=2,
        grid=(n_tiles,),
        in_specs=[pl.BlockSpec((1, 1, tm * TOP_K), lambda i, lo, hi: (i, 0, 0), memory_space=pltpu.SMEM),
                  pl.BlockSpec((tm, half), lambda i, lo, hi: (i, 0)),
                  pl.BlockSpec((tm, TOP_K), lambda i, lo, hi: (i, 0))],
        out_specs=pl.BlockSpec(memory_space=pl.ANY),
        scratch_shapes=[pltpu.VMEM((TOP_K, tm, width), jnp.int32),
                        pltpu.VMEM((ROW_BLOCK, width), jnp.int32),
                        pltpu.SemaphoreType.DMA(()), pltpu.SemaphoreType.DMA(())],
    )
    return pl.pallas_call(
        functools.partial(_dispatch_kernel, tm=tm, n_blk=n_rows // ROW_BLOCK, n_tok=t),
        grid_spec=grid_spec,
        out_shape=jax.ShapeDtypeStruct((n_rows, width), jnp.int32),
        compiler_params=_params("arbitrary"),
        name="dispatch",
    )(pad_lo, pad_hi, dest3, xp, gate_tok)


def _expert_kernel(blk_e_ref, nused_ref, nvalid_ref, xs_ref, wgu_ref, wd_ref, out_hbm,
                   wgu_bf, wd_bf, ybuf, idv, ids_smem, sem, id_sem, *, n_blk, n_slots):
    i = pl.program_id(0)
    nused = nused_ref[0]
    s = i % 2
    half = ybuf.shape[2]

    def rows_copy(parity):
        return pltpu.make_async_copy(ybuf.at[parity], out_hbm.at[pl.ds(0, ROW_BLOCK)], sem.at[parity])

    def spill_init(parity):
        return pltpu.make_async_copy(ybuf.at[parity], out_hbm.at[pl.ds(n_slots + parity * ROW_BLOCK, ROW_BLOCK)],
                                     sem.at[parity])

    def compute(par, retire_first, scatter_par=None):
        @pl.when((i == 0) | (blk_e_ref[i] != blk_e_ref[jnp.maximum(i - 1, 0)]))
        def _():
            wgu_bf[...] = wgu_ref[0, 0].astype(BF16)
            wd_bf[...] = wd_ref[0, 0].astype(BF16)

        if scatter_par is not None:
            ids_ready(scatter_par)

        meta = xs_ref[:, half:]
        row = lax.broadcasted_iota(jnp.int32, meta.shape, 0)
        dst = jnp.where(row < nvalid_ref[i], meta, n_slots + par * ROW_BLOCK + row)
        idv[par] = jnp.transpose(dst)[0:idv.shape[1], :]
        pltpu.make_async_copy(idv.at[par], ids_smem.at[par], id_sem.at[par]).start()

        gate = lax.bitcast_convert_type(meta[:, 1:2], F32)
        xs = _unpack_rows(xs_ref[:, 0:half])
        d = xs.shape[1]
        step = d // SCATTER_CHUNKS
        hgu = None
        for c in range(SCATTER_CHUNKS):
            part = _dot(xs[:, c * step:(c + 1) * step], wgu_bf[c * step:(c + 1) * step, :])
            hgu = part if hgu is None else hgu + part
            if scatter_par is not None:
                scatter(scatter_par, c)
        h = _silu(hgu[:, :EXPERT_DIM]) * hgu[:, EXPERT_DIM:]
        y = _pack_rows(_dot(h.astype(BF16), wd_bf[...]) * gate)
        if retire_first:
            rows_copy(par).wait()
        ybuf[par] = y

    def ids_ready(par):
        pltpu.make_async_copy(idv.at[par], ids_smem.at[par], id_sem.at[par]).wait()

    def scatter(par, chunk):
        n = ROW_BLOCK // SCATTER_CHUNKS
        for r in range(chunk * n, (chunk + 1) * n):
            pltpu.make_async_copy(ybuf.at[par, pl.ds(r, 1)], _row(out_hbm, ids_smem[par, 0, r]),
                                  sem.at[par]).start()

    @pl.when(i == 0)
    def _():
        ybuf[...] = jnp.zeros_like(ybuf)
        spill_init(0).start()
        spill_init(0).wait()
        spill_init(1).start()
        compute(0, retire_first=False)

    for par in range(2):
        @pl.when((i >= 1) & (i < nused) & (s == par))
        def _(par=par):
            compute(par, retire_first=True, scatter_par=1 - par)

        @pl.when((i >= 1) & (i == nused) & (s == par))
        def _(par=par):
            ids_ready(1 - par)
            for c in range(SCATTER_CHUNKS):
                scatter(1 - par, c)

    @pl.when((i >= nused) & (i >= 2) & (i - 2 < nused))
    def _():
        rows_copy(s).wait()

    @pl.when((i == n_blk - 1) & (i - 1 < nused))
    def _():
        rows_copy(1 - s).wait()


def _experts(xs, blk_e, nused, nvalid, w_gu, w_down, layer, n_slots):
    n_rows, width = xs.shape
    half = width - LANES
    n_blk = n_rows // ROW_BLOCK
    d, f2 = w_gu.shape[2], w_gu.shape[3]
    blk = lambda i, be, nu, nv: (jnp.minimum(i, nu[0] - 1), 0)
    wblk = lambda i, be, nu, nv: (layer, be[jnp.minimum(i, nu[0] - 1)], 0, 0)
    grid_spec = pltpu.PrefetchScalarGridSpec(
        num_scalar_prefetch=3,
        grid=(n_blk,),
        in_specs=[pl.BlockSpec((ROW_BLOCK, width), blk),
                  pl.BlockSpec((1, 1, d, f2), wblk),
                  pl.BlockSpec((1, 1, f2 // 2, d), wblk)],
        out_specs=pl.BlockSpec(memory_space=pl.ANY),
        scratch_shapes=[pltpu.VMEM((d, f2), BF16), pltpu.VMEM((f2 // 2, d), BF16),
                        pltpu.VMEM((2, ROW_BLOCK, half), jnp.int32),
                        pltpu.VMEM((2, 8, ROW_BLOCK), jnp.int32),
                        pltpu.SMEM((2, 8, ROW_BLOCK), jnp.int32),
                        pltpu.SemaphoreType.DMA((2,)), pltpu.SemaphoreType.DMA((2,))],
    )
    return pl.pallas_call(
        functools.partial(_expert_kernel, n_blk=n_blk, n_slots=n_slots),
        grid_spec=grid_spec,
        out_shape=jax.ShapeDtypeStruct((n_slots + 2 * ROW_BLOCK, half), jnp.int32),
        compiler_params=_params("arbitrary"),
        name="experts",
    )(blk_e, nused, nvalid, xs, w_gu, w_down)


def _combine_kernel(*refs):
    y_refs = refs[:TOP_K]
    xf_ref, xb_ref, sgu_ref, sd_ref, g_ref, b_ref, of_ref, ob_ref = refs[TOP_K:]
    lo = hi = None
    for y_ref in y_refs:
        p = y_ref[...]
        lo_k = lax.bitcast_convert_type(lax.shift_left(p, 16), F32)
        hi_k = lax.bitcast_convert_type(p & HIGH_HALF, F32)
        lo = lo_k if lo is None else lo + lo_k
        hi = hi_k if hi is None else hi + hi_k
    routed = jnp.concatenate([lo, hi], axis=1)

    sgu = _dot(xb_ref[...], sgu_ref[...])
    sh = _silu(sgu[:, :SHARED_DIM]) * sgu[:, SHARED_DIM:]
    shared = _dot(sh.astype(BF16), sd_ref[...])
    z = ALPHA * xf_ref[...] + (routed + shared)
    y = _layer_norm(z, g_ref[...], b_ref[...])
    of_ref[...] = y
    ob_ref[...] = y.astype(BF16)


def _combine(y, xf, xb, sw_gu, sw_down, g, b, tm=256):
    t, d = xf.shape
    n_tiles = t // tm
    row = lambda i: (i, 0)
    const = lambda i: (0, 0)
    y_specs = [pl.BlockSpec((tm, d // 2), functools.partial(lambda i, k: (k * n_tiles + i, 0), k=k))
               for k in range(TOP_K)]
    return pl.pallas_call(
        _combine_kernel,
        grid=(n_tiles,),
        in_specs=y_specs + [pl.BlockSpec((tm, d), row), pl.BlockSpec((tm, d), row),
                            pl.BlockSpec(sw_gu.shape, const), pl.BlockSpec(sw_down.shape, const),
                            pl.BlockSpec((1, d), const), pl.BlockSpec((1, d), const)],
        out_specs=[pl.BlockSpec((tm, d), row), pl.BlockSpec((tm, d), row)],
        out_shape=[jax.ShapeDtypeStruct((t, d), F32), jax.ShapeDtypeStruct((t, d), BF16)],
        compiler_params=_params("arbitrary"),
        name="combine_ln",
    )(*([y] * TOP_K), xf, xb, sw_gu, sw_down, g, b)


def _moe_layer(xf, xb, xp, router_w, router_b, w_gu, w_down, layer, sw_gu, sw_down, g, b):
    t, d = xf.shape
    eidx, gate, pos, cnt = _route(xf, router_w.T, router_b.reshape(N_EXPERTS, 1))

    sizes = cnt[:, 0].astype(jnp.int32)
    padded = (sizes + ROW_BLOCK - 1) // ROW_BLOCK * ROW_BLOCK
    pend = jnp.cumsum(padded)
    pstart = pend - padded
    tk = t * TOP_K
    n_rows = -(-tk // ROW_BLOCK) * ROW_BLOCK + N_EXPERTS * ROW_BLOCK
    blk_start = jnp.arange(n_rows // ROW_BLOCK, dtype=jnp.int32) * ROW_BLOCK
    blk_e = jnp.minimum(jnp.sum(pend[None, :] <= blk_start[:, None], axis=1), N_EXPERTS - 1).astype(jnp.int32)
    nused = (pend[-1:] // ROW_BLOCK).astype(jnp.int32)

    real_end = pstart + sizes
    nvalid = jnp.clip(real_end[blk_e] - blk_start, 0, ROW_BLOCK).astype(jnp.int32)

    dest_tok = _dest_rows(pstart, eidx, pos).T
    xs = _dispatch(xp, dest_tok, gate.T, real_end, pend, n_rows)
    y = _experts(xs, blk_e, nused, nvalid, w_gu, w_down, layer, tk)
    return _combine(y, xf, xb, sw_gu, sw_down, g, b)


def kernel(x, rel_bias, pool_w_in, pool_w_group, pool_scale, pool_w_out, dil_w_qkv, dil_w_out,
           conv_w_in, conv_w, conv_w_out, ln_gain, ln_bias, router_w, router_bias,
           expert_w_gu, expert_w_down, shared_w_gu, shared_w_down):
    bn, s, d = x.shape
    t = bn * s
    xf = x.reshape(t, d)
    xb = xf.astype(BF16)
    bf = lambda a: a.astype(BF16)
    ia = ib = ic = 0
    for i in range(DEPTH):
        kind = i % 3
        g1, b1 = ln_gain[i, 0].reshape(1, d), ln_bias[i, 0].reshape(1, d)
        g2, b2 = ln_gain[i, 1].reshape(1, d), ln_bias[i, 1].reshape(1, d)
        if kind == 0:
            mixed = _pool_mix(xb, bf(pool_w_in[ia]), bf(pool_w_group[ia]), pool_scale[ia].reshape(1, d), s)
            xf, xb, xp = _proj_ln(mixed, bf(pool_w_out[ia]), xf, g1, b1)
            ia += 1
        elif kind == 1:
            qkv = _matmul(xb, bf(dil_w_qkv[ib]), tm=512, tn=3 * DIL_WIDTH)
            os_, ls_ = [], []
            for g in range(len(DIL_CONFIGS)):
                o, l = _dilated_group(qkv, rel_bias[:, g * DIL_HEADS:(g + 1) * DIL_HEADS], g, bn, s)
                os_.append(o)
                ls_.append(l)
            xf, xb, xp = _attn_out(os_, ls_, bf(dil_w_out[ib]), xf, g1, b1)
            ib += 1
        else:
            v = _conv_mix(xb, bf(conv_w_in[ic]), conv_w[ic], s)
            xf, xb, xp = _proj_ln(v, bf(conv_w_out[ic]), xf, g1, b1)
            ic += 1
        xf, xb = _moe_layer(xf, xb, xp, router_w[i], router_bias[i], expert_w_gu, expert_w_down, i,
                            bf(shared_w_gu[i]), bf(shared_w_down[i]), g2, b2)
    return xf.reshape(bn, s, d)
```

```python
import functools
import math

import jax
import jax.numpy as jnp
from jax import lax
from jax.experimental import pallas as pl
from jax.experimental.pallas import tpu as pltpu

F32 = jnp.float32
BF16 = jnp.bfloat16

DEPTH = 4
D_MODEL = 2048
POOL_WINDOWS = (2, 4, 8, 16)
POOL_GROUP_DIM = D_MODEL // len(POOL_WINDOWS)
POOL_HALO = 16
DIL_CONFIGS = ((128, 1), (512, 4), (2048, 16))
HEAD_DIM = 64
DIL_HEADS = 8
DIL_WIDTH = DIL_HEADS * HEAD_DIM
DIL_BLOCK = 128
N_BUCKETS = 32
MAX_DISTANCE = 2048
CONV_WIDTH = 3
CONV_HALO = 8
N_EXPERTS = 64
TOP_K = 8
N_EXPERT_GROUPS = 8
GROUP_SIZE = N_EXPERTS // N_EXPERT_GROUPS
TOPK_GROUPS = 4
EXPERT_DIM = 128
SHARED_DIM = 256
ROUTED_SCALE = 2.5
ROW_BLOCK = 256
ALPHA = (2.0 * DEPTH) ** 0.25
LN_EPS = 1e-5
NEG = -1e30

LANES = 128
HIGH_HALF = -65536
SCATTER_CHUNKS = 8
VMEM_LIMIT = 52 * 1024 * 1024

_NT = (((1,), (1,)), ((), ()))


def _dot(a, b):
    return jnp.dot(a, b, preferred_element_type=F32)


def _params(*sem):
    return pltpu.CompilerParams(dimension_semantics=sem, vmem_limit_bytes=VMEM_LIMIT)


def _layer_norm(z, g, b):
    mu = jnp.mean(z, axis=-1, keepdims=True)
    zc = z - mu
    var = jnp.mean(zc * zc, axis=-1, keepdims=True)
    return zc * lax.rsqrt(var + LN_EPS) * g + b


def _silu(v):
    return v / (1.0 + jnp.exp(-v))


def _pack_rows(y):
    half = D_MODEL // 2
    lo = lax.bitcast_convert_type(y[:, :half].astype(BF16).astype(F32), jnp.int32)
    hi = lax.bitcast_convert_type(y[:, half:].astype(BF16).astype(F32), jnp.int32)
    return (hi & HIGH_HALF) | lax.shift_right_logical(lo, 16)


def _unpack_rows(p):
    lo = lax.bitcast_convert_type(lax.shift_left(p, 16), F32)
    hi = lax.bitcast_convert_type(p & HIGH_HALF, F32)
    return jnp.concatenate([lo, hi], axis=1).astype(BF16)


def _store_ln_outputs(y, of_ref, ob_ref, op_ref):
    of_ref[...] = y
    ob_ref[...] = y.astype(BF16)
    op_ref[...] = _pack_rows(y)


def _ln_out_specs(t, d, tm):
    row = lambda i: (i, 0)
    specs = [pl.BlockSpec((tm, d), row), pl.BlockSpec((tm, d), row), pl.BlockSpec((tm, d // 2), row)]
    shapes = [jax.ShapeDtypeStruct((t, d), F32), jax.ShapeDtypeStruct((t, d), BF16),
              jax.ShapeDtypeStruct((t, d // 2), jnp.int32)]
    return specs, shapes


def _proj_ln_kernel(a_ref, w_ref, x_ref, g_ref, b_ref, of_ref, ob_ref, op_ref):
    z = ALPHA * x_ref[...] + _dot(a_ref[...], w_ref[...])
    _store_ln_outputs(_layer_norm(z, g_ref[...], b_ref[...]), of_ref, ob_ref, op_ref)


def _proj_ln(a, w, x, g, b, tm=256):
    t, k = a.shape
    d = w.shape[1]
    row = lambda i: (i, 0)
    const = lambda i: (0, 0)
    out_specs, out_shape = _ln_out_specs(t, d, tm)
    return pl.pallas_call(
        _proj_ln_kernel,
        grid=(t // tm,),
        in_specs=[pl.BlockSpec((tm, k), row), pl.BlockSpec((k, d), const),
                  pl.BlockSpec((tm, d), row), pl.BlockSpec((1, d), const),
                  pl.BlockSpec((1, d), const)],
        out_specs=out_specs,
        out_shape=out_shape,
        compiler_params=_params("arbitrary"),
        name="proj_ln",
    )(a, w, x, g, b)


def _pool_kernel(x_ref, win_ref, wg_ref, sc_ref, o_ref, carry_ref, *, tm, tiles_per_seq):
    i = pl.program_id(0)
    seq_tile = i % tiles_per_seq

    @pl.when(seq_tile == 0)
    def _():
        carry_ref[...] = jnp.zeros_like(carry_ref)

    xb = x_ref[...]
    pos = lax.broadcasted_iota(jnp.int32, (tm, 1), 0) + seq_tile * tm
    gd = POOL_GROUP_DIM
    for g, w in enumerate(POOL_WINDOWS):
        cols = slice(g * gd, (g + 1) * gd)
        u = _dot(xb, win_ref[:, cols])
        s = jnp.concatenate([carry_ref[g], u], axis=0)
        sh = 1
        while sh < w:
            s = s + pltpu.roll(s, sh, axis=0)
            sh *= 2
        cnt = jnp.minimum(pos + 1, w).astype(F32)
        pooled = s[POOL_HALO:, :] / cnt - u
        carry_ref[g] = u[tm - POOL_HALO:, :]
        mixed = _dot(pooled.astype(BF16), wg_ref[g]) * sc_ref[:, cols]
        o_ref[:, cols] = mixed.astype(BF16)


def _pool_mix(xb, w_in, w_group, scale, seq_len, tm=256):
    t, d = xb.shape
    gd = POOL_GROUP_DIM
    ng = len(POOL_WINDOWS)
    return pl.pallas_call(
        functools.partial(_pool_kernel, tm=tm, tiles_per_seq=seq_len // tm),
        grid=(t // tm,),
        in_specs=[pl.BlockSpec((tm, d), lambda i: (i, 0)),
                  pl.BlockSpec((d, d), lambda i: (0, 0)),
                  pl.BlockSpec((ng, gd, gd), lambda i: (0, 0, 0)),
                  pl.BlockSpec((1, d), lambda i: (0, 0))],
        out_specs=pl.BlockSpec((tm, d), lambda i: (i, 0)),
        out_shape=jax.ShapeDtypeStruct((t, d), BF16),
        scratch_shapes=[pltpu.VMEM((ng, POOL_HALO, gd), F32)],
        compiler_params=_params("arbitrary"),
        name="pool_mix",
    )(xb, w_in, w_group, scale)


def _conv_kernel(x_ref, wb_ref, wc_ref, wh_ref, cw_ref, o_ref, carry_ref, *, tm, tiles_per_seq):
    i = pl.program_id(1)

    @pl.when(i % tiles_per_seq == 0)
    def _():
        carry_ref[...] = jnp.zeros_like(carry_ref)

    xb = x_ref[...]
    b_gate = _dot(xb, wb_ref[...])
    u = _dot(xb, wc_ref[...]) * _dot(xb, wh_ref[...])
    ext = jnp.concatenate([carry_ref[...], u], axis=0)
    y = cw_ref[CONV_WIDTH - 1:CONV_WIDTH, :] * u
    for back in range(1, CONV_WIDTH):
        tap = CONV_WIDTH - 1 - back
        y = y + cw_ref[tap:tap + 1, :] * pltpu.roll(ext, back, axis=0)[CONV_HALO:, :]
    carry_ref[...] = u[tm - CONV_HALO:, :]
    o_ref[...] = (b_gate * y).astype(BF16)


def _conv_mix(xb, w_in, conv_w, seq_len, tm=512, tn=512):
    t, d = xb.shape
    nj = d // tn
    return pl.pallas_call(
        functools.partial(_conv_kernel, tm=tm, tiles_per_seq=seq_len // tm),
        grid=(nj, t // tm),
        in_specs=[pl.BlockSpec((tm, d), lambda j, i: (i, 0)),
                  pl.BlockSpec((d, tn), lambda j, i: (0, j)),
                  pl.BlockSpec((d, tn), lambda j, i: (0, nj + j)),
                  pl.BlockSpec((d, tn), lambda j, i: (0, 2 * nj + j)),
                  pl.BlockSpec((CONV_WIDTH, tn), lambda j, i: (0, j))],
        out_specs=pl.BlockSpec((tm, tn), lambda j, i: (i, j)),
        out_shape=jax.ShapeDtypeStruct((t, d), BF16),
        scratch_shapes=[pltpu.VMEM((CONV_HALO, tn), F32)],
        compiler_params=_params("parallel", "arbitrary"),
        name="conv_mix",
    )(xb, w_in, w_in, w_in, conv_w)


def _t5_bucket(dist):
    max_exact = N_BUCKETS // 2
    is_small = dist < max_exact
    distf = jnp.maximum(dist, 1).astype(F32)
    large = max_exact + (jnp.log(distf / max_exact) / math.log(MAX_DISTANCE / max_exact)
                         * (N_BUCKETS - max_exact)).astype(jnp.int32)
    large = jnp.minimum(large, N_BUCKETS - 1)
    return jnp.where(is_small, dist, large)


def _attn_bias(bias_tab, window, dil, has_prev):
    steps = window // dil
    q_loc = jnp.arange(DIL_BLOCK)[:, None]
    k_loc = jnp.arange(2 * DIL_BLOCK)[None, :]
    rel = q_loc + DIL_BLOCK - k_loc
    band = (rel >= 0) & (rel <= steps)
    bucket = _t5_bucket(jnp.maximum(rel, 0) * dil)
    tab = bias_tab.astype(F32).T[:, :, None, None]
    bias = sum(jnp.where(bucket == b, tab[:, b], 0.0) for b in range(N_BUCKETS))
    bias = jnp.where(band[None], bias, NEG)
    return bias if has_prev else bias[:, :, DIL_BLOCK:]


def _attn_kernel(*refs, has_prev):
    if has_prev:
        q_ref, kc_ref, kp_ref, vc_ref, vp_ref, bias_ref, o_ref, l_ref = refs
    else:
        q_ref, kc_ref, vc_ref, bias_ref, o_ref, l_ref = refs
    n = pl.program_id(1)
    q = q_ref[0, 0]
    kc = kc_ref[0, 0]
    vc = vc_ref[0, 0]
    if has_prev:
        kp = kp_ref[0, 0]
        vp = vp_ref[0, 0]
        col = lax.broadcasted_iota(jnp.int32, (DIL_BLOCK, 2 * DIL_BLOCK), 1)
        keep = (col >= DIL_BLOCK) | (n > 0)
    outs, lses = [], []
    for h in range(DIL_HEADS):
        sl = slice(h * HEAD_DIM, (h + 1) * HEAD_DIM)
        if has_prev:
            kh = jnp.concatenate([kp[:, sl], kc[:, sl]], axis=0)
            vh = jnp.concatenate([vp[:, sl], vc[:, sl]], axis=0)
        else:
            kh, vh = kc[:, sl], vc[:, sl]
        s = lax.dot_general(q[:, sl], kh, _NT, preferred_element_type=F32) * (HEAD_DIM ** -0.5) + bias_ref[h]
        if has_prev:
            s = jnp.where(keep, s, NEG)
        m = jnp.max(s, axis=-1, keepdims=True)
        p = jnp.exp(s - m)
        l = jnp.sum(p, axis=-1, keepdims=True)
        outs.append(_dot(p.astype(BF16), vh) / l)
        lses.append(jnp.broadcast_to(m + jnp.log(l), (DIL_BLOCK, HEAD_DIM)))
    o_ref[0, 0] = jnp.concatenate(outs, axis=1)
    l_ref[0, 0] = jnp.concatenate(lses, axis=1)


def _subseq_tile(dil, tm):
    span = DIL_BLOCK * dil
    if tm >= span:
        block = (tm // span, dil, DIL_BLOCK)
        imap = lambda i: (i, 0, 0, 0)
    else:
        per_span = span // tm
        block = (1, dil, tm // dil)
        imap = lambda i: (i // per_span, 0, i % per_span, 0)
    runs = [(sp, r, sp * span + r) for sp in range(block[0]) for r in range(dil)]
    return block, imap, runs


def _qkv_kernel(x_ref, w_ref, o_ref, *, dil):
    x = x_ref[...]
    if dil > 1:
        tm, n = x.shape[0], o_ref.shape[2]
        out_row = lax.broadcasted_iota(jnp.int32, (tm, tm), 0)
        in_row = lax.broadcasted_iota(jnp.int32, (tm, tm), 1)
        pick = in_row == (out_row % n) * dil + out_row // n
        x = _dot(pick.astype(BF16), x).astype(BF16)
    o_ref[...] = _dot(x, w_ref[...]).astype(o_ref.dtype).reshape(o_ref.shape)


def _qkv_proj(xb, w_qkv, g, tm=512):
    t, d = xb.shape
    _, dil = DIL_CONFIGS[g]
    c = 3 * DIL_WIDTH
    block, imap, _ = _subseq_tile(dil, tm)
    assert dil == 1 or block[0] == 1, "the in-kernel row reorder handles one span (or part of one) per tile"
    return pl.pallas_call(
        functools.partial(_qkv_kernel, dil=dil),
        grid=(t // tm,),
        in_specs=[pl.BlockSpec((tm, d), lambda i: (i, 0)),
                  pl.BlockSpec((d, c), lambda i: (0, g))],
        out_specs=pl.BlockSpec(block + (c,), imap),
        out_shape=jax.ShapeDtypeStruct((t // (DIL_BLOCK * dil), dil, DIL_BLOCK, c), BF16),
        compiler_params=_params("arbitrary"),
        name=f"qkv_proj_g{g}",
    )(xb, w_qkv)


def _dilated_group(qkv, bias_tab, g, batch, seq_len):
    window, dil = DIL_CONFIGS[g]
    nb = seq_len // (DIL_BLOCK * dil)
    has_prev = nb > 1
    blk = (1, 1, DIL_BLOCK, DIL_WIDTH)

    def cur(kind):
        return pl.BlockSpec(blk, lambda b, n, r: (b * nb + n, r, 0, kind))

    def prev(kind):
        return pl.BlockSpec(blk, lambda b, n, r: (b * nb + jnp.maximum(n - 1, 0), r, 0, kind))

    bias = _attn_bias(bias_tab, window, dil, has_prev)
    bias_spec = pl.BlockSpec(bias.shape, lambda b, n, r: (0, 0, 0))
    if has_prev:
        in_specs = [cur(0), cur(1), prev(1), cur(2), prev(2), bias_spec]
        args = (qkv,) * 5 + (bias,)
    else:
        in_specs = [cur(0), cur(1), cur(2), bias_spec]
        args = (qkv,) * 3 + (bias,)
    out_spec = pl.BlockSpec(blk, lambda b, n, r: (b * nb + n, r, 0, 0))
    out_sds = jax.ShapeDtypeStruct(qkv.shape[:3] + (DIL_WIDTH,), F32)
    return pl.pallas_call(
        functools.partial(_attn_kernel, has_prev=has_prev),
        grid=(batch, nb, dil),
        in_specs=in_specs,
        out_specs=[out_spec, out_spec],
        out_shape=[out_sds, out_sds],
        compiler_params=_params("parallel", "arbitrary", "arbitrary"),
        name=f"dilated_attn_g{g}",
    )(*args)


def _attn_out_kernel(*refs, tm, tiles):
    n_g = len(tiles)
    o_refs, l_refs = refs[:n_g], refs[n_g:2 * n_g]
    w_ref, x_ref, g_ref, b_ref, of_ref, ob_ref, op_ref = refs[2 * n_g:2 * n_g + 7]
    scratch = list(refs[2 * n_g + 7:])

    def token_order(ref, dil, runs):
        if dil == 1:
            return ref[...].reshape(tm, ref.shape[-1])
        buf = scratch.pop(0)
        n = ref.shape[2]
        for sp, r, first in runs:
            rows = ref[sp, r]
            for c in range(buf.shape[0]):
                buf[c, pl.ds(first, n, stride=dil), :] = rows[:, c * LANES:(c + 1) * LANES]
        return jnp.concatenate([buf[c] for c in range(buf.shape[0])], axis=1)

    os_ = [token_order(ref, dil, runs) for ref, (dil, runs) in zip(o_refs, tiles)]
    ls_ = [token_order(ref, dil, runs) for ref, (dil, runs) in zip(l_refs, tiles)]
    m = functools.reduce(jnp.maximum, ls_)
    es = [jnp.exp(a - m) for a in ls_]
    o = sum(e * ov for e, ov in zip(es, os_)) / sum(es)
    z = ALPHA * x_ref[...] + _dot(o.astype(BF16), w_ref[...])
    _store_ln_outputs(_layer_norm(z, g_ref[...], b_ref[...]), of_ref, ob_ref, op_ref)


def _attn_out(os_, ls_, w, x, g, b, tm=256):
    t, d = x.shape
    k = w.shape[0]
    row = lambda i: (i, 0)
    const = lambda i: (0, 0)
    specs, tiles = [], []
    for _, dil in DIL_CONFIGS:
        block, imap, runs = _subseq_tile(dil, tm)
        specs.append(pl.BlockSpec(block + (k,), imap))
        tiles.append((dil, runs))
    n_scratch = 2 * sum(1 for dil, _ in tiles if dil > 1)
    out_specs, out_shape = _ln_out_specs(t, d, tm)
    return pl.pallas_call(
        functools.partial(_attn_out_kernel, tm=tm, tiles=tiles),
        grid=(t // tm,),
        in_specs=specs * 2 + [pl.BlockSpec((k, d), const), pl.BlockSpec((tm, d), row),
                              pl.BlockSpec((1, d), const), pl.BlockSpec((1, d), const)],
        out_specs=out_specs,
        out_shape=out_shape,
        scratch_shapes=[pltpu.VMEM((k // LANES, tm, LANES), F32)] * n_scratch,
        compiler_params=_params("arbitrary"),
        name="attn_out_ln",
    )(*os_, *ls_, w, x, g, b)


def _router_kernel(x_ref, wt_ref, b_ref, eidx_ref, gate_ref, pos_ref, cnt_ref, base_ref, *, tm):
    @pl.when(pl.program_id(0) == 0)
    def _():
        base_ref[...] = jnp.zeros_like(base_ref)

    x = x_ref[...]
    w = wt_ref[...]
    xh = x.astype(BF16)
    xl = (x - xh.astype(F32)).astype(BF16)
    wh = w.astype(BF16)
    wl = (w - wh.astype(F32)).astype(BF16)
    dg = functools.partial(lax.dot_general, dimension_numbers=_NT, preferred_element_type=F32)
    logits = dg(wh, xh) + (dg(wh, xl) + dg(wl, xh))
    scores = 1.0 / (1.0 + jnp.exp(-logits))
    choice = scores + b_ref[...]

    iota_g = lax.broadcasted_iota(jnp.int32, (GROUP_SIZE, tm), 0)
    gs = []
    for gi in range(N_EXPERT_GROUPS):
        cg = choice[gi * GROUP_SIZE:(gi + 1) * GROUP_SIZE, :]
        t1 = jnp.max(cg, axis=0, keepdims=True)
        i1 = jnp.min(jnp.where(cg == t1, iota_g, GROUP_SIZE), axis=0, keepdims=True)
        t2 = jnp.max(jnp.where(iota_g == i1, -jnp.inf, cg), axis=0, keepdims=True)
        gs.append(t1 + t2)
    masked = []
    for gi in range(N_EXPERT_GROUPS):
        rank = jnp.zeros((1, tm), jnp.int32)
        for gj in range(N_EXPERT_GROUPS):
            if gj != gi:
                beats = (gs[gj] >= gs[gi]) if gj < gi else (gs[gj] > gs[gi])
                rank = rank + beats.astype(jnp.int32)
        masked.append(jnp.where(rank < TOPK_GROUPS, choice[gi * GROUP_SIZE:(gi + 1) * GROUP_SIZE, :], -jnp.inf))
    c = jnp.concatenate(masked, axis=0)

    iota_e = lax.broadcasted_iota(jnp.int32, (N_EXPERTS, tm), 0)
    picks, gates = [], []
    sel = jnp.zeros((N_EXPERTS, tm), F32)
    for _ in range(TOP_K):
        m = jnp.max(c, axis=0, keepdims=True)
        idx = jnp.min(jnp.where(c == m, iota_e, N_EXPERTS), axis=0, keepdims=True)
        hit = iota_e == idx
        picks.append(hit)
        gates.append(jnp.sum(jnp.where(hit, scores, 0.0), axis=0, keepdims=True))
        sel = jnp.where(hit, 1.0, sel)
        c = jnp.where(hit, -jnp.inf, c)
    gsum = gates[0]
    for gk in gates[1:]:
        gsum = gsum + gk

    before = (lax.broadcasted_iota(jnp.int32, (tm, tm), 0) < lax.broadcasted_iota(jnp.int32, (tm, tm), 1))
    rank_in_tile = _dot(sel.astype(BF16), before.astype(BF16))
    base = base_ref[...]
    posfull = base[:, 0:1] + rank_in_tile
    efull = iota_e.astype(F32)
    for k in range(TOP_K):
        hit = picks[k]
        eidx_ref[k:k + 1, :] = jnp.sum(jnp.where(hit, efull, 0.0), axis=0, keepdims=True).astype(jnp.int32)
        pos_ref[k:k + 1, :] = jnp.sum(jnp.where(hit, posfull, 0.0), axis=0, keepdims=True).astype(jnp.int32)
        gate_ref[k:k + 1, :] = gates[k] / gsum * ROUTED_SCALE
    new_base = base + jnp.sum(sel, axis=1, keepdims=True)
    base_ref[...] = new_base
    cnt_ref[...] = new_base


def _route(x, router_wt, router_b, tm=512):
    t, d = x.shape
    tok = pl.BlockSpec((TOP_K, tm), lambda i: (0, i))
    cnt = pl.BlockSpec((N_EXPERTS, LANES), lambda i: (0, 0))
    return pl.pallas_call(
        functools.partial(_router_kernel, tm=tm),
        grid=(t // tm,),
        in_specs=[pl.BlockSpec((tm, d), lambda i: (i, 0)),
                  pl.BlockSpec((N_EXPERTS, d), lambda i: (0, 0)),
                  pl.BlockSpec((N_EXPERTS, 1), lambda i: (0, 0))],
        out_specs=[tok, tok, tok, cnt],
        out_shape=[jax.ShapeDtypeStruct((TOP_K, t), jnp.int32), jax.ShapeDtypeStruct((TOP_K, t), F32),
                   jax.ShapeDtypeStruct((TOP_K, t), jnp.int32), jax.ShapeDtypeStruct((N_EXPERTS, LANES), F32)],
        scratch_shapes=[pltpu.VMEM((N_EXPERTS, LANES), F32)],
        compiler_params=_params("arbitrary"),
        name="router",
    )(x, router_wt, router_b)


def _row(ref, r):
    return ref.at[pl.ds(r, 1)]


def _dest_kernel(pstart_ref, eidx_ref, pos_ref, dest_ref):
    e = eidx_ref[...]
    acc = pos_ref[...]
    for k in range(N_EXPERTS):
        acc = acc + jnp.where(e == k, pstart_ref[k], 0)
    dest_ref[...] = acc


def _dest_rows(pstart, eidx, pos):
    full = pl.BlockSpec(eidx.shape, lambda i, ps: (0, 0))
    return pl.pallas_call(
        _dest_kernel,
        grid_spec=pltpu.PrefetchScalarGridSpec(num_scalar_prefetch=1, grid=(1,), in_specs=[full, full],
                                               out_specs=full),
        out_shape=jax.ShapeDtypeStruct(eidx.shape, jnp.int32),
        compiler_params=_params("arbitrary"),
        name="dest_rows",
    )(pstart, eidx, pos)


def _dispatch_kernel(pad_lo_ref, pad_hi_ref, dest_ref, x_ref, gate_ref, xs_hbm, src_ref, zero_ref, sem, pad_sem,
                     *, tm, n_blk, n_tok):
    i = pl.program_id(0)

    @pl.when(i == 0)
    def _():
        zero_ref[...] = jnp.zeros_like(zero_ref)

        def fill(lo, hi, copy):
            def start(r, c):
                copy(r).start()
                return c

            def wait(r, c):
                copy(r).wait()
                return c

            lax.fori_loop(lo, hi, start, 0)
            lax.fori_loop(lo, hi, wait, 0)

        def per_expert(e, carry):
            fill(pad_lo_ref[e], pad_hi_ref[e],
                 lambda r: pltpu.make_async_copy(_row(zero_ref, 0), _row(xs_hbm, r), pad_sem))
            return carry

        lax.fori_loop(0, N_EXPERTS, per_expert, 0)
        fill(pad_hi_ref[N_EXPERTS - 1] // ROW_BLOCK, n_blk,
             lambda blk: pltpu.make_async_copy(
                 zero_ref, xs_hbm.at[pl.ds(pl.multiple_of(blk * ROW_BLOCK, ROW_BLOCK), ROW_BLOCK)], pad_sem))

    x = x_ref[...]
    gate_bits = lax.bitcast_convert_type(gate_ref[...], jnp.int32)
    lane = lax.broadcasted_iota(jnp.int32, (tm, LANES), 1)
    token = lax.broadcasted_iota(jnp.int32, (tm, LANES), 0) + i * tm
    half = x.shape[1]
    for k in range(TOP_K):
        src_ref[k, :, 0:half] = x
        src_ref[k, :, half:] = jnp.where(lane == 0, token + k * n_tok,
                                         jnp.where(lane == 1, gate_bits[:, k:k + 1], 0))

    def per_token(j, carry):
        for k in range(TOP_K):
            pltpu.make_async_copy(src_ref.at[k, pl.ds(j, 1)], _row(xs_hbm, dest_ref[0, 0, j * TOP_K + k]),
                                  sem).start()
        return carry

    lax.fori_loop(0, tm, per_token, 0)
    for k in range(TOP_K):
        pltpu.make_async_copy(src_ref.at[k], xs_hbm.at[pl.ds(0, tm)], sem).wait()


def _dispatch(xp, dest_tok, gate_tok, pad_lo, pad_hi, n_rows, tm=512):
    t, half = xp.shape
    width = half + LANES
    n_tiles = t // tm
    dest3 = dest_tok.reshape(n_tiles, 1, tm * TOP_K)
    grid_spec = pltpu.PrefetchScalarGridSpec(
        num_scalar_prefetch=2,
        grid=(n_tiles,),
        in_specs=[pl.BlockSpec((1, 1, tm * TOP_K), lambda i, lo, hi: (i, 0, 0), memory_space=pltpu.SMEM),
                  pl.BlockSpec((tm, half), lambda i, lo, hi: (i, 0)),
                  pl.BlockSpec((tm, TOP_K), lambda i, lo, hi: (i, 0))],
        out_specs=pl.BlockSpec(memory_space=pl.ANY),
        scratch_shapes=[pltpu.VMEM((TOP_K, tm, width), jnp.int32),
                        pltpu.VMEM((ROW_BLOCK, width), jnp.int32),
                        pltpu.SemaphoreType.DMA(()), pltpu.SemaphoreType.DMA(())],
    )
    return pl.pallas_call(
        functools.partial(_dispatch_kernel, tm=tm, n_blk=n_rows // ROW_BLOCK, n_tok=t),
        grid_spec=grid_spec,
        out_shape=jax.ShapeDtypeStruct((n_rows, width), jnp.int32),
        compiler_params=_params("arbitrary"),
        name="dispatch",
    )(pad_lo, pad_hi, dest3, xp, gate_tok)


def _expert_kernel(blk_e_ref, nused_ref, nvalid_ref, xs_ref, wgu_ref, wd_ref, out_hbm,
                   wgu_bf, wd_bf, ybuf, idv, ids_smem, sem, id_sem, *, n_blk, n_slots):
    i = pl.program_id(0)
    nused = nused_ref[0]
    s = i % 2
    half = ybuf.shape[2]

    def rows_copy(parity):
        return pltpu.make_async_copy(ybuf.at[parity], out_hbm.at[pl.ds(0, ROW_BLOCK)], sem.at[parity])

    def spill_init(parity):
        return pltpu.make_async_copy(ybuf.at[parity], out_hbm.at[pl.ds(n_slots + parity * ROW_BLOCK, ROW_BLOCK)],
                                     sem.at[parity])

    def compute(par, retire_first, scatter_par=None):
        @pl.when((i == 0) | (blk_e_ref[i] != blk_e_ref[jnp.maximum(i - 1, 0)]))
        def _():
            wgu_bf[...] = wgu_ref[0, 0].astype(BF16)
            wd_bf[...] = wd_ref[0, 0].astype(BF16)

        if scatter_par is not None:
            ids_ready(scatter_par)

        meta = xs_ref[:, half:]
        row = lax.broadcasted_iota(jnp.int32, meta.shape, 0)
        dst = jnp.where(row < nvalid_ref[i], meta, n_slots + par * ROW_BLOCK + row)
        idv[par] = jnp.transpose(dst)[0:idv.shape[1], :]
        pltpu.make_async_copy(idv.at[par], ids_smem.at[par], id_sem.at[par]).start()

        gate = lax.bitcast_convert_type(meta[:, 1:2], F32)
        xs = _unpack_rows(xs_ref[:, 0:half])
        d = xs.shape[1]
        step = d // SCATTER_CHUNKS
        hgu = None
        for c in range(SCATTER_CHUNKS):
            part = _dot(xs[:, c * step:(c + 1) * step], wgu_bf[c * step:(c + 1) * step, :])
            hgu = part if hgu is None else hgu + part
            if scatter_par is not None:
                scatter(scatter_par, c)
        h = _silu(hgu[:, :EXPERT_DIM]) * hgu[:, EXPERT_DIM:]
        y = _pack_rows(_dot(h.astype(BF16), wd_bf[...]) * gate)
        if retire_first:
            rows_copy(par).wait()
        ybuf[par] = y

    def ids_ready(par):
        pltpu.make_async_copy(idv.at[par], ids_smem.at[par], id_sem.at[par]).wait()

    def scatter(par, chunk):
        n = ROW_BLOCK // SCATTER_CHUNKS
        for r in range(chunk * n, (chunk + 1) * n):
            pltpu.make_async_copy(ybuf.at[par, pl.ds(r, 1)], _row(out_hbm, ids_smem[par, 0, r]),
                                  sem.at[par]).start()

    @pl.when(i == 0)
    def _():
        ybuf[...] = jnp.zeros_like(ybuf)
        spill_init(0).start()
        spill_init(0).wait()
        spill_init(1).start()
        compute(0, retire_first=False)

    for par in range(2):
        @pl.when((i >= 1) & (i < nused) & (s == par))
        def _(par=par):
            compute(par, retire_first=True, scatter_par=1 - par)

        @pl.when((i >= 1) & (i == nused) & (s == par))
        def _(par=par):
            ids_ready(1 - par)
            for c in range(SCATTER_CHUNKS):
                scatter(1 - par, c)

    @pl.when((i >= nused) & (i >= 2) & (i - 2 < nused))
    def _():
        rows_copy(s).wait()

    @pl.when((i == n_blk - 1) & (i - 1 < nused))
    def _():
        rows_copy(1 - s).wait()


def _experts(xs, blk_e, nused, nvalid, w_gu, w_down, layer, n_slots):
    n_rows, width = xs.shape
    half = width - LANES
    n_blk = n_rows // ROW_BLOCK
    d, f2 = w_gu.shape[2], w_gu.shape[3]
    blk = lambda i, be, nu, nv: (jnp.minimum(i, nu[0] - 1), 0)
    wblk = lambda i, be, nu, nv: (layer, be[jnp.minimum(i, nu[0] - 1)], 0, 0)
    grid_spec = pltpu.PrefetchScalarGridSpec(
        num_scalar_prefetch=3,
        grid=(n_blk,),
        in_specs=[pl.BlockSpec((ROW_BLOCK, width), blk),
                  pl.BlockSpec((1, 1, d, f2), wblk),
                  pl.BlockSpec((1, 1, f2 // 2, d), wblk)],
        out_specs=pl.BlockSpec(memory_space=pl.ANY),
        scratch_shapes=[pltpu.VMEM((d, f2), BF16), pltpu.VMEM((f2 // 2, d), BF16),
                        pltpu.VMEM((2, ROW_BLOCK, half), jnp.int32),
                        pltpu.VMEM((2, 8, ROW_BLOCK), jnp.int32),
                        pltpu.SMEM((2, 8, ROW_BLOCK), jnp.int32),
                        pltpu.SemaphoreType.DMA((2,)), pltpu.SemaphoreType.DMA((2,))],
    )
    return pl.pallas_call(
        functools.partial(_expert_kernel, n_blk=n_blk, n_slots=n_slots),
        grid_spec=grid_spec,
        out_shape=jax.ShapeDtypeStruct((n_slots + 2 * ROW_BLOCK, half), jnp.int32),
        compiler_params=_params("arbitrary"),
        name="experts",
    )(blk_e, nused, nvalid, xs, w_gu, w_down)


def _combine_kernel(*refs):
    y_refs = refs[:TOP_K]
    xf_ref, xb_ref, sgu_ref, sd_ref, g_ref, b_ref, of_ref, ob_ref = refs[TOP_K:]
    lo = hi = None
    for y_ref in y_refs:
        p = y_ref[...]
        lo_k = lax.bitcast_convert_type(lax.shift_left(p, 16), F32)
        hi_k = lax.bitcast_convert_type(p & HIGH_HALF, F32)
        lo = lo_k if lo is None else lo + lo_k
        hi = hi_k if hi is None else hi + hi_k
    routed = jnp.concatenate([lo, hi], axis=1)

    sgu = _dot(xb_ref[...], sgu_ref[...])
    sh = _silu(sgu[:, :SHARED_DIM]) * sgu[:, SHARED_DIM:]
    shared = _dot(sh.astype(BF16), sd_ref[...])
    z = ALPHA * xf_ref[...] + (routed + shared)
    y = _layer_norm(z, g_ref[...], b_ref[...])
    of_ref[...] = y
    ob_ref[...] = y.astype(BF16)


def _combine(y, xf, xb, sw_gu, sw_down, g, b, tm=256):
    t, d = xf.shape
    n_tiles = t // tm
    row = lambda i: (i, 0)
    const = lambda i: (0, 0)
    y_specs = [pl.BlockSpec((tm, d // 2), functools.partial(lambda i, k: (k * n_tiles + i, 0), k=k))
               for k in range(TOP_K)]
    return pl.pallas_call(
        _combine_kernel,
        grid=(n_tiles,),
        in_specs=y_specs + [pl.BlockSpec((tm, d), row), pl.BlockSpec((tm, d), row),
                            pl.BlockSpec(sw_gu.shape, const), pl.BlockSpec(sw_down.shape, const),
                            pl.BlockSpec((1, d), const), pl.BlockSpec((1, d), const)],
        out_specs=[pl.BlockSpec((tm, d), row), pl.BlockSpec((tm, d), row)],
        out_shape=[jax.ShapeDtypeStruct((t, d), F32), jax.ShapeDtypeStruct((t, d), BF16)],
        compiler_params=_params("arbitrary"),
        name="combine_ln",
    )(*([y] * TOP_K), xf, xb, sw_gu, sw_down, g, b)


def _moe_layer(xf, xb, xp, router_w, router_b, w_gu, w_down, layer, sw_gu, sw_down, g, b):
    t, d = xf.shape
    eidx, gate, pos, cnt = _route(xf, router_w.T, router_b.reshape(N_EXPERTS, 1))

    sizes = cnt[:, 0].astype(jnp.int32)
    padded = (sizes + ROW_BLOCK - 1) // ROW_BLOCK * ROW_BLOCK
    pend = jnp.cumsum(padded)
    pstart = pend - padded
    tk = t * TOP_K
    n_rows = -(-tk // ROW_BLOCK) * ROW_BLOCK + N_EXPERTS * ROW_BLOCK
    blk_start = jnp.arange(n_rows // ROW_BLOCK, dtype=jnp.int32) * ROW_BLOCK
    blk_e = jnp.minimum(jnp.sum(pend[None, :] <= blk_start[:, None], axis=1), N_EXPERTS - 1).astype(jnp.int32)
    nused = (pend[-1:] // ROW_BLOCK).astype(jnp.int32)

    real_end = pstart + sizes
    nvalid = jnp.clip(real_end[blk_e] - blk_start, 0, ROW_BLOCK).astype(jnp.int32)

    dest_tok = _dest_rows(pstart, eidx, pos).T
    xs = _dispatch(xp, dest_tok, gate.T, real_end, pend, n_rows)
    y = _experts(xs, blk_e, nused, nvalid, w_gu, w_down, layer, tk)
    return _combine(y, xf, xb, sw_gu, sw_down, g, b)


def kernel(x, rel_bias, pool_w_in, pool_w_group, pool_scale, pool_w_out, dil_w_qkv, dil_w_out,
           conv_w_in, conv_w, conv_w_out, ln_gain, ln_bias, router_w, router_bias,
           expert_w_gu, expert_w_down, shared_w_gu, shared_w_down):
    bn, s, d = x.shape
    t = bn * s
    xf = x.reshape(t, d)
    xb = xf.astype(BF16)
    bf = lambda a: a.astype(BF16)
    ia = ib = ic = 0
    for i in range(DEPTH):
        kind = i % 3
        g1, b1 = ln_gain[i, 0].reshape(1, d), ln_bias[i, 0].reshape(1, d)
        g2, b2 = ln_gain[i, 1].reshape(1, d), ln_bias[i, 1].reshape(1, d)
        if kind == 0:
            mixed = _pool_mix(xb, bf(pool_w_in[ia]), bf(pool_w_group[ia]), pool_scale[ia].reshape(1, d), s)
            xf, xb, xp = _proj_ln(mixed, bf(pool_w_out[ia]), xf, g1, b1)
            ia += 1
        elif kind == 1:
            w_qkv = bf(dil_w_qkv[ib])
            os_, ls_ = [], []
            for g in range(len(DIL_CONFIGS)):
                qkv = _qkv_proj(xb, w_qkv, g)
                o, l = _dilated_group(qkv, rel_bias[:, g * DIL_HEADS:(g + 1) * DIL_HEADS], g, bn, s)
                os_.append(o)
                ls_.append(l)
            xf, xb, xp = _attn_out(os_, ls_, bf(dil_w_out[ib]), xf, g1, b1)
            ib += 1
        else:
            v = _conv_mix(xb, bf(conv_w_in[ic]), conv_w[ic], s)
            xf, xb, xp = _proj_ln(v, bf(conv_w_out[ic]), xf, g1, b1)
            ic += 1
        xf, xb = _moe_layer(xf, xb, xp, router_w[i], router_bias[i], expert_w_gu, expert_w_down, i,
                            bf(shared_w_gu[i]), bf(shared_w_down[i]), g2, b2)
    return xf.reshape(bn, s, d)
```

```python
import functools
import math

import jax
import jax.numpy as jnp
from jax import lax
from jax.experimental import pallas as pl
from jax.experimental.pallas import tpu as pltpu

F32 = jnp.float32
BF16 = jnp.bfloat16

DEPTH = 4
D_MODEL = 2048
POOL_WINDOWS = (2, 4, 8, 16)
POOL_GROUP_DIM = D_MODEL // len(POOL_WINDOWS)
POOL_HALO = 16
DIL_CONFIGS = ((128, 1), (512, 4), (2048, 16))
HEAD_DIM = 64
DIL_HEADS = 8
DIL_WIDTH = DIL_HEADS * HEAD_DIM
DIL_BLOCK = 128
N_BUCKETS = 32
MAX_DISTANCE = 2048
CONV_WIDTH = 3
CONV_HALO = 8
N_EXPERTS = 64
TOP_K = 8
N_EXPERT_GROUPS = 8
GROUP_SIZE = N_EXPERTS // N_EXPERT_GROUPS
TOPK_GROUPS = 4
EXPERT_DIM = 128
SHARED_DIM = 256
ROUTED_SCALE = 2.5
ROW_BLOCK = 256
ALPHA = (2.0 * DEPTH) ** 0.25
LN_EPS = 1e-5
NEG = -1e30

LANES = 128
HIGH_HALF = -65536
UP_CHUNKS = 8
UP_SHARE = 160
VMEM_LIMIT = 52 * 1024 * 1024

_NT = (((1,), (1,)), ((), ()))


def _dot(a, b):
    return jnp.dot(a, b, preferred_element_type=F32)


def _params(*sem):
    return pltpu.CompilerParams(dimension_semantics=sem, vmem_limit_bytes=VMEM_LIMIT)


def _layer_norm(z, g, b):
    mu = jnp.mean(z, axis=-1, keepdims=True)
    zc = z - mu
    var = jnp.mean(zc * zc, axis=-1, keepdims=True)
    return zc * lax.rsqrt(var + LN_EPS) * g + b


def _silu(v):
    return v / (1.0 + jnp.exp(-v))


def _pack_pair(lo, hi):
    lo = lax.bitcast_convert_type(lo.astype(BF16).astype(F32), jnp.int32)
    hi = lax.bitcast_convert_type(hi.astype(BF16).astype(F32), jnp.int32)
    return (hi & HIGH_HALF) | lax.shift_right_logical(lo, 16)


def _pack_rows(y):
    half = D_MODEL // 2
    return _pack_pair(y[:, :half], y[:, half:])


def _unpack_rows(p):
    lo = lax.bitcast_convert_type(lax.shift_left(p, 16), F32)
    hi = lax.bitcast_convert_type(p & HIGH_HALF, F32)
    return jnp.concatenate([lo, hi], axis=1).astype(BF16)


def _store_ln_outputs(y, of_ref, ob_ref, op_ref):
    of_ref[...] = y
    ob_ref[...] = y.astype(BF16)
    op_ref[...] = _pack_rows(y)


def _ln_out_specs(t, d, tm):
    row = lambda i: (i, 0)
    specs = [pl.BlockSpec((tm, d), row), pl.BlockSpec((tm, d), row), pl.BlockSpec((tm, d // 2), row)]
    shapes = [jax.ShapeDtypeStruct((t, d), F32), jax.ShapeDtypeStruct((t, d), BF16),
              jax.ShapeDtypeStruct((t, d // 2), jnp.int32)]
    return specs, shapes


def _proj_ln_kernel(a_ref, w_ref, x_ref, g_ref, b_ref, of_ref, ob_ref, op_ref):
    z = ALPHA * x_ref[...] + _dot(a_ref[...], w_ref[...])
    _store_ln_outputs(_layer_norm(z, g_ref[...], b_ref[...]), of_ref, ob_ref, op_ref)


def _proj_ln(a, w, x, g, b, tm=256):
    t, k = a.shape
    d = w.shape[1]
    row = lambda i: (i, 0)
    const = lambda i: (0, 0)
    out_specs, out_shape = _ln_out_specs(t, d, tm)
    return pl.pallas_call(
        _proj_ln_kernel,
        grid=(t // tm,),
        in_specs=[pl.BlockSpec((tm, k), row), pl.BlockSpec((k, d), const),
                  pl.BlockSpec((tm, d), row), pl.BlockSpec((1, d), const),
                  pl.BlockSpec((1, d), const)],
        out_specs=out_specs,
        out_shape=out_shape,
        compiler_params=_params("arbitrary"),
        name="proj_ln",
    )(a, w, x, g, b)


def _pool_kernel(x_ref, win_ref, wg_ref, sc_ref, o_ref, carry_ref, *, tm, tiles_per_seq):
    i = pl.program_id(0)
    seq_tile = i % tiles_per_seq

    @pl.when(seq_tile == 0)
    def _():
        carry_ref[...] = jnp.zeros_like(carry_ref)

    xb = x_ref[...]
    pos = lax.broadcasted_iota(jnp.int32, (tm, 1), 0) + seq_tile * tm
    gd = POOL_GROUP_DIM
    for g, w in enumerate(POOL_WINDOWS):
        cols = slice(g * gd, (g + 1) * gd)
        u = _dot(xb, win_ref[:, cols])
        s = jnp.concatenate([carry_ref[g], u], axis=0)
        sh = 1
        while sh < w:
            s = s + pltpu.roll(s, sh, axis=0)
            sh *= 2
        cnt = jnp.minimum(pos + 1, w).astype(F32)
        pooled = s[POOL_HALO:, :] / cnt - u
        carry_ref[g] = u[tm - POOL_HALO:, :]
        mixed = _dot(pooled.astype(BF16), wg_ref[g]) * sc_ref[:, cols]
        o_ref[:, cols] = mixed.astype(BF16)


def _pool_mix(xb, w_in, w_group, scale, seq_len, tm=256):
    t, d = xb.shape
    gd = POOL_GROUP_DIM
    ng = len(POOL_WINDOWS)
    return pl.pallas_call(
        functools.partial(_pool_kernel, tm=tm, tiles_per_seq=seq_len // tm),
        grid=(t // tm,),
        in_specs=[pl.BlockSpec((tm, d), lambda i: (i, 0)),
                  pl.BlockSpec((d, d), lambda i: (0, 0)),
                  pl.BlockSpec((ng, gd, gd), lambda i: (0, 0, 0)),
                  pl.BlockSpec((1, d), lambda i: (0, 0))],
        out_specs=pl.BlockSpec((tm, d), lambda i: (i, 0)),
        out_shape=jax.ShapeDtypeStruct((t, d), BF16),
        scratch_shapes=[pltpu.VMEM((ng, POOL_HALO, gd), F32)],
        compiler_params=_params("arbitrary"),
        name="pool_mix",
    )(xb, w_in, w_group, scale)


def _conv_kernel(x_ref, wb_ref, wc_ref, wh_ref, cw_ref, o_ref, carry_ref, *, tm, tiles_per_seq):
    i = pl.program_id(1)

    @pl.when(i % tiles_per_seq == 0)
    def _():
        carry_ref[...] = jnp.zeros_like(carry_ref)

    xb = x_ref[...]
    b_gate = _dot(xb, wb_ref[...])
    u = _dot(xb, wc_ref[...]) * _dot(xb, wh_ref[...])
    ext = jnp.concatenate([carry_ref[...], u], axis=0)
    y = cw_ref[CONV_WIDTH - 1:CONV_WIDTH, :] * u
    for back in range(1, CONV_WIDTH):
        tap = CONV_WIDTH - 1 - back
        y = y + cw_ref[tap:tap + 1, :] * pltpu.roll(ext, back, axis=0)[CONV_HALO:, :]
    carry_ref[...] = u[tm - CONV_HALO:, :]
    o_ref[...] = (b_gate * y).astype(BF16)


def _conv_mix(xb, w_in, conv_w, seq_len, tm=512, tn=512):
    t, d = xb.shape
    nj = d // tn
    return pl.pallas_call(
        functools.partial(_conv_kernel, tm=tm, tiles_per_seq=seq_len // tm),
        grid=(nj, t // tm),
        in_specs=[pl.BlockSpec((tm, d), lambda j, i: (i, 0)),
                  pl.BlockSpec((d, tn), lambda j, i: (0, j)),
                  pl.BlockSpec((d, tn), lambda j, i: (0, nj + j)),
                  pl.BlockSpec((d, tn), lambda j, i: (0, 2 * nj + j)),
                  pl.BlockSpec((CONV_WIDTH, tn), lambda j, i: (0, j))],
        out_specs=pl.BlockSpec((tm, tn), lambda j, i: (i, j)),
        out_shape=jax.ShapeDtypeStruct((t, d), BF16),
        scratch_shapes=[pltpu.VMEM((CONV_HALO, tn), F32)],
        compiler_params=_params("parallel", "arbitrary"),
        name="conv_mix",
    )(xb, w_in, w_in, w_in, conv_w)


def _t5_bucket(dist):
    max_exact = N_BUCKETS // 2
    is_small = dist < max_exact
    distf = jnp.maximum(dist, 1).astype(F32)
    large = max_exact + (jnp.log(distf / max_exact) / math.log(MAX_DISTANCE / max_exact)
                         * (N_BUCKETS - max_exact)).astype(jnp.int32)
    large = jnp.minimum(large, N_BUCKETS - 1)
    return jnp.where(is_small, dist, large)


def _attn_bias(bias_tab, window, dil, has_prev):
    steps = window // dil
    q_loc = jnp.arange(DIL_BLOCK)[:, None]
    k_loc = jnp.arange(2 * DIL_BLOCK)[None, :]
    rel = q_loc + DIL_BLOCK - k_loc
    band = (rel >= 0) & (rel <= steps)
    bucket = _t5_bucket(jnp.maximum(rel, 0) * dil)
    tab = bias_tab.astype(F32).T[:, :, None, None]
    bias = sum(jnp.where(bucket == b, tab[:, b], 0.0) for b in range(N_BUCKETS))
    bias = jnp.where(band[None], bias, NEG)
    return bias if has_prev else bias[:, :, DIL_BLOCK:]


def _attn_kernel(*refs, has_prev):
    if has_prev:
        q_ref, kc_ref, kp_ref, vc_ref, vp_ref, bias_ref, o_ref, l_ref = refs
    else:
        q_ref, kc_ref, vc_ref, bias_ref, o_ref, l_ref = refs
    n = pl.program_id(1)
    q = q_ref[0, 0]
    kc = kc_ref[0, 0]
    vc = vc_ref[0, 0]
    if has_prev:
        kp = kp_ref[0, 0]
        vp = vp_ref[0, 0]
        col = lax.broadcasted_iota(jnp.int32, (DIL_BLOCK, 2 * DIL_BLOCK), 1)
        keep = (col >= DIL_BLOCK) | (n > 0)
    outs, lses = [], []
    for h in range(DIL_HEADS):
        sl = slice(h * HEAD_DIM, (h + 1) * HEAD_DIM)
        if has_prev:
            kh = jnp.concatenate([kp[:, sl], kc[:, sl]], axis=0)
            vh = jnp.concatenate([vp[:, sl], vc[:, sl]], axis=0)
        else:
            kh, vh = kc[:, sl], vc[:, sl]
        s = lax.dot_general(q[:, sl], kh, _NT, preferred_element_type=F32) * (HEAD_DIM ** -0.5) + bias_ref[h]
        if has_prev:
            s = jnp.where(keep, s, NEG)
        m = jnp.max(s, axis=-1, keepdims=True)
        p = jnp.exp(s - m)
        l = jnp.sum(p, axis=-1, keepdims=True)
        outs.append(_dot(p.astype(BF16), vh) / l)
        lses.append(jnp.broadcast_to(m + jnp.log(l), (DIL_BLOCK, HEAD_DIM)))
    o_ref[0, 0] = jnp.concatenate(outs, axis=1)
    l_ref[0, 0] = jnp.concatenate(lses, axis=1)


def _subseq_tile(dil, tm):
    span = DIL_BLOCK * dil
    if tm >= span:
        block = (tm // span, dil, DIL_BLOCK)
        imap = lambda i: (i, 0, 0, 0)
    else:
        per_span = span // tm
        block = (1, dil, tm // dil)
        imap = lambda i: (i // per_span, 0, i % per_span, 0)
    runs = [(sp, r, sp * span + r) for sp in range(block[0]) for r in range(dil)]
    return block, imap, runs


def _qkv_kernel(x_ref, w_ref, o_ref, *, dil):
    x = x_ref[...]
    if dil > 1:
        tm, n = x.shape[0], o_ref.shape[2]
        out_row = lax.broadcasted_iota(jnp.int32, (tm, tm), 0)
        in_row = lax.broadcasted_iota(jnp.int32, (tm, tm), 1)
        pick = in_row == (out_row % n) * dil + out_row // n
        x = _dot(pick.astype(BF16), x).astype(BF16)
    o_ref[...] = _dot(x, w_ref[...]).astype(o_ref.dtype).reshape(o_ref.shape)


def _qkv_proj(xb, w_qkv, g, tm=512):
    t, d = xb.shape
    _, dil = DIL_CONFIGS[g]
    c = 3 * DIL_WIDTH
    block, imap, _ = _subseq_tile(dil, tm)
    assert dil == 1 or block[0] == 1, "the in-kernel row reorder handles one span (or part of one) per tile"
    return pl.pallas_call(
        functools.partial(_qkv_kernel, dil=dil),
        grid=(t // tm,),
        in_specs=[pl.BlockSpec((tm, d), lambda i: (i, 0)),
                  pl.BlockSpec((d, c), lambda i: (0, g))],
        out_specs=pl.BlockSpec(block + (c,), imap),
        out_shape=jax.ShapeDtypeStruct((t // (DIL_BLOCK * dil), dil, DIL_BLOCK, c), BF16),
        compiler_params=_params("arbitrary"),
        name=f"qkv_proj_g{g}",
    )(xb, w_qkv)


def _dilated_group(qkv, bias_tab, g, batch, seq_len):
    window, dil = DIL_CONFIGS[g]
    nb = seq_len // (DIL_BLOCK * dil)
    has_prev = nb > 1
    blk = (1, 1, DIL_BLOCK, DIL_WIDTH)

    def cur(kind):
        return pl.BlockSpec(blk, lambda b, n, r: (b * nb + n, r, 0, kind))

    def prev(kind):
        return pl.BlockSpec(blk, lambda b, n, r: (b * nb + jnp.maximum(n - 1, 0), r, 0, kind))

    bias = _attn_bias(bias_tab, window, dil, has_prev)
    bias_spec = pl.BlockSpec(bias.shape, lambda b, n, r: (0, 0, 0))
    if has_prev:
        in_specs = [cur(0), cur(1), prev(1), cur(2), prev(2), bias_spec]
        args = (qkv,) * 5 + (bias,)
    else:
        in_specs = [cur(0), cur(1), cur(2), bias_spec]
        args = (qkv,) * 3 + (bias,)
    out_spec = pl.BlockSpec(blk, lambda b, n, r: (b * nb + n, r, 0, 0))
    out_sds = jax.ShapeDtypeStruct(qkv.shape[:3] + (DIL_WIDTH,), F32)
    return pl.pallas_call(
        functools.partial(_attn_kernel, has_prev=has_prev),
        grid=(batch, nb, dil),
        in_specs=in_specs,
        out_specs=[out_spec, out_spec],
        out_shape=[out_sds, out_sds],
        compiler_params=_params("parallel", "arbitrary", "arbitrary"),
        name=f"dilated_attn_g{g}",
    )(*args)


def _attn_out_kernel(*refs, tm, tiles):
    n_g = len(tiles)
    o_refs, l_refs = refs[:n_g], refs[n_g:2 * n_g]
    w_ref, x_ref, g_ref, b_ref, of_ref, ob_ref, op_ref = refs[2 * n_g:2 * n_g + 7]
    scratch = list(refs[2 * n_g + 7:])

    def token_order(ref, dil, runs):
        if dil == 1:
            return ref[...].reshape(tm, ref.shape[-1])
        buf = scratch.pop(0)
        n = ref.shape[2]
        for sp, r, first in runs:
            rows = ref[sp, r]
            for c in range(buf.shape[0]):
                buf[c, pl.ds(first, n, stride=dil), :] = rows[:, c * LANES:(c + 1) * LANES]
        return jnp.concatenate([buf[c] for c in range(buf.shape[0])], axis=1)

    os_ = [token_order(ref, dil, runs) for ref, (dil, runs) in zip(o_refs, tiles)]
    ls_ = [token_order(ref, dil, runs) for ref, (dil, runs) in zip(l_refs, tiles)]
    m = functools.reduce(jnp.maximum, ls_)
    es = [jnp.exp(a - m) for a in ls_]
    o = sum(e * ov for e, ov in zip(es, os_)) / sum(es)
    z = ALPHA * x_ref[...] + _dot(o.astype(BF16), w_ref[...])
    _store_ln_outputs(_layer_norm(z, g_ref[...], b_ref[...]), of_ref, ob_ref, op_ref)


def _attn_out(os_, ls_, w, x, g, b, tm=256):
    t, d = x.shape
    k = w.shape[0]
    row = lambda i: (i, 0)
    const = lambda i: (0, 0)
    specs, tiles = [], []
    for _, dil in DIL_CONFIGS:
        block, imap, runs = _subseq_tile(dil, tm)
        specs.append(pl.BlockSpec(block + (k,), imap))
        tiles.append((dil, runs))
    n_scratch = 2 * sum(1 for dil, _ in tiles if dil > 1)
    out_specs, out_shape = _ln_out_specs(t, d, tm)
    return pl.pallas_call(
        functools.partial(_attn_out_kernel, tm=tm, tiles=tiles),
        grid=(t // tm,),
        in_specs=specs * 2 + [pl.BlockSpec((k, d), const), pl.BlockSpec((tm, d), row),
                              pl.BlockSpec((1, d), const), pl.BlockSpec((1, d), const)],
        out_specs=out_specs,
        out_shape=out_shape,
        scratch_shapes=[pltpu.VMEM((k // LANES, tm, LANES), F32)] * n_scratch,
        compiler_params=_params("arbitrary"),
        name="attn_out_ln",
    )(*os_, *ls_, w, x, g, b)


def _router_kernel(x_ref, wt_ref, b_ref, eidx_ref, gate_ref, pos_ref, cnt_ref, base_ref, *, tm):
    @pl.when(pl.program_id(0) == 0)
    def _():
        base_ref[...] = jnp.zeros_like(base_ref)

    x = x_ref[...]
    w = wt_ref[...]
    xh = x.astype(BF16)
    xl = (x - xh.astype(F32)).astype(BF16)
    wh = w.astype(BF16)
    wl = (w - wh.astype(F32)).astype(BF16)
    dg = functools.partial(lax.dot_general, dimension_numbers=_NT, preferred_element_type=F32)
    logits = dg(wh, xh) + (dg(wh, xl) + dg(wl, xh))
    scores = 1.0 / (1.0 + jnp.exp(-logits))
    choice = scores + b_ref[...]

    iota_g = lax.broadcasted_iota(jnp.int32, (GROUP_SIZE, tm), 0)
    gs = []
    for gi in range(N_EXPERT_GROUPS):
        cg = choice[gi * GROUP_SIZE:(gi + 1) * GROUP_SIZE, :]
        t1 = jnp.max(cg, axis=0, keepdims=True)
        i1 = jnp.min(jnp.where(cg == t1, iota_g, GROUP_SIZE), axis=0, keepdims=True)
        t2 = jnp.max(jnp.where(iota_g == i1, -jnp.inf, cg), axis=0, keepdims=True)
        gs.append(t1 + t2)
    masked = []
    for gi in range(N_EXPERT_GROUPS):
        rank = jnp.zeros((1, tm), jnp.int32)
        for gj in range(N_EXPERT_GROUPS):
            if gj != gi:
                beats = (gs[gj] >= gs[gi]) if gj < gi else (gs[gj] > gs[gi])
                rank = rank + beats.astype(jnp.int32)
        masked.append(jnp.where(rank < TOPK_GROUPS, choice[gi * GROUP_SIZE:(gi + 1) * GROUP_SIZE, :], -jnp.inf))
    c = jnp.concatenate(masked, axis=0)

    iota_e = lax.broadcasted_iota(jnp.int32, (N_EXPERTS, tm), 0)
    picks, gates = [], []
    sel = jnp.zeros((N_EXPERTS, tm), F32)
    for _ in range(TOP_K):
        m = jnp.max(c, axis=0, keepdims=True)
        idx = jnp.min(jnp.where(c == m, iota_e, N_EXPERTS), axis=0, keepdims=True)
        hit = iota_e == idx
        picks.append(hit)
        gates.append(jnp.sum(jnp.where(hit, scores, 0.0), axis=0, keepdims=True))
        sel = jnp.where(hit, 1.0, sel)
        c = jnp.where(hit, -jnp.inf, c)
    gsum = gates[0]
    for gk in gates[1:]:
        gsum = gsum + gk

    before = (lax.broadcasted_iota(jnp.int32, (tm, tm), 0) < lax.broadcasted_iota(jnp.int32, (tm, tm), 1))
    rank_in_tile = _dot(sel.astype(BF16), before.astype(BF16))
    base = base_ref[...]
    posfull = base[:, 0:1] + rank_in_tile
    efull = iota_e.astype(F32)
    for k in range(TOP_K):
        hit = picks[k]
        eidx_ref[k:k + 1, :] = jnp.sum(jnp.where(hit, efull, 0.0), axis=0, keepdims=True).astype(jnp.int32)
        pos_ref[k:k + 1, :] = jnp.sum(jnp.where(hit, posfull, 0.0), axis=0, keepdims=True).astype(jnp.int32)
        gate_ref[k:k + 1, :] = gates[k] / gsum * ROUTED_SCALE
    new_base = base + jnp.sum(sel, axis=1, keepdims=True)
    base_ref[...] = new_base
    cnt_ref[...] = new_base


def _route(x, router_wt, router_b, tm=512):
    t, d = x.shape
    tok = pl.BlockSpec((TOP_K, tm), lambda i: (0, i))
    cnt = pl.BlockSpec((N_EXPERTS, LANES), lambda i: (0, 0))
    return pl.pallas_call(
        functools.partial(_router_kernel, tm=tm),
        grid=(t // tm,),
        in_specs=[pl.BlockSpec((tm, d), lambda i: (i, 0)),
                  pl.BlockSpec((N_EXPERTS, d), lambda i: (0, 0)),
                  pl.BlockSpec((N_EXPERTS, 1), lambda i: (0, 0))],
        out_specs=[tok, tok, tok, cnt],
        out_shape=[jax.ShapeDtypeStruct((TOP_K, t), jnp.int32), jax.ShapeDtypeStruct((TOP_K, t), F32),
                   jax.ShapeDtypeStruct((TOP_K, t), jnp.int32), jax.ShapeDtypeStruct((N_EXPERTS, LANES), F32)],
        scratch_shapes=[pltpu.VMEM((N_EXPERTS, LANES), F32)],
        compiler_params=_params("arbitrary"),
        name="router",
    )(x, router_wt, router_b)


def _row(ref, r):
    return ref.at[pl.ds(r, 1)]


def _dest_kernel(pstart_ref, eidx_ref, pos_ref, dest_ref):
    e = eidx_ref[...]
    acc = pos_ref[...]
    for k in range(N_EXPERTS):
        acc = acc + jnp.where(e == k, pstart_ref[k], 0)
    dest_ref[...] = acc


def _dest_rows(pstart, eidx, pos):
    full = pl.BlockSpec(eidx.shape, lambda i, ps: (0, 0))
    return pl.pallas_call(
        _dest_kernel,
        grid_spec=pltpu.PrefetchScalarGridSpec(num_scalar_prefetch=1, grid=(1,), in_specs=[full, full],
                                               out_specs=full),
        out_shape=jax.ShapeDtypeStruct(eidx.shape, jnp.int32),
        compiler_params=_params("arbitrary"),
        name="dest_rows",
    )(pstart, eidx, pos)


def _dispatch_kernel(pad_lo_ref, pad_hi_ref, dest_ref, x_ref, gate_ref, xs_hbm, src_ref, zero_ref, sem, pad_sem,
                     *, tm, n_blk, n_tok, n_tiles):
    i = pl.program_id(0)

    @pl.when(i == 0)
    def _():
        zero_ref[...] = jnp.zeros_like(zero_ref)

        def fill(lo, hi, copy):
            def start(r, c):
                copy(r).start()
                return c

            def wait(r, c):
                copy(r).wait()
                return c

            lax.fori_loop(lo, hi, start, 0)
            lax.fori_loop(lo, hi, wait, 0)

        def per_expert(e, carry):
            fill(pad_lo_ref[e], pad_hi_ref[e],
                 lambda r: pltpu.make_async_copy(_row(zero_ref, 0), _row(xs_hbm, r), pad_sem))
            return carry

        lax.fori_loop(0, N_EXPERTS, per_expert, 0)
        fill(pad_hi_ref[N_EXPERTS - 1] // ROW_BLOCK, n_blk,
             lambda blk: pltpu.make_async_copy(
                 zero_ref, xs_hbm.at[pl.ds(pl.multiple_of(blk * ROW_BLOCK, ROW_BLOCK), ROW_BLOCK)], pad_sem))

    x = x_ref[...]
    gate_bits = lax.bitcast_convert_type(gate_ref[...], jnp.int32)
    lane = lax.broadcasted_iota(jnp.int32, (tm, LANES), 1)
    token = lax.broadcasted_iota(jnp.int32, (tm, LANES), 0) + i * tm
    half = x.shape[1]
    def retire(par):
        for k in range(TOP_K):
            pltpu.make_async_copy(src_ref.at[par, k], xs_hbm.at[pl.ds(0, tm)], sem.at[par]).wait()

    for par in range(2):
        @pl.when(i % 2 == par)
        def _(par=par):
            for k in range(TOP_K):
                src_ref[par, k, :, 0:half] = x
                src_ref[par, k, :, half:] = jnp.where(lane == 0, token + k * n_tok,
                                                      jnp.where(lane == 1, gate_bits[:, k:k + 1], 0))

            def per_token(j, carry):
                for k in range(TOP_K):
                    pltpu.make_async_copy(src_ref.at[par, k, pl.ds(j, 1)],
                                          _row(xs_hbm, dest_ref[0, 0, j * TOP_K + k]), sem.at[par]).start()
                return carry

            lax.fori_loop(0, tm, per_token, 0)

            @pl.when(i >= 1)
            def _():
                retire(1 - par)

            @pl.when(i == n_tiles - 1)
            def _():
                retire(par)


def _dispatch(xp, dest_tok, gate_tok, pad_lo, pad_hi, n_rows, tm=256):
    t, half = xp.shape
    width = half + LANES
    n_tiles = t // tm
    dest3 = dest_tok.reshape(n_tiles, 1, tm * TOP_K)
    grid_spec = pltpu.PrefetchScalarGridSpec(
        num_scalar_prefetch=2,
        grid=(n_tiles,),
        in_specs=[pl.BlockSpec((1, 1, tm * TOP_K), lambda i, lo, hi: (i, 0, 0), memory_space=pltpu.SMEM),
                  pl.BlockSpec((tm, half), lambda i, lo, hi: (i, 0)),
                  pl.BlockSpec((tm, TOP_K), lambda i, lo, hi: (i, 0))],
        out_specs=pl.BlockSpec(memory_space=pl.ANY),
        scratch_shapes=[pltpu.VMEM((2, TOP_K, tm, width), jnp.int32),
                        pltpu.VMEM((ROW_BLOCK, width), jnp.int32),
                        pltpu.SemaphoreType.DMA((2,)), pltpu.SemaphoreType.DMA(())],
    )
    return pl.pallas_call(
        functools.partial(_dispatch_kernel, tm=tm, n_blk=n_rows // ROW_BLOCK, n_tok=t, n_tiles=n_tiles),
        grid_spec=grid_spec,
        out_shape=jax.ShapeDtypeStruct((n_rows, width), jnp.int32),
        compiler_params=_params("arbitrary"),
        name="dispatch",
    )(pad_lo, pad_hi, dest3, xp, gate_tok)


def _expert_kernel(blk_e_ref, nused_ref, nvalid_ref, xs_ref, wgu_ref, wd_ref, out_hbm,
                   wgu_bf, wd_bf, ybuf, idv, ids_smem, sem, id_sem, *, n_blk, n_slots):
    i = pl.program_id(0)
    nused = nused_ref[0]
    s = i % 2
    half = ybuf.shape[2]

    def rows_copy(parity):
        return pltpu.make_async_copy(ybuf.at[parity], out_hbm.at[pl.ds(0, ROW_BLOCK)], sem.at[parity])

    def spill_init(parity):
        return pltpu.make_async_copy(ybuf.at[parity], out_hbm.at[pl.ds(n_slots + parity * ROW_BLOCK, ROW_BLOCK)],
                                     sem.at[parity])

    def compute(par, retire_first, scatter_par=None):
        @pl.when((i == 0) | (blk_e_ref[i] != blk_e_ref[jnp.maximum(i - 1, 0)]))
        def _():
            wgu_bf[...] = wgu_ref[0, 0].astype(BF16)
            wd_bf[...] = wd_ref[0, 0].astype(BF16)

        if scatter_par is not None:
            ids_ready(scatter_par)

        meta = xs_ref[:, half:]
        row = lax.broadcasted_iota(jnp.int32, meta.shape, 0)
        dst = jnp.where(row < nvalid_ref[i], meta, n_slots + par * ROW_BLOCK + row)
        idv[par] = jnp.transpose(dst)[0:idv.shape[1], :]
        pltpu.make_async_copy(idv.at[par], ids_smem.at[par], id_sem.at[par]).start()

        gate = lax.bitcast_convert_type(meta[:, 1:2], F32)
        xs = _unpack_rows(xs_ref[:, 0:half])
        up_rows = UP_SHARE // UP_CHUNKS
        d = xs.shape[1]
        step = d // UP_CHUNKS
        hgu = None
        for c in range(UP_CHUNKS):
            part = _dot(xs[:, c * step:(c + 1) * step], wgu_bf[c * step:(c + 1) * step, :])
            hgu = part if hgu is None else hgu + part
            if scatter_par is not None:
                scatter(scatter_par, c * up_rows, up_rows)
        h = (_silu(hgu[:, :EXPERT_DIM]) * hgu[:, EXPERT_DIM:]).astype(BF16)
        if scatter_par is not None:
            scatter(scatter_par, UP_SHARE, ROW_BLOCK - UP_SHARE)
        y = _pack_rows(_dot(h, wd_bf[...]) * gate)
        if retire_first:
            rows_copy(par).wait()
        ybuf[par] = y

    def ids_ready(par):
        pltpu.make_async_copy(idv.at[par], ids_smem.at[par], id_sem.at[par]).wait()

    def scatter(par, first, count):
        for r in range(first, first + count):
            pltpu.make_async_copy(ybuf.at[par, pl.ds(r, 1)], _row(out_hbm, ids_smem[par, 0, r]),
                                  sem.at[par]).start()

    @pl.when(i == 0)
    def _():
        ybuf[...] = jnp.zeros_like(ybuf)
        spill_init(0).start()
        spill_init(0).wait()
        spill_init(1).start()
        compute(0, retire_first=False)

    for par in range(2):
        @pl.when((i >= 1) & (i < nused) & (s == par))
        def _(par=par):
            compute(par, retire_first=True, scatter_par=1 - par)

        @pl.when((i >= 1) & (i == nused) & (s == par))
        def _(par=par):
            ids_ready(1 - par)
            scatter(1 - par, 0, ROW_BLOCK)

    @pl.when((i >= nused) & (i >= 2) & (i - 2 < nused))
    def _():
        rows_copy(s).wait()

    @pl.when((i == n_blk - 1) & (i - 1 < nused))
    def _():
        rows_copy(1 - s).wait()


def _experts(xs, blk_e, nused, nvalid, w_gu, w_down, layer, n_slots):
    n_rows, width = xs.shape
    half = width - LANES
    n_blk = n_rows // ROW_BLOCK
    d, f2 = w_gu.shape[2], w_gu.shape[3]
    blk = lambda i, be, nu, nv: (jnp.minimum(i, nu[0] - 1), 0)
    wblk = lambda i, be, nu, nv: (layer, be[jnp.minimum(i, nu[0] - 1)], 0, 0)
    grid_spec = pltpu.PrefetchScalarGridSpec(
        num_scalar_prefetch=3,
        grid=(n_blk,),
        in_specs=[pl.BlockSpec((ROW_BLOCK, width), blk),
                  pl.BlockSpec((1, 1, d, f2), wblk),
                  pl.BlockSpec((1, 1, f2 // 2, d), wblk)],
        out_specs=pl.BlockSpec(memory_space=pl.ANY),
        scratch_shapes=[pltpu.VMEM((d, f2), BF16), pltpu.VMEM((f2 // 2, d), BF16),
                        pltpu.VMEM((2, ROW_BLOCK, half), jnp.int32),
                        pltpu.VMEM((2, 8, ROW_BLOCK), jnp.int32),
                        pltpu.SMEM((2, 8, ROW_BLOCK), jnp.int32),
                        pltpu.SemaphoreType.DMA((2,)), pltpu.SemaphoreType.DMA((2,))],
    )
    return pl.pallas_call(
        functools.partial(_expert_kernel, n_blk=n_blk, n_slots=n_slots),
        grid_spec=grid_spec,
        out_shape=jax.ShapeDtypeStruct((n_slots + 2 * ROW_BLOCK, half), jnp.int32),
        compiler_params=_params("arbitrary"),
        name="experts",
    )(blk_e, nused, nvalid, xs, w_gu, w_down)


def _combine_kernel(*refs):
    y_refs = refs[:TOP_K]
    xf_ref, xb_ref, sgu_ref, sd_ref, g_ref, b_ref, of_ref, ob_ref = refs[TOP_K:]
    lo = hi = None
    for y_ref in y_refs:
        p = y_ref[...]
        lo_k = lax.bitcast_convert_type(lax.shift_left(p, 16), F32)
        hi_k = lax.bitcast_convert_type(p & HIGH_HALF, F32)
        lo = lo_k if lo is None else lo + lo_k
        hi = hi_k if hi is None else hi + hi_k
    routed = jnp.concatenate([lo, hi], axis=1)

    sgu = _dot(xb_ref[...], sgu_ref[...])
    sh = _silu(sgu[:, :SHARED_DIM]) * sgu[:, SHARED_DIM:]
    shared = _dot(sh.astype(BF16), sd_ref[...])
    z = ALPHA * xf_ref[...] + (routed + shared)
    y = _layer_norm(z, g_ref[...], b_ref[...])
    of_ref[...] = y
    ob_ref[...] = y.astype(BF16)


def _combine(y, xf, xb, sw_gu, sw_down, g, b, tm=256):
    t, d = xf.shape
    n_tiles = t // tm
    row = lambda i: (i, 0)
    const = lambda i: (0, 0)
    y_specs = [pl.BlockSpec((tm, d // 2), functools.partial(lambda i, k: (k * n_tiles + i, 0), k=k))
               for k in range(TOP_K)]
    return pl.pallas_call(
        _combine_kernel,
        grid=(n_tiles,),
        in_specs=y_specs + [pl.BlockSpec((tm, d), row), pl.BlockSpec((tm, d), row),
                            pl.BlockSpec(sw_gu.shape, const), pl.BlockSpec(sw_down.shape, const),
                            pl.BlockSpec((1, d), const), pl.BlockSpec((1, d), const)],
        out_specs=[pl.BlockSpec((tm, d), row), pl.BlockSpec((tm, d), row)],
        out_shape=[jax.ShapeDtypeStruct((t, d), F32), jax.ShapeDtypeStruct((t, d), BF16)],
        compiler_params=_params("arbitrary"),
        name="combine_ln",
    )(*([y] * TOP_K), xf, xb, sw_gu, sw_down, g, b)


def _moe_layer(xf, xb, xp, router_w, router_b, w_gu, w_down, layer, sw_gu, sw_down, g, b):
    t, d = xf.shape
    eidx, gate, pos, cnt = _route(xf, router_w.T, router_b.reshape(N_EXPERTS, 1))

    sizes = cnt[:, 0].astype(jnp.int32)
    padded = (sizes + ROW_BLOCK - 1) // ROW_BLOCK * ROW_BLOCK
    pend = jnp.cumsum(padded)
    pstart = pend - padded
    tk = t * TOP_K
    n_rows = -(-tk // ROW_BLOCK) * ROW_BLOCK + N_EXPERTS * ROW_BLOCK
    blk_start = jnp.arange(n_rows // ROW_BLOCK, dtype=jnp.int32) * ROW_BLOCK
    blk_e = jnp.minimum(jnp.sum(pend[None, :] <= blk_start[:, None], axis=1), N_EXPERTS - 1).astype(jnp.int32)
    nused = (pend[-1:] // ROW_BLOCK).astype(jnp.int32)

    real_end = pstart + sizes
    nvalid = jnp.clip(real_end[blk_e] - blk_start, 0, ROW_BLOCK).astype(jnp.int32)

    dest_tok = _dest_rows(pstart, eidx, pos).T
    xs = _dispatch(xp, dest_tok, gate.T, real_end, pend, n_rows)
    y = _experts(xs, blk_e, nused, nvalid, w_gu, w_down, layer, tk)
    return _combine(y, xf, xb, sw_gu, sw_down, g, b)


def kernel(x, rel_bias, pool_w_in, pool_w_group, pool_scale, pool_w_out, dil_w_qkv, dil_w_out,
           conv_w_in, conv_w, conv_w_out, ln_gain, ln_bias, router_w, router_bias,
           expert_w_gu, expert_w_down, shared_w_gu, shared_w_down):
    bn, s, d = x.shape
    t = bn * s
    xf = x.reshape(t, d)
    xb = xf.astype(BF16)
    bf = lambda a: a.astype(BF16)
    ia = ib = ic = 0
    for i in range(DEPTH):
        kind = i % 3
        g1, b1 = ln_gain[i, 0].reshape(1, d), ln_bias[i, 0].reshape(1, d)
        g2, b2 = ln_gain[i, 1].reshape(1, d), ln_bias[i, 1].reshape(1, d)
        if kind == 0:
            mixed = _pool_mix(xb, bf(pool_w_in[ia]), bf(pool_w_group[ia]), pool_scale[ia].reshape(1, d), s)
            xf, xb, xp = _proj_ln(mixed, bf(pool_w_out[ia]), xf, g1, b1)
            ia += 1
        elif kind == 1:
            w_qkv = bf(dil_w_qkv[ib])
            os_, ls_ = [], []
            for g in range(len(DIL_CONFIGS)):
                qkv = _qkv_proj(xb, w_qkv, g)
                o, l = _dilated_group(qkv, rel_bias[:, g * DIL_HEADS:(g + 1) * DIL_HEADS], g, bn, s)
                os_.append(o)
                ls_.append(l)
            xf, xb, xp = _attn_out(os_, ls_, bf(dil_w_out[ib]), xf, g1, b1)
            ib += 1
        else:
            v = _conv_mix(xb, bf(conv_w_in[ic]), conv_w[ic], s)
            xf, xb, xp = _proj_ln(v, bf(conv_w_out[ic]), xf, g1, b1)
            ic += 1
        xf, xb = _moe_layer(xf, xb, xp, router_w[i], router_bias[i], expert_w_gu, expert_w_down, i,
                            bf(shared_w_gu[i]), bf(shared_w_down[i]), g2, b2)
    return xf.reshape(bn, s, d)
```

```python
import functools
import math

import jax
import jax.numpy as jnp
from jax import lax
from jax.experimental import pallas as pl
from jax.experimental.pallas import tpu as pltpu

F32 = jnp.float32
BF16 = jnp.bfloat16

DEPTH = 4
D_MODEL = 2048
POOL_WINDOWS = (2, 4, 8, 16)
POOL_GROUP_DIM = D_MODEL // len(POOL_WINDOWS)
POOL_HALO = 16
DIL_CONFIGS = ((128, 1), (512, 4), (2048, 16))
HEAD_DIM = 64
DIL_HEADS = 8
DIL_WIDTH = DIL_HEADS * HEAD_DIM
DIL_BLOCK = 128
N_BUCKETS = 32
MAX_DISTANCE = 2048
CONV_WIDTH = 3
CONV_HALO = 8
N_EXPERTS = 64
TOP_K = 8
N_EXPERT_GROUPS = 8
GROUP_SIZE = N_EXPERTS // N_EXPERT_GROUPS
TOPK_GROUPS = 4
EXPERT_DIM = 128
SHARED_DIM = 256
ROUTED_SCALE = 2.5
ROW_BLOCK = 256
ALPHA = (2.0 * DEPTH) ** 0.25
LN_EPS = 1e-5
NEG = -1e30

LANES = 128
HIGH_HALF = -65536
SLAB = D_MODEL // 2 // LANES + 1
UP_CHUNKS = 8
UP_SHARE = 160
VMEM_LIMIT = 52 * 1024 * 1024

_NT = (((1,), (1,)), ((), ()))


def _dot(a, b):
    return jnp.dot(a, b, preferred_element_type=F32)


def _params(*sem):
    return pltpu.CompilerParams(dimension_semantics=sem, vmem_limit_bytes=VMEM_LIMIT)


def _layer_norm(z, g, b):
    mu = jnp.mean(z, axis=-1, keepdims=True)
    zc = z - mu
    var = jnp.mean(zc * zc, axis=-1, keepdims=True)
    return zc * lax.rsqrt(var + LN_EPS) * g + b


def _silu(v):
    return v / (1.0 + jnp.exp(-v))


def _pack_pair(lo, hi):
    lo = lax.bitcast_convert_type(lo.astype(BF16).astype(F32), jnp.int32)
    hi = lax.bitcast_convert_type(hi.astype(BF16).astype(F32), jnp.int32)
    return (hi & HIGH_HALF) | lax.shift_right_logical(lo, 16)


def _pack_rows(y):
    half = D_MODEL // 2
    return _pack_pair(y[:, :half], y[:, half:])


def _unpack_rows(p):
    lo = lax.bitcast_convert_type(lax.shift_left(p, 16), F32)
    hi = lax.bitcast_convert_type(p & HIGH_HALF, F32)
    return jnp.concatenate([lo, hi], axis=1).astype(BF16)


def _store_ln_outputs(y, of_ref, ob_ref, op_ref):
    of_ref[...] = y
    ob_ref[...] = y.astype(BF16)
    op_ref[...] = _pack_rows(y)


def _ln_out_specs(t, d, tm):
    row = lambda i: (i, 0)
    specs = [pl.BlockSpec((tm, d), row), pl.BlockSpec((tm, d), row), pl.BlockSpec((tm, d // 2), row)]
    shapes = [jax.ShapeDtypeStruct((t, d), F32), jax.ShapeDtypeStruct((t, d), BF16),
              jax.ShapeDtypeStruct((t, d // 2), jnp.int32)]
    return specs, shapes


def _proj_ln_kernel(a_ref, w_ref, x_ref, g_ref, b_ref, of_ref, ob_ref, op_ref):
    z = ALPHA * x_ref[...] + _dot(a_ref[...], w_ref[...])
    _store_ln_outputs(_layer_norm(z, g_ref[...], b_ref[...]), of_ref, ob_ref, op_ref)


def _proj_ln(a, w, x, g, b, tm=256):
    t, k = a.shape
    d = w.shape[1]
    row = lambda i: (i, 0)
    const = lambda i: (0, 0)
    out_specs, out_shape = _ln_out_specs(t, d, tm)
    return pl.pallas_call(
        _proj_ln_kernel,
        grid=(t // tm,),
        in_specs=[pl.BlockSpec((tm, k), row), pl.BlockSpec((k, d), const),
                  pl.BlockSpec((tm, d), row), pl.BlockSpec((1, d), const),
                  pl.BlockSpec((1, d), const)],
        out_specs=out_specs,
        out_shape=out_shape,
        compiler_params=_params("arbitrary"),
        name="proj_ln",
    )(a, w, x, g, b)


def _pool_kernel(x_ref, win_ref, wg_ref, sc_ref, o_ref, carry_ref, *, tm, tiles_per_seq):
    i = pl.program_id(0)
    seq_tile = i % tiles_per_seq

    @pl.when(seq_tile == 0)
    def _():
        carry_ref[...] = jnp.zeros_like(carry_ref)

    xb = x_ref[...]
    pos = lax.broadcasted_iota(jnp.int32, (tm, 1), 0) + seq_tile * tm
    gd = POOL_GROUP_DIM
    for g, w in enumerate(POOL_WINDOWS):
        cols = slice(g * gd, (g + 1) * gd)
        u = _dot(xb, win_ref[:, cols])
        s = jnp.concatenate([carry_ref[g], u], axis=0)
        sh = 1
        while sh < w:
            s = s + pltpu.roll(s, sh, axis=0)
            sh *= 2
        cnt = jnp.minimum(pos + 1, w).astype(F32)
        pooled = s[POOL_HALO:, :] / cnt - u
        carry_ref[g] = u[tm - POOL_HALO:, :]
        mixed = _dot(pooled.astype(BF16), wg_ref[g]) * sc_ref[:, cols]
        o_ref[:, cols] = mixed.astype(BF16)


def _pool_mix(xb, w_in, w_group, scale, seq_len, tm=256):
    t, d = xb.shape
    gd = POOL_GROUP_DIM
    ng = len(POOL_WINDOWS)
    return pl.pallas_call(
        functools.partial(_pool_kernel, tm=tm, tiles_per_seq=seq_len // tm),
        grid=(t // tm,),
        in_specs=[pl.BlockSpec((tm, d), lambda i: (i, 0)),
                  pl.BlockSpec((d, d), lambda i: (0, 0)),
                  pl.BlockSpec((ng, gd, gd), lambda i: (0, 0, 0)),
                  pl.BlockSpec((1, d), lambda i: (0, 0))],
        out_specs=pl.BlockSpec((tm, d), lambda i: (i, 0)),
        out_shape=jax.ShapeDtypeStruct((t, d), BF16),
        scratch_shapes=[pltpu.VMEM((ng, POOL_HALO, gd), F32)],
        compiler_params=_params("arbitrary"),
        name="pool_mix",
    )(xb, w_in, w_group, scale)


def _conv_kernel(x_ref, wb_ref, wc_ref, wh_ref, cw_ref, o_ref, carry_ref, *, tm, tiles_per_seq):
    i = pl.program_id(1)

    @pl.when(i % tiles_per_seq == 0)
    def _():
        carry_ref[...] = jnp.zeros_like(carry_ref)

    xb = x_ref[...]
    b_gate = _dot(xb, wb_ref[...])
    u = _dot(xb, wc_ref[...]) * _dot(xb, wh_ref[...])
    ext = jnp.concatenate([carry_ref[...], u], axis=0)
    y = cw_ref[CONV_WIDTH - 1:CONV_WIDTH, :] * u
    for back in range(1, CONV_WIDTH):
        tap = CONV_WIDTH - 1 - back
        y = y + cw_ref[tap:tap + 1, :] * pltpu.roll(ext, back, axis=0)[CONV_HALO:, :]
    carry_ref[...] = u[tm - CONV_HALO:, :]
    o_ref[...] = (b_gate * y).astype(BF16)


def _conv_mix(xb, w_in, conv_w, seq_len, tm=512, tn=512):
    t, d = xb.shape
    nj = d // tn
    return pl.pallas_call(
        functools.partial(_conv_kernel, tm=tm, tiles_per_seq=seq_len // tm),
        grid=(nj, t // tm),
        in_specs=[pl.BlockSpec((tm, d), lambda j, i: (i, 0)),
                  pl.BlockSpec((d, tn), lambda j, i: (0, j)),
                  pl.BlockSpec((d, tn), lambda j, i: (0, nj + j)),
                  pl.BlockSpec((d, tn), lambda j, i: (0, 2 * nj + j)),
                  pl.BlockSpec((CONV_WIDTH, tn), lambda j, i: (0, j))],
        out_specs=pl.BlockSpec((tm, tn), lambda j, i: (i, j)),
        out_shape=jax.ShapeDtypeStruct((t, d), BF16),
        scratch_shapes=[pltpu.VMEM((CONV_HALO, tn), F32)],
        compiler_params=_params("parallel", "arbitrary"),
        name="conv_mix",
    )(xb, w_in, w_in, w_in, conv_w)


def _t5_bucket(dist):
    max_exact = N_BUCKETS // 2
    is_small = dist < max_exact
    distf = jnp.maximum(dist, 1).astype(F32)
    large = max_exact + (jnp.log(distf / max_exact) / math.log(MAX_DISTANCE / max_exact)
                         * (N_BUCKETS - max_exact)).astype(jnp.int32)
    large = jnp.minimum(large, N_BUCKETS - 1)
    return jnp.where(is_small, dist, large)


def _attn_bias(bias_tab, window, dil, has_prev):
    steps = window // dil
    q_loc = jnp.arange(DIL_BLOCK)[:, None]
    k_loc = jnp.arange(2 * DIL_BLOCK)[None, :]
    rel = q_loc + DIL_BLOCK - k_loc
    band = (rel >= 0) & (rel <= steps)
    bucket = _t5_bucket(jnp.maximum(rel, 0) * dil)
    tab = bias_tab.astype(F32).T[:, :, None, None]
    bias = sum(jnp.where(bucket == b, tab[:, b], 0.0) for b in range(N_BUCKETS))
    bias = jnp.where(band[None], bias, NEG)
    return bias if has_prev else bias[:, :, DIL_BLOCK:]


def _attn_kernel(*refs, has_prev):
    if has_prev:
        q_ref, kc_ref, kp_ref, vc_ref, vp_ref, bias_ref, o_ref, l_ref = refs
    else:
        q_ref, kc_ref, vc_ref, bias_ref, o_ref, l_ref = refs
    n = pl.program_id(1)
    q = q_ref[0, 0]
    kc = kc_ref[0, 0]
    vc = vc_ref[0, 0]
    if has_prev:
        kp = kp_ref[0, 0]
        vp = vp_ref[0, 0]
        col = lax.broadcasted_iota(jnp.int32, (DIL_BLOCK, 2 * DIL_BLOCK), 1)
        keep = (col >= DIL_BLOCK) | (n > 0)
    outs, lses = [], []
    for h in range(DIL_HEADS):
        sl = slice(h * HEAD_DIM, (h + 1) * HEAD_DIM)
        if has_prev:
            kh = jnp.concatenate([kp[:, sl], kc[:, sl]], axis=0)
            vh = jnp.concatenate([vp[:, sl], vc[:, sl]], axis=0)
        else:
            kh, vh = kc[:, sl], vc[:, sl]
        s = lax.dot_general(q[:, sl], kh, _NT, preferred_element_type=F32) * (HEAD_DIM ** -0.5) + bias_ref[h]
        if has_prev:
            s = jnp.where(keep, s, NEG)
        m = jnp.max(s, axis=-1, keepdims=True)
        p = jnp.exp(s - m)
        l = jnp.sum(p, axis=-1, keepdims=True)
        outs.append(_dot(p.astype(BF16), vh) / l)
        lses.append(jnp.broadcast_to(m + jnp.log(l), (DIL_BLOCK, HEAD_DIM)))
    o_ref[0, 0] = jnp.concatenate(outs, axis=1)
    l_ref[0, 0] = jnp.concatenate(lses, axis=1)


def _subseq_tile(dil, tm):
    span = DIL_BLOCK * dil
    if tm >= span:
        block = (tm // span, dil, DIL_BLOCK)
        imap = lambda i: (i, 0, 0, 0)
    else:
        per_span = span // tm
        block = (1, dil, tm // dil)
        imap = lambda i: (i // per_span, 0, i % per_span, 0)
    runs = [(sp, r, sp * span + r) for sp in range(block[0]) for r in range(dil)]
    return block, imap, runs


def _qkv_kernel(x_ref, w_ref, o_ref, *, dil):
    x = x_ref[...]
    if dil > 1:
        tm, n = x.shape[0], o_ref.shape[2]
        out_row = lax.broadcasted_iota(jnp.int32, (tm, tm), 0)
        in_row = lax.broadcasted_iota(jnp.int32, (tm, tm), 1)
        pick = in_row == (out_row % n) * dil + out_row // n
        x = _dot(pick.astype(BF16), x).astype(BF16)
    o_ref[...] = _dot(x, w_ref[...]).astype(o_ref.dtype).reshape(o_ref.shape)


def _qkv_proj(xb, w_qkv, g, tm=512):
    t, d = xb.shape
    _, dil = DIL_CONFIGS[g]
    c = 3 * DIL_WIDTH
    block, imap, _ = _subseq_tile(dil, tm)
    assert dil == 1 or block[0] == 1, "the in-kernel row reorder handles one span (or part of one) per tile"
    return pl.pallas_call(
        functools.partial(_qkv_kernel, dil=dil),
        grid=(t // tm,),
        in_specs=[pl.BlockSpec((tm, d), lambda i: (i, 0)),
                  pl.BlockSpec((d, c), lambda i: (0, g))],
        out_specs=pl.BlockSpec(block + (c,), imap),
        out_shape=jax.ShapeDtypeStruct((t // (DIL_BLOCK * dil), dil, DIL_BLOCK, c), BF16),
        compiler_params=_params("arbitrary"),
        name=f"qkv_proj_g{g}",
    )(xb, w_qkv)


def _dilated_group(qkv, bias_tab, g, batch, seq_len):
    window, dil = DIL_CONFIGS[g]
    nb = seq_len // (DIL_BLOCK * dil)
    has_prev = nb > 1
    blk = (1, 1, DIL_BLOCK, DIL_WIDTH)

    def cur(kind):
        return pl.BlockSpec(blk, lambda b, n, r: (b * nb + n, r, 0, kind))

    def prev(kind):
        return pl.BlockSpec(blk, lambda b, n, r: (b * nb + jnp.maximum(n - 1, 0), r, 0, kind))

    bias = _attn_bias(bias_tab, window, dil, has_prev)
    bias_spec = pl.BlockSpec(bias.shape, lambda b, n, r: (0, 0, 0))
    if has_prev:
        in_specs = [cur(0), cur(1), prev(1), cur(2), prev(2), bias_spec]
        args = (qkv,) * 5 + (bias,)
    else:
        in_specs = [cur(0), cur(1), cur(2), bias_spec]
        args = (qkv,) * 3 + (bias,)
    out_spec = pl.BlockSpec(blk, lambda b, n, r: (b * nb + n, r, 0, 0))
    out_sds = jax.ShapeDtypeStruct(qkv.shape[:3] + (DIL_WIDTH,), F32)
    return pl.pallas_call(
        functools.partial(_attn_kernel, has_prev=has_prev),
        grid=(batch, nb, dil),
        in_specs=in_specs,
        out_specs=[out_spec, out_spec],
        out_shape=[out_sds, out_sds],
        compiler_params=_params("parallel", "arbitrary", "arbitrary"),
        name=f"dilated_attn_g{g}",
    )(*args)


def _attn_out_kernel(*refs, tm, tiles):
    n_g = len(tiles)
    o_refs, l_refs = refs[:n_g], refs[n_g:2 * n_g]
    w_ref, x_ref, g_ref, b_ref, of_ref, ob_ref, op_ref = refs[2 * n_g:2 * n_g + 7]
    scratch = list(refs[2 * n_g + 7:])

    def token_order(ref, dil, runs):
        if dil == 1:
            return ref[...].reshape(tm, ref.shape[-1])
        buf = scratch.pop(0)
        n = ref.shape[2]
        for sp, r, first in runs:
            rows = ref[sp, r]
            for c in range(buf.shape[0]):
                buf[c, pl.ds(first, n, stride=dil), :] = rows[:, c * LANES:(c + 1) * LANES]
        return jnp.concatenate([buf[c] for c in range(buf.shape[0])], axis=1)

    os_ = [token_order(ref, dil, runs) for ref, (dil, runs) in zip(o_refs, tiles)]
    ls_ = [token_order(ref, dil, runs) for ref, (dil, runs) in zip(l_refs, tiles)]
    m = functools.reduce(jnp.maximum, ls_)
    es = [jnp.exp(a - m) for a in ls_]
    o = sum(e * ov for e, ov in zip(es, os_)) / sum(es)
    z = ALPHA * x_ref[...] + _dot(o.astype(BF16), w_ref[...])
    _store_ln_outputs(_layer_norm(z, g_ref[...], b_ref[...]), of_ref, ob_ref, op_ref)


def _attn_out(os_, ls_, w, x, g, b, tm=256):
    t, d = x.shape
    k = w.shape[0]
    row = lambda i: (i, 0)
    const = lambda i: (0, 0)
    specs, tiles = [], []
    for _, dil in DIL_CONFIGS:
        block, imap, runs = _subseq_tile(dil, tm)
        specs.append(pl.BlockSpec(block + (k,), imap))
        tiles.append((dil, runs))
    n_scratch = 2 * sum(1 for dil, _ in tiles if dil > 1)
    out_specs, out_shape = _ln_out_specs(t, d, tm)
    return pl.pallas_call(
        functools.partial(_attn_out_kernel, tm=tm, tiles=tiles),
        grid=(t // tm,),
        in_specs=specs * 2 + [pl.BlockSpec((k, d), const), pl.BlockSpec((tm, d), row),
                              pl.BlockSpec((1, d), const), pl.BlockSpec((1, d), const)],
        out_specs=out_specs,
        out_shape=out_shape,
        scratch_shapes=[pltpu.VMEM((k // LANES, tm, LANES), F32)] * n_scratch,
        compiler_params=_params("arbitrary"),
        name="attn_out_ln",
    )(*os_, *ls_, w, x, g, b)


def _router_kernel(x_ref, wt_ref, b_ref, eidx_ref, gate_ref, pos_ref, cnt_ref, base_ref, *, tm):
    @pl.when(pl.program_id(0) == 0)
    def _():
        base_ref[...] = jnp.zeros_like(base_ref)

    x = x_ref[...]
    w = wt_ref[...]
    xh = x.astype(BF16)
    xl = (x - xh.astype(F32)).astype(BF16)
    wh = w.astype(BF16)
    wl = (w - wh.astype(F32)).astype(BF16)
    dg = functools.partial(lax.dot_general, dimension_numbers=_NT, preferred_element_type=F32)
    logits = dg(wh, xh) + (dg(wh, xl) + dg(wl, xh))
    scores = 1.0 / (1.0 + jnp.exp(-logits))
    choice = scores + b_ref[...]

    iota_g = lax.broadcasted_iota(jnp.int32, (GROUP_SIZE, tm), 0)
    gs = []
    for gi in range(N_EXPERT_GROUPS):
        cg = choice[gi * GROUP_SIZE:(gi + 1) * GROUP_SIZE, :]
        t1 = jnp.max(cg, axis=0, keepdims=True)
        i1 = jnp.min(jnp.where(cg == t1, iota_g, GROUP_SIZE), axis=0, keepdims=True)
        t2 = jnp.max(jnp.where(iota_g == i1, -jnp.inf, cg), axis=0, keepdims=True)
        gs.append(t1 + t2)
    masked = []
    for gi in range(N_EXPERT_GROUPS):
        rank = jnp.zeros((1, tm), jnp.int32)
        for gj in range(N_EXPERT_GROUPS):
            if gj != gi:
                beats = (gs[gj] >= gs[gi]) if gj < gi else (gs[gj] > gs[gi])
                rank = rank + beats.astype(jnp.int32)
        masked.append(jnp.where(rank < TOPK_GROUPS, choice[gi * GROUP_SIZE:(gi + 1) * GROUP_SIZE, :], -jnp.inf))
    c = jnp.concatenate(masked, axis=0)

    iota_e = lax.broadcasted_iota(jnp.int32, (N_EXPERTS, tm), 0)
    picks, gates = [], []
    sel = jnp.zeros((N_EXPERTS, tm), F32)
    for _ in range(TOP_K):
        m = jnp.max(c, axis=0, keepdims=True)
        idx = jnp.min(jnp.where(c == m, iota_e, N_EXPERTS), axis=0, keepdims=True)
        hit = iota_e == idx
        picks.append(hit)
        gates.append(jnp.sum(jnp.where(hit, scores, 0.0), axis=0, keepdims=True))
        sel = jnp.where(hit, 1.0, sel)
        c = jnp.where(hit, -jnp.inf, c)
    gsum = gates[0]
    for gk in gates[1:]:
        gsum = gsum + gk

    before = (lax.broadcasted_iota(jnp.int32, (tm, tm), 0) < lax.broadcasted_iota(jnp.int32, (tm, tm), 1))
    rank_in_tile = _dot(sel.astype(BF16), before.astype(BF16))
    base = base_ref[...]
    posfull = base[:, 0:1] + rank_in_tile
    efull = iota_e.astype(F32)
    for k in range(TOP_K):
        hit = picks[k]
        eidx_ref[k:k + 1, :] = jnp.sum(jnp.where(hit, efull, 0.0), axis=0, keepdims=True).astype(jnp.int32)
        pos_ref[k:k + 1, :] = jnp.sum(jnp.where(hit, posfull, 0.0), axis=0, keepdims=True).astype(jnp.int32)
        gate_ref[k:k + 1, :] = gates[k] / gsum * ROUTED_SCALE
    new_base = base + jnp.sum(sel, axis=1, keepdims=True)
    base_ref[...] = new_base
    cnt_ref[...] = new_base


def _route(x, router_wt, router_b, tm=512):
    t, d = x.shape
    tok = pl.BlockSpec((TOP_K, tm), lambda i: (0, i))
    cnt = pl.BlockSpec((N_EXPERTS, LANES), lambda i: (0, 0))
    return pl.pallas_call(
        functools.partial(_router_kernel, tm=tm),
        grid=(t // tm,),
        in_specs=[pl.BlockSpec((tm, d), lambda i: (i, 0)),
                  pl.BlockSpec((N_EXPERTS, d), lambda i: (0, 0)),
                  pl.BlockSpec((N_EXPERTS, 1), lambda i: (0, 0))],
        out_specs=[tok, tok, tok, cnt],
        out_shape=[jax.ShapeDtypeStruct((TOP_K, t), jnp.int32), jax.ShapeDtypeStruct((TOP_K, t), F32),
                   jax.ShapeDtypeStruct((TOP_K, t), jnp.int32), jax.ShapeDtypeStruct((N_EXPERTS, LANES), F32)],
        scratch_shapes=[pltpu.VMEM((N_EXPERTS, LANES), F32)],
        compiler_params=_params("arbitrary"),
        name="router",
    )(x, router_wt, router_b)


def _row(ref, r):
    return ref.at[pl.ds(r, 1)]


def _slab(ref, r):
    return ref.at[pl.ds(r * SLAB, SLAB)]


def _dest_kernel(pstart_ref, eidx_ref, pos_ref, dest_ref):
    e = eidx_ref[...]
    acc = pos_ref[...]
    for k in range(N_EXPERTS):
        acc = acc + jnp.where(e == k, pstart_ref[k], 0)
    dest_ref[...] = acc


def _dest_rows(pstart, eidx, pos):
    full = pl.BlockSpec(eidx.shape, lambda i, ps: (0, 0))
    return pl.pallas_call(
        _dest_kernel,
        grid_spec=pltpu.PrefetchScalarGridSpec(num_scalar_prefetch=1, grid=(1,), in_specs=[full, full],
                                               out_specs=full),
        out_shape=jax.ShapeDtypeStruct(eidx.shape, jnp.int32),
        compiler_params=_params("arbitrary"),
        name="dest_rows",
    )(pstart, eidx, pos)


def _dispatch_kernel(pad_lo_ref, pad_hi_ref, dest_ref, x_ref, gate_ref, xs_hbm, src_ref, zero_ref, sem, pad_sem,
                     *, tm, n_blk, n_tok):
    i = pl.program_id(0)

    @pl.when(i == 0)
    def _():
        zero_ref[...] = jnp.zeros_like(zero_ref)

        def fill(lo, hi, copy):
            def start(r, c):
                copy(r).start()
                return c

            def wait(r, c):
                copy(r).wait()
                return c

            lax.fori_loop(lo, hi, start, 0)
            lax.fori_loop(lo, hi, wait, 0)

        def per_expert(e, carry):
            fill(pad_lo_ref[e], pad_hi_ref[e],
                 lambda r: pltpu.make_async_copy(_slab(zero_ref, 0), _slab(xs_hbm, r), pad_sem))
            return carry

        blk_rows = ROW_BLOCK * SLAB
        lax.fori_loop(0, N_EXPERTS, per_expert, 0)
        fill(pad_hi_ref[N_EXPERTS - 1] // ROW_BLOCK, n_blk,
             lambda blk: pltpu.make_async_copy(
                 zero_ref, xs_hbm.at[pl.ds(pl.multiple_of(blk * blk_rows, blk_rows), blk_rows)], pad_sem))

    x = x_ref[...]
    gate_bits = lax.bitcast_convert_type(gate_ref[...], jnp.int32)
    lane = lax.broadcasted_iota(jnp.int32, (tm, LANES), 1)
    token = lax.broadcasted_iota(jnp.int32, (tm, LANES), 0) + i * tm
    for k in range(TOP_K):
        for c in range(SLAB - 1):
            src_ref[k, pl.ds(c, tm, stride=SLAB), :] = x[:, c * LANES:(c + 1) * LANES]
        src_ref[k, pl.ds(SLAB - 1, tm, stride=SLAB), :] = jnp.where(
            lane == 0, token + k * n_tok, jnp.where(lane == 1, gate_bits[:, k:k + 1], 0))

    def per_token(j, carry):
        for k in range(TOP_K):
            pltpu.make_async_copy(_slab(src_ref.at[k], j), _slab(xs_hbm, dest_ref[0, 0, j * TOP_K + k]),
                                  sem).start()
        return carry

    lax.fori_loop(0, tm, per_token, 0)
    for k in range(TOP_K):
        pltpu.make_async_copy(src_ref.at[k], xs_hbm.at[pl.ds(0, tm * SLAB)], sem).wait()


def _dispatch(xp, dest_tok, gate_tok, pad_lo, pad_hi, n_rows, tm=512):
    t, half = xp.shape
    assert half == (SLAB - 1) * LANES
    n_tiles = t // tm
    dest3 = dest_tok.reshape(n_tiles, 1, tm * TOP_K)
    grid_spec = pltpu.PrefetchScalarGridSpec(
        num_scalar_prefetch=2,
        grid=(n_tiles,),
        in_specs=[pl.BlockSpec((1, 1, tm * TOP_K), lambda i, lo, hi: (i, 0, 0), memory_space=pltpu.SMEM),
                  pl.BlockSpec((tm, half), lambda i, lo, hi: (i, 0)),
                  pl.BlockSpec((tm, TOP_K), lambda i, lo, hi: (i, 0))],
        out_specs=pl.BlockSpec(memory_space=pl.ANY),
        scratch_shapes=[pltpu.VMEM((TOP_K, tm * SLAB, LANES), jnp.int32),
                        pltpu.VMEM((ROW_BLOCK * SLAB, LANES), jnp.int32),
                        pltpu.SemaphoreType.DMA(()), pltpu.SemaphoreType.DMA(())],
    )
    return pl.pallas_call(
        functools.partial(_dispatch_kernel, tm=tm, n_blk=n_rows // ROW_BLOCK, n_tok=t),
        grid_spec=grid_spec,
        out_shape=jax.ShapeDtypeStruct((n_rows * SLAB, LANES), jnp.int32),
        compiler_params=_params("arbitrary"),
        name="dispatch",
    )(pad_lo, pad_hi, dest3, xp, gate_tok)


def _expert_kernel(blk_e_ref, nused_ref, nvalid_ref, xs_ref, wgu_ref, wd_ref, out_hbm,
                   wgu_bf, wd_bf, ybuf, idv, ids_smem, sem, id_sem, *, n_blk, n_slots):
    i = pl.program_id(0)
    nused = nused_ref[0]
    s = i % 2
    half = ybuf.shape[2]

    def rows_copy(parity):
        return pltpu.make_async_copy(ybuf.at[parity], out_hbm.at[pl.ds(0, ROW_BLOCK)], sem.at[parity])

    def spill_init(parity):
        return pltpu.make_async_copy(ybuf.at[parity], out_hbm.at[pl.ds(n_slots + parity * ROW_BLOCK, ROW_BLOCK)],
                                     sem.at[parity])

    def compute(par, retire_first, scatter_par=None):
        @pl.when((i == 0) | (blk_e_ref[i] != blk_e_ref[jnp.maximum(i - 1, 0)]))
        def _():
            wgu_bf[...] = wgu_ref[0, 0].astype(BF16)
            wd_bf[...] = wd_ref[0, 0].astype(BF16)

        if scatter_par is not None:
            ids_ready(scatter_par)

        tiles = [xs_ref[pl.ds(c, ROW_BLOCK, stride=SLAB), :] for c in range(SLAB)]
        meta = tiles[SLAB - 1]
        row = lax.broadcasted_iota(jnp.int32, meta.shape, 0)
        dst = jnp.where(row < nvalid_ref[i], meta, n_slots + par * ROW_BLOCK + row)
        idv[par] = jnp.transpose(dst)[0:idv.shape[1], :]
        pltpu.make_async_copy(idv.at[par], ids_smem.at[par], id_sem.at[par]).start()

        gate = lax.bitcast_convert_type(meta[:, 1:2], F32)
        xs = _unpack_rows(jnp.concatenate(tiles[:SLAB - 1], axis=1))
        up_rows = UP_SHARE // UP_CHUNKS
        d = xs.shape[1]
        step = d // UP_CHUNKS
        hgu = None
        for c in range(UP_CHUNKS):
            part = _dot(xs[:, c * step:(c + 1) * step], wgu_bf[c * step:(c + 1) * step, :])
            hgu = part if hgu is None else hgu + part
            if scatter_par is not None:
                scatter(scatter_par, c * up_rows, up_rows)
        h = (_silu(hgu[:, :EXPERT_DIM]) * hgu[:, EXPERT_DIM:]).astype(BF16)
        if scatter_par is not None:
            scatter(scatter_par, UP_SHARE, ROW_BLOCK - UP_SHARE)
        y = _pack_rows(_dot(h, wd_bf[...]) * gate)
        if retire_first:
            rows_copy(par).wait()
        ybuf[par] = y

    def ids_ready(par):
        pltpu.make_async_copy(idv.at[par], ids_smem.at[par], id_sem.at[par]).wait()

    def scatter(par, first, count):
        for r in range(first, first + count):
            pltpu.make_async_copy(ybuf.at[par, pl.ds(r, 1)], _row(out_hbm, ids_smem[par, 0, r]),
                                  sem.at[par]).start()

    @pl.when(i == 0)
    def _():
        ybuf[...] = jnp.zeros_like(ybuf)
        spill_init(0).start()
        spill_init(0).wait()
        spill_init(1).start()
        compute(0, retire_first=False)

    for par in range(2):
        @pl.when((i >= 1) & (i < nused) & (s == par))
        def _(par=par):
            compute(par, retire_first=True, scatter_par=1 - par)

        @pl.when((i >= 1) & (i == nused) & (s == par))
        def _(par=par):
            ids_ready(1 - par)
            scatter(1 - par, 0, ROW_BLOCK)

    @pl.when((i >= nused) & (i >= 2) & (i - 2 < nused))
    def _():
        rows_copy(s).wait()

    @pl.when((i == n_blk - 1) & (i - 1 < nused))
    def _():
        rows_copy(1 - s).wait()


def _experts(xs, blk_e, nused, nvalid, w_gu, w_down, layer, n_slots):
    half = (SLAB - 1) * LANES
    n_blk = xs.shape[0] // (ROW_BLOCK * SLAB)
    d, f2 = w_gu.shape[2], w_gu.shape[3]
    blk = lambda i, be, nu, nv: (jnp.minimum(i, nu[0] - 1), 0)
    wblk = lambda i, be, nu, nv: (layer, be[jnp.minimum(i, nu[0] - 1)], 0, 0)
    grid_spec = pltpu.PrefetchScalarGridSpec(
        num_scalar_prefetch=3,
        grid=(n_blk,),
        in_specs=[pl.BlockSpec((ROW_BLOCK * SLAB, LANES), blk),
                  pl.BlockSpec((1, 1, d, f2), wblk),
                  pl.BlockSpec((1, 1, f2 // 2, d), wblk)],
        out_specs=pl.BlockSpec(memory_space=pl.ANY),
        scratch_shapes=[pltpu.VMEM((d, f2), BF16), pltpu.VMEM((f2 // 2, d), BF16),
                        pltpu.VMEM((2, ROW_BLOCK, half), jnp.int32),
                        pltpu.VMEM((2, 8, ROW_BLOCK), jnp.int32),
                        pltpu.SMEM((2, 8, ROW_BLOCK), jnp.int32),
                        pltpu.SemaphoreType.DMA((2,)), pltpu.SemaphoreType.DMA((2,))],
    )
    return pl.pallas_call(
        functools.partial(_expert_kernel, n_blk=n_blk, n_slots=n_slots),
        grid_spec=grid_spec,
        out_shape=jax.ShapeDtypeStruct((n_slots + 2 * ROW_BLOCK, half), jnp.int32),
        compiler_params=_params("arbitrary"),
        name="experts",
    )(blk_e, nused, nvalid, xs, w_gu, w_down)


def _combine_kernel(*refs):
    y_refs = refs[:TOP_K]
    xf_ref, xb_ref, sgu_ref, sd_ref, g_ref, b_ref, of_ref, ob_ref = refs[TOP_K:]
    lo = hi = None
    for y_ref in y_refs:
        p = y_ref[...]
        lo_k = lax.bitcast_convert_type(lax.shift_left(p, 16), F32)
        hi_k = lax.bitcast_convert_type(p & HIGH_HALF, F32)
        lo = lo_k if lo is None else lo + lo_k
        hi = hi_k if hi is None else hi + hi_k
    routed = jnp.concatenate([lo, hi], axis=1)

    sgu = _dot(xb_ref[...], sgu_ref[...])
    sh = _silu(sgu[:, :SHARED_DIM]) * sgu[:, SHARED_DIM:]
    shared = _dot(sh.astype(BF16), sd_ref[...])
    z = ALPHA * xf_ref[...] + (routed + shared)
    y = _layer_norm(z, g_ref[...], b_ref[...])
    of_ref[...] = y
    ob_ref[...] = y.astype(BF16)


def _combine(y, xf, xb, sw_gu, sw_down, g, b, tm=256):
    t, d = xf.shape
    n_tiles = t // tm
    row = lambda i: (i, 0)
    const = lambda i: (0, 0)
    y_specs = [pl.BlockSpec((tm, d // 2), functools.partial(lambda i, k: (k * n_tiles + i, 0), k=k))
               for k in range(TOP_K)]
    return pl.pallas_call(
        _combine_kernel,
        grid=(n_tiles,),
        in_specs=y_specs + [pl.BlockSpec((tm, d), row), pl.BlockSpec((tm, d), row),
                            pl.BlockSpec(sw_gu.shape, const), pl.BlockSpec(sw_down.shape, const),
                            pl.BlockSpec((1, d), const), pl.BlockSpec((1, d), const)],
        out_specs=[pl.BlockSpec((tm, d), row), pl.BlockSpec((tm, d), row)],
        out_shape=[jax.ShapeDtypeStruct((t, d), F32), jax.ShapeDtypeStruct((t, d), BF16)],
        compiler_params=_params("arbitrary"),
        name="combine_ln",
    )(*([y] * TOP_K), xf, xb, sw_gu, sw_down, g, b)


def _moe_layer(xf, xb, xp, router_w, router_b, w_gu, w_down, layer, sw_gu, sw_down, g, b):
    t, d = xf.shape
    eidx, gate, pos, cnt = _route(xf, router_w.T, router_b.reshape(N_EXPERTS, 1))

    sizes = cnt[:, 0].astype(jnp.int32)
    padded = (sizes + ROW_BLOCK - 1) // ROW_BLOCK * ROW_BLOCK
    pend = jnp.cumsum(padded)
    pstart = pend - padded
    tk = t * TOP_K
    n_rows = -(-tk // ROW_BLOCK) * ROW_BLOCK + N_EXPERTS * ROW_BLOCK
    blk_start = jnp.arange(n_rows // ROW_BLOCK, dtype=jnp.int32) * ROW_BLOCK
    blk_e = jnp.minimum(jnp.sum(pend[None, :] <= blk_start[:, None], axis=1), N_EXPERTS - 1).astype(jnp.int32)
    nused = (pend[-1:] // ROW_BLOCK).astype(jnp.int32)

    real_end = pstart + sizes
    nvalid = jnp.clip(real_end[blk_e] - blk_start, 0, ROW_BLOCK).astype(jnp.int32)

    dest_tok = _dest_rows(pstart, eidx, pos).T
    xs = _dispatch(xp, dest_tok, gate.T, real_end, pend, n_rows)
    y = _experts(xs, blk_e, nused, nvalid, w_gu, w_down, layer, tk)
    return _combine(y, xf, xb, sw_gu, sw_down, g, b)


def kernel(x, rel_bias, pool_w_in, pool_w_group, pool_scale, pool_w_out, dil_w_qkv, dil_w_out,
           conv_w_in, conv_w, conv_w_out, ln_gain, ln_bias, router_w, router_bias,
           expert_w_gu, expert_w_down, shared_w_gu, shared_w_down):
    bn, s, d = x.shape
    t = bn * s
    xf = x.reshape(t, d)
    xb = xf.astype(BF16)
    bf = lambda a: a.astype(BF16)
    ia = ib = ic = 0
    for i in range(DEPTH):
        kind = i % 3
        g1, b1 = ln_gain[i, 0].reshape(1, d), ln_bias[i, 0].reshape(1, d)
        g2, b2 = ln_gain[i, 1].reshape(1, d), ln_bias[i, 1].reshape(1, d)
        if kind == 0:
            mixed = _pool_mix(xb, bf(pool_w_in[ia]), bf(pool_w_group[ia]), pool_scale[ia].reshape(1, d), s)
            xf, xb, xp = _proj_ln(mixed, bf(pool_w_out[ia]), xf, g1, b1)
            ia += 1
        elif kind == 1:
            w_qkv = bf(dil_w_qkv[ib])
            os_, ls_ = [], []
            for g in range(len(DIL_CONFIGS)):
                qkv = _qkv_proj(xb, w_qkv, g)
                o, l = _dilated_group(qkv, rel_bias[:, g * DIL_HEADS:(g + 1) * DIL_HEADS], g, bn, s)
                os_.append(o)
                ls_.append(l)
            xf, xb, xp = _attn_out(os_, ls_, bf(dil_w_out[ib]), xf, g1, b1)
            ib += 1
        else:
            v = _conv_mix(xb, bf(conv_w_in[ic]), conv_w[ic], s)
            xf, xb, xp = _proj_ln(v, bf(conv_w_out[ic]), xf, g1, b1)
            ic += 1
        xf, xb = _moe_layer(xf, xb, xp, router_w[i], router_bias[i], expert_w_gu, expert_w_down, i,
                            bf(shared_w_gu[i]), bf(shared_w_down[i]), g2, b2)
    return xf.reshape(bn, s, d)
```

```python
import functools
import math

import jax
import jax.numpy as jnp
from jax import lax
from jax.experimental import pallas as pl
from jax.experimental.pallas import tpu as pltpu

F32 = jnp.float32
BF16 = jnp.bfloat16

DEPTH = 4
D_MODEL = 2048
POOL_WINDOWS = (2, 4, 8, 16)
POOL_GROUP_DIM = D_MODEL // len(POOL_WINDOWS)
POOL_HALO = 16
DIL_CONFIGS = ((128, 1), (512, 4), (2048, 16))
HEAD_DIM = 64
DIL_HEADS = 8
DIL_WIDTH = DIL_HEADS * HEAD_DIM
DIL_BLOCK = 128
N_BUCKETS = 32
MAX_DISTANCE = 2048
CONV_WIDTH = 3
CONV_HALO = 8
N_EXPERTS = 64
TOP_K = 8
N_EXPERT_GROUPS = 8
GROUP_SIZE = N_EXPERTS // N_EXPERT_GROUPS
TOPK_GROUPS = 4
EXPERT_DIM = 128
SHARED_DIM = 256
ROUTED_SCALE = 2.5
ROW_BLOCK = 256
ALPHA = (2.0 * DEPTH) ** 0.25
LN_EPS = 1e-5
NEG = -1e30

LANES = 128
HIGH_HALF = -65536
SLAB = D_MODEL // 2 // LANES + 1
UP_CHUNKS = 8
UP_SHARE = 160
VMEM_LIMIT = 52 * 1024 * 1024

_NT = (((1,), (1,)), ((), ()))


def _dot(a, b):
    return jnp.dot(a, b, preferred_element_type=F32)


def _params(*sem):
    return pltpu.CompilerParams(dimension_semantics=sem, vmem_limit_bytes=VMEM_LIMIT)


def _layer_norm(z, g, b):
    mu = jnp.mean(z, axis=-1, keepdims=True)
    zc = z - mu
    var = jnp.mean(zc * zc, axis=-1, keepdims=True)
    return zc * lax.rsqrt(var + LN_EPS) * g + b


def _silu(v):
    return v / (1.0 + jnp.exp(-v))


def _pack_pair(lo, hi):
    lo = lax.bitcast_convert_type(lo.astype(BF16).astype(F32), jnp.int32)
    hi = lax.bitcast_convert_type(hi.astype(BF16).astype(F32), jnp.int32)
    return (hi & HIGH_HALF) | lax.shift_right_logical(lo, 16)


def _pack_rows(y):
    half = D_MODEL // 2
    return _pack_pair(y[:, :half], y[:, half:])


def _unpack_rows(p):
    lo = lax.bitcast_convert_type(lax.shift_left(p, 16), F32)
    hi = lax.bitcast_convert_type(p & HIGH_HALF, F32)
    return jnp.concatenate([lo, hi], axis=1).astype(BF16)


def _store_ln_outputs(y, of_ref, ob_ref, op_ref):
    of_ref[...] = y
    ob_ref[...] = y.astype(BF16)
    op_ref[...] = _pack_rows(y)


def _ln_out_specs(t, d, tm):
    row = lambda i: (i, 0)
    specs = [pl.BlockSpec((tm, d), row), pl.BlockSpec((tm, d), row), pl.BlockSpec((tm, d // 2), row)]
    shapes = [jax.ShapeDtypeStruct((t, d), F32), jax.ShapeDtypeStruct((t, d), BF16),
              jax.ShapeDtypeStruct((t, d // 2), jnp.int32)]
    return specs, shapes


def _proj_ln_kernel(a_ref, w_ref, x_ref, g_ref, b_ref, of_ref, ob_ref, op_ref):
    z = ALPHA * x_ref[...] + _dot(a_ref[...], w_ref[...])
    _store_ln_outputs(_layer_norm(z, g_ref[...], b_ref[...]), of_ref, ob_ref, op_ref)


def _proj_ln(a, w, x, g, b, tm=256):
    t, k = a.shape
    d = w.shape[1]
    row = lambda i: (i, 0)
    const = lambda i: (0, 0)
    out_specs, out_shape = _ln_out_specs(t, d, tm)
    return pl.pallas_call(
        _proj_ln_kernel,
        grid=(t // tm,),
        in_specs=[pl.BlockSpec((tm, k), row), pl.BlockSpec((k, d), const),
                  pl.BlockSpec((tm, d), row), pl.BlockSpec((1, d), const),
                  pl.BlockSpec((1, d), const)],
        out_specs=out_specs,
        out_shape=out_shape,
        compiler_params=_params("arbitrary"),
        name="proj_ln",
    )(a, w, x, g, b)


def _pool_kernel(x_ref, win_ref, wg_ref, sc_ref, o_ref, carry_ref, *, tm, tiles_per_seq):
    i = pl.program_id(0)
    seq_tile = i % tiles_per_seq

    @pl.when(seq_tile == 0)
    def _():
        carry_ref[...] = jnp.zeros_like(carry_ref)

    xb = x_ref[...]
    pos = lax.broadcasted_iota(jnp.int32, (tm, 1), 0) + seq_tile * tm
    gd = POOL_GROUP_DIM
    for g, w in enumerate(POOL_WINDOWS):
        cols = slice(g * gd, (g + 1) * gd)
        u = _dot(xb, win_ref[:, cols])
        s = jnp.concatenate([carry_ref[g], u], axis=0)
        sh = 1
        while sh < w:
            s = s + pltpu.roll(s, sh, axis=0)
            sh *= 2
        cnt = jnp.minimum(pos + 1, w).astype(F32)
        pooled = s[POOL_HALO:, :] / cnt - u
        carry_ref[g] = u[tm - POOL_HALO:, :]
        mixed = _dot(pooled.astype(BF16), wg_ref[g]) * sc_ref[:, cols]
        o_ref[:, cols] = mixed.astype(BF16)


def _pool_mix(xb, w_in, w_group, scale, seq_len, tm=256):
    t, d = xb.shape
    gd = POOL_GROUP_DIM
    ng = len(POOL_WINDOWS)
    return pl.pallas_call(
        functools.partial(_pool_kernel, tm=tm, tiles_per_seq=seq_len // tm),
        grid=(t // tm,),
        in_specs=[pl.BlockSpec((tm, d), lambda i: (i, 0)),
                  pl.BlockSpec((d, d), lambda i: (0, 0)),
                  pl.BlockSpec((ng, gd, gd), lambda i: (0, 0, 0)),
                  pl.BlockSpec((1, d), lambda i: (0, 0))],
        out_specs=pl.BlockSpec((tm, d), lambda i: (i, 0)),
        out_shape=jax.ShapeDtypeStruct((t, d), BF16),
        scratch_shapes=[pltpu.VMEM((ng, POOL_HALO, gd), F32)],
        compiler_params=_params("arbitrary"),
        name="pool_mix",
    )(xb, w_in, w_group, scale)


def _conv_kernel(x_ref, wb_ref, wc_ref, wh_ref, cw_ref, o_ref, carry_ref, *, tm, tiles_per_seq):
    i = pl.program_id(1)

    @pl.when(i % tiles_per_seq == 0)
    def _():
        carry_ref[...] = jnp.zeros_like(carry_ref)

    xb = x_ref[...]
    b_gate = _dot(xb, wb_ref[...])
    u = _dot(xb, wc_ref[...]) * _dot(xb, wh_ref[...])
    ext = jnp.concatenate([carry_ref[...], u], axis=0)
    y = cw_ref[CONV_WIDTH - 1:CONV_WIDTH, :] * u
    for back in range(1, CONV_WIDTH):
        tap = CONV_WIDTH - 1 - back
        y = y + cw_ref[tap:tap + 1, :] * pltpu.roll(ext, back, axis=0)[CONV_HALO:, :]
    carry_ref[...] = u[tm - CONV_HALO:, :]
    o_ref[...] = (b_gate * y).astype(BF16)


def _conv_mix(xb, w_in, conv_w, seq_len, tm=512, tn=512):
    t, d = xb.shape
    nj = d // tn
    return pl.pallas_call(
        functools.partial(_conv_kernel, tm=tm, tiles_per_seq=seq_len // tm),
        grid=(nj, t // tm),
        in_specs=[pl.BlockSpec((tm, d), lambda j, i: (i, 0)),
                  pl.BlockSpec((d, tn), lambda j, i: (0, j)),
                  pl.BlockSpec((d, tn), lambda j, i: (0, nj + j)),
                  pl.BlockSpec((d, tn), lambda j, i: (0, 2 * nj + j)),
                  pl.BlockSpec((CONV_WIDTH, tn), lambda j, i: (0, j))],
        out_specs=pl.BlockSpec((tm, tn), lambda j, i: (i, j)),
        out_shape=jax.ShapeDtypeStruct((t, d), BF16),
        scratch_shapes=[pltpu.VMEM((CONV_HALO, tn), F32)],
        compiler_params=_params("parallel", "arbitrary"),
        name="conv_mix",
    )(xb, w_in, w_in, w_in, conv_w)


def _t5_bucket(dist):
    max_exact = N_BUCKETS // 2
    is_small = dist < max_exact
    distf = jnp.maximum(dist, 1).astype(F32)
    large = max_exact + (jnp.log(distf / max_exact) / math.log(MAX_DISTANCE / max_exact)
                         * (N_BUCKETS - max_exact)).astype(jnp.int32)
    large = jnp.minimum(large, N_BUCKETS - 1)
    return jnp.where(is_small, dist, large)


def _attn_bias(bias_tab, window, dil, has_prev):
    steps = window // dil
    q_loc = jnp.arange(DIL_BLOCK)[:, None]
    k_loc = jnp.arange(2 * DIL_BLOCK)[None, :]
    rel = q_loc + DIL_BLOCK - k_loc
    band = (rel >= 0) & (rel <= steps)
    bucket = _t5_bucket(jnp.maximum(rel, 0) * dil)
    tab = bias_tab.astype(F32).T[:, :, None, None]
    bias = sum(jnp.where(bucket == b, tab[:, b], 0.0) for b in range(N_BUCKETS))
    bias = jnp.where(band[None], bias, NEG)
    return bias if has_prev else bias[:, :, DIL_BLOCK:]


def _attn_kernel(*refs, has_prev):
    if has_prev:
        q_ref, kc_ref, kp_ref, vc_ref, vp_ref, bias_ref, o_ref, l_ref = refs
    else:
        q_ref, kc_ref, vc_ref, bias_ref, o_ref, l_ref = refs
    n = pl.program_id(1)
    q = q_ref[0, 0]
    kc = kc_ref[0, 0]
    vc = vc_ref[0, 0]
    if has_prev:
        kp = kp_ref[0, 0]
        vp = vp_ref[0, 0]
        col = lax.broadcasted_iota(jnp.int32, (DIL_BLOCK, 2 * DIL_BLOCK), 1)
        keep = (col >= DIL_BLOCK) | (n > 0)
    outs, lses = [], []
    for h in range(DIL_HEADS):
        sl = slice(h * HEAD_DIM, (h + 1) * HEAD_DIM)
        if has_prev:
            kh = jnp.concatenate([kp[:, sl], kc[:, sl]], axis=0)
            vh = jnp.concatenate([vp[:, sl], vc[:, sl]], axis=0)
        else:
            kh, vh = kc[:, sl], vc[:, sl]
        s = lax.dot_general(q[:, sl], kh, _NT, preferred_element_type=F32) * (HEAD_DIM ** -0.5) + bias_ref[h]
        if has_prev:
            s = jnp.where(keep, s, NEG)
        m = jnp.max(s, axis=-1, keepdims=True)
        p = jnp.exp(s - m)
        l = jnp.sum(p, axis=-1, keepdims=True)
        outs.append(_dot(p.astype(BF16), vh) / l)
        lses.append(jnp.broadcast_to(m + jnp.log(l), (DIL_BLOCK, HEAD_DIM)))
    o_ref[0, 0] = jnp.concatenate(outs, axis=1)
    l_ref[0, 0] = jnp.concatenate(lses, axis=1)


def _subseq_tile(dil, tm):
    span = DIL_BLOCK * dil
    if tm >= span:
        block = (tm // span, dil, DIL_BLOCK)
        imap = lambda i: (i, 0, 0, 0)
    else:
        per_span = span // tm
        block = (1, dil, tm // dil)
        imap = lambda i: (i // per_span, 0, i % per_span, 0)
    runs = [(sp, r, sp * span + r) for sp in range(block[0]) for r in range(dil)]
    return block, imap, runs


def _qkv_kernel(x_ref, w_ref, o_ref, *, dil):
    x = x_ref[...]
    if dil > 1:
        tm, n = x.shape[0], o_ref.shape[2]
        out_row = lax.broadcasted_iota(jnp.int32, (tm, tm), 0)
        in_row = lax.broadcasted_iota(jnp.int32, (tm, tm), 1)
        pick = in_row == (out_row % n) * dil + out_row // n
        x = _dot(pick.astype(BF16), x).astype(BF16)
    o_ref[...] = _dot(x, w_ref[...]).astype(o_ref.dtype).reshape(o_ref.shape)


def _qkv_proj(xb, w_qkv, g, tm=512):
    t, d = xb.shape
    _, dil = DIL_CONFIGS[g]
    c = 3 * DIL_WIDTH
    block, imap, _ = _subseq_tile(dil, tm)
    assert dil == 1 or block[0] == 1, "the in-kernel row reorder handles one span (or part of one) per tile"
    return pl.pallas_call(
        functools.partial(_qkv_kernel, dil=dil),
        grid=(t // tm,),
        in_specs=[pl.BlockSpec((tm, d), lambda i: (i, 0)),
                  pl.BlockSpec((d, c), lambda i: (0, g))],
        out_specs=pl.BlockSpec(block + (c,), imap),
        out_shape=jax.ShapeDtypeStruct((t // (DIL_BLOCK * dil), dil, DIL_BLOCK, c), BF16),
        compiler_params=_params("arbitrary"),
        name=f"qkv_proj_g{g}",
    )(xb, w_qkv)


def _dilated_group(qkv, bias_tab, g, batch, seq_len):
    window, dil = DIL_CONFIGS[g]
    nb = seq_len // (DIL_BLOCK * dil)
    has_prev = nb > 1
    blk = (1, 1, DIL_BLOCK, DIL_WIDTH)

    def cur(kind):
        return pl.BlockSpec(blk, lambda b, n, r: (b * nb + n, r, 0, kind))

    def prev(kind):
        return pl.BlockSpec(blk, lambda b, n, r: (b * nb + jnp.maximum(n - 1, 0), r, 0, kind))

    bias = _attn_bias(bias_tab, window, dil, has_prev)
    bias_spec = pl.BlockSpec(bias.shape, lambda b, n, r: (0, 0, 0))
    if has_prev:
        in_specs = [cur(0), cur(1), prev(1), cur(2), prev(2), bias_spec]
        args = (qkv,) * 5 + (bias,)
    else:
        in_specs = [cur(0), cur(1), cur(2), bias_spec]
        args = (qkv,) * 3 + (bias,)
    out_spec = pl.BlockSpec(blk, lambda b, n, r: (b * nb + n, r, 0, 0))
    out_sds = jax.ShapeDtypeStruct(qkv.shape[:3] + (DIL_WIDTH,), F32)
    return pl.pallas_call(
        functools.partial(_attn_kernel, has_prev=has_prev),
        grid=(batch, nb, dil),
        in_specs=in_specs,
        out_specs=[out_spec, out_spec],
        out_shape=[out_sds, out_sds],
        compiler_params=_params("parallel", "arbitrary", "arbitrary"),
        name=f"dilated_attn_g{g}",
    )(*args)


def _attn_out_kernel(*refs, tm, tiles):
    n_g = len(tiles)
    o_refs, l_refs = refs[:n_g], refs[n_g:2 * n_g]
    w_ref, x_ref, g_ref, b_ref, of_ref, ob_ref, op_ref = refs[2 * n_g:2 * n_g + 7]
    scratch = list(refs[2 * n_g + 7:])

    def token_order(ref, dil, runs):
        if dil == 1:
            return ref[...].reshape(tm, ref.shape[-1])
        buf = scratch.pop(0)
        n = ref.shape[2]
        for sp, r, first in runs:
            rows = ref[sp, r]
            for c in range(buf.shape[0]):
                buf[c, pl.ds(first, n, stride=dil), :] = rows[:, c * LANES:(c + 1) * LANES]
        return jnp.concatenate([buf[c] for c in range(buf.shape[0])], axis=1)

    os_ = [token_order(ref, dil, runs) for ref, (dil, runs) in zip(o_refs, tiles)]
    ls_ = [token_order(ref, dil, runs) for ref, (dil, runs) in zip(l_refs, tiles)]
    m = functools.reduce(jnp.maximum, ls_)
    es = [jnp.exp(a - m) for a in ls_]
    o = sum(e * ov for e, ov in zip(es, os_)) / sum(es)
    z = ALPHA * x_ref[...] + _dot(o.astype(BF16), w_ref[...])
    _store_ln_outputs(_layer_norm(z, g_ref[...], b_ref[...]), of_ref, ob_ref, op_ref)


def _attn_out(os_, ls_, w, x, g, b, tm=256):
    t, d = x.shape
    k = w.shape[0]
    row = lambda i: (i, 0)
    const = lambda i: (0, 0)
    specs, tiles = [], []
    for _, dil in DIL_CONFIGS:
        block, imap, runs = _subseq_tile(dil, tm)
        specs.append(pl.BlockSpec(block + (k,), imap))
        tiles.append((dil, runs))
    n_scratch = 2 * sum(1 for dil, _ in tiles if dil > 1)
    out_specs, out_shape = _ln_out_specs(t, d, tm)
    return pl.pallas_call(
        functools.partial(_attn_out_kernel, tm=tm, tiles=tiles),
        grid=(t // tm,),
        in_specs=specs * 2 + [pl.BlockSpec((k, d), const), pl.BlockSpec((tm, d), row),
                              pl.BlockSpec((1, d), const), pl.BlockSpec((1, d), const)],
        out_specs=out_specs,
        out_shape=out_shape,
        scratch_shapes=[pltpu.VMEM((k // LANES, tm, LANES), F32)] * n_scratch,
        compiler_params=_params("arbitrary"),
        name="attn_out_ln",
    )(*os_, *ls_, w, x, g, b)


def _router_kernel(x_ref, wt_ref, b_ref, eidx_ref, gate_ref, pos_ref, cnt_ref, base_ref, *, tm):
    @pl.when(pl.program_id(0) == 0)
    def _():
        base_ref[...] = jnp.zeros_like(base_ref)

    x = x_ref[...]
    w = wt_ref[...]
    xh = x.astype(BF16)
    xl = (x - xh.astype(F32)).astype(BF16)
    wh = w.astype(BF16)
    wl = (w - wh.astype(F32)).astype(BF16)
    dg = functools.partial(lax.dot_general, dimension_numbers=_NT, preferred_element_type=F32)
    logits = dg(wh, xh) + (dg(wh, xl) + dg(wl, xh))
    scores = 1.0 / (1.0 + jnp.exp(-logits))
    choice = scores + b_ref[...]

    iota_g = lax.broadcasted_iota(jnp.int32, (GROUP_SIZE, tm), 0)
    gs = []
    for gi in range(N_EXPERT_GROUPS):
        cg = choice[gi * GROUP_SIZE:(gi + 1) * GROUP_SIZE, :]
        t1 = jnp.max(cg, axis=0, keepdims=True)
        i1 = jnp.min(jnp.where(cg == t1, iota_g, GROUP_SIZE), axis=0, keepdims=True)
        t2 = jnp.max(jnp.where(iota_g == i1, -jnp.inf, cg), axis=0, keepdims=True)
        gs.append(t1 + t2)
    masked = []
    for gi in range(N_EXPERT_GROUPS):
        rank = jnp.zeros((1, tm), jnp.int32)
        for gj in range(N_EXPERT_GROUPS):
            if gj != gi:
                beats = (gs[gj] >= gs[gi]) if gj < gi else (gs[gj] > gs[gi])
                rank = rank + beats.astype(jnp.int32)
        masked.append(jnp.where(rank < TOPK_GROUPS, choice[gi * GROUP_SIZE:(gi + 1) * GROUP_SIZE, :], -jnp.inf))
    c = jnp.concatenate(masked, axis=0)

    iota_e = lax.broadcasted_iota(jnp.int32, (N_EXPERTS, tm), 0)
    picks, gates = [], []
    sel = jnp.zeros((N_EXPERTS, tm), F32)
    for _ in range(TOP_K):
        m = jnp.max(c, axis=0, keepdims=True)
        idx = jnp.min(jnp.where(c == m, iota_e, N_EXPERTS), axis=0, keepdims=True)
        hit = iota_e == idx
        picks.append(hit)
        gates.append(jnp.sum(jnp.where(hit, scores, 0.0), axis=0, keepdims=True))
        sel = jnp.where(hit, 1.0, sel)
        c = jnp.where(hit, -jnp.inf, c)
    gsum = gates[0]
    for gk in gates[1:]:
        gsum = gsum + gk

    before = (lax.broadcasted_iota(jnp.int32, (tm, tm), 0) < lax.broadcasted_iota(jnp.int32, (tm, tm), 1))
    rank_in_tile = _dot(sel.astype(BF16), before.astype(BF16))
    base = base_ref[...]
    posfull = base[:, 0:1] + rank_in_tile
    efull = iota_e.astype(F32)
    for k in range(TOP_K):
        hit = picks[k]
        eidx_ref[k:k + 1, :] = jnp.sum(jnp.where(hit, efull, 0.0), axis=0, keepdims=True).astype(jnp.int32)
        pos_ref[k:k + 1, :] = jnp.sum(jnp.where(hit, posfull, 0.0), axis=0, keepdims=True).astype(jnp.int32)
        gate_ref[k:k + 1, :] = gates[k] / gsum * ROUTED_SCALE
    new_base = base + jnp.sum(sel, axis=1, keepdims=True)
    base_ref[...] = new_base
    cnt_ref[...] = new_base


def _route(x, router_wt, router_b, tm=512):
    t, d = x.shape
    tok = pl.BlockSpec((TOP_K, tm), lambda i: (0, i))
    cnt = pl.BlockSpec((N_EXPERTS, LANES), lambda i: (0, 0))
    return pl.pallas_call(
        functools.partial(_router_kernel, tm=tm),
        grid=(t // tm,),
        in_specs=[pl.BlockSpec((tm, d), lambda i: (i, 0)),
                  pl.BlockSpec((N_EXPERTS, d), lambda i: (0, 0)),
                  pl.BlockSpec((N_EXPERTS, 1), lambda i: (0, 0))],
        out_specs=[tok, tok, tok, cnt],
        out_shape=[jax.ShapeDtypeStruct((TOP_K, t), jnp.int32), jax.ShapeDtypeStruct((TOP_K, t), F32),
                   jax.ShapeDtypeStruct((TOP_K, t), jnp.int32), jax.ShapeDtypeStruct((N_EXPERTS, LANES), F32)],
        scratch_shapes=[pltpu.VMEM((N_EXPERTS, LANES), F32)],
        compiler_params=_params("arbitrary"),
        name="router",
    )(x, router_wt, router_b)


def _row(ref, r):
    return ref.at[pl.ds(r, 1)]


def _slab(ref, r):
    return ref.at[pl.ds(r * SLAB, SLAB)]


def _dest_kernel(pstart_ref, eidx_ref, pos_ref, dest_ref):
    e = eidx_ref[...]
    acc = pos_ref[...]
    for k in range(N_EXPERTS):
        acc = acc + jnp.where(e == k, pstart_ref[k], 0)
    dest_ref[...] = acc


def _dest_rows(pstart, eidx, pos):
    full = pl.BlockSpec(eidx.shape, lambda i, ps: (0, 0))
    return pl.pallas_call(
        _dest_kernel,
        grid_spec=pltpu.PrefetchScalarGridSpec(num_scalar_prefetch=1, grid=(1,), in_specs=[full, full],
                                               out_specs=full),
        out_shape=jax.ShapeDtypeStruct(eidx.shape, jnp.int32),
        compiler_params=_params("arbitrary"),
        name="dest_rows",
    )(pstart, eidx, pos)


def _dispatch_kernel(pad_lo_ref, pad_hi_ref, dest_ref, x_ref, gate_ref, xs_hbm, src_ref, zero_ref, sem, pad_sem,
                     *, tm, n_blk, n_tok, n_tiles):
    i = pl.program_id(0)

    def zero_fill(wait):
        def each(lo, hi, copy):
            def body(r, c):
                copy(r).wait() if wait else copy(r).start()
                return c

            lax.fori_loop(lo, hi, body, 0)

        def per_expert(e, carry):
            each(pad_lo_ref[e], pad_hi_ref[e],
                 lambda r: pltpu.make_async_copy(_slab(zero_ref, 0), _slab(xs_hbm, r), pad_sem))
            return carry

        blk_rows = ROW_BLOCK * SLAB
        lax.fori_loop(0, N_EXPERTS, per_expert, 0)
        each(pad_hi_ref[N_EXPERTS - 1] // ROW_BLOCK, n_blk,
             lambda blk: pltpu.make_async_copy(
                 zero_ref, xs_hbm.at[pl.ds(pl.multiple_of(blk * blk_rows, blk_rows), blk_rows)], pad_sem))

    @pl.when(i == 0)
    def _():
        zero_ref[...] = jnp.zeros_like(zero_ref)
        zero_fill(wait=False)

    x = x_ref[...]
    gate_bits = lax.bitcast_convert_type(gate_ref[...], jnp.int32)
    lane = lax.broadcasted_iota(jnp.int32, (tm, LANES), 1)
    token = lax.broadcasted_iota(jnp.int32, (tm, LANES), 0) + i * tm

    def retire(par):
        for k in range(TOP_K):
            pltpu.make_async_copy(src_ref.at[par, k], xs_hbm.at[pl.ds(0, tm * SLAB)], sem.at[par]).wait()

    for par in range(2):
        @pl.when(i % 2 == par)
        def _(par=par):
            for k in range(TOP_K):
                for c in range(SLAB - 1):
                    src_ref[par, k, pl.ds(c, tm, stride=SLAB), :] = x[:, c * LANES:(c + 1) * LANES]
                src_ref[par, k, pl.ds(SLAB - 1, tm, stride=SLAB), :] = jnp.where(
                    lane == 0, token + k * n_tok, jnp.where(lane == 1, gate_bits[:, k:k + 1], 0))

            def per_token(j, carry):
                for k in range(TOP_K):
                    pltpu.make_async_copy(_slab(src_ref.at[par, k], j),
                                          _slab(xs_hbm, dest_ref[0, 0, j * TOP_K + k]), sem.at[par]).start()
                return carry

            lax.fori_loop(0, tm, per_token, 0)

            @pl.when(i >= 1)
            def _():
                retire(1 - par)

            @pl.when(i == n_tiles - 1)
            def _():
                retire(par)

    @pl.when(i == 0)
    def _():
        zero_fill(wait=True)


def _dispatch(xp, dest_tok, gate_tok, pad_lo, pad_hi, n_rows, tm=256):
    t, half = xp.shape
    assert half == (SLAB - 1) * LANES
    n_tiles = t // tm
    dest3 = dest_tok.reshape(n_tiles, 1, tm * TOP_K)
    grid_spec = pltpu.PrefetchScalarGridSpec(
        num_scalar_prefetch=2,
        grid=(n_tiles,),
        in_specs=[pl.BlockSpec((1, 1, tm * TOP_K), lambda i, lo, hi: (i, 0, 0), memory_space=pltpu.SMEM),
                  pl.BlockSpec((tm, half), lambda i, lo, hi: (i, 0)),
                  pl.BlockSpec((tm, TOP_K), lambda i, lo, hi: (i, 0))],
        out_specs=pl.BlockSpec(memory_space=pl.ANY),
        scratch_shapes=[pltpu.VMEM((2, TOP_K, tm * SLAB, LANES), jnp.int32),
                        pltpu.VMEM((ROW_BLOCK * SLAB, LANES), jnp.int32),
                        pltpu.SemaphoreType.DMA((2,)), pltpu.SemaphoreType.DMA(())],
    )
    return pl.pallas_call(
        functools.partial(_dispatch_kernel, tm=tm, n_blk=n_rows // ROW_BLOCK, n_tok=t, n_tiles=n_tiles),
        grid_spec=grid_spec,
        out_shape=jax.ShapeDtypeStruct((n_rows * SLAB, LANES), jnp.int32),
        compiler_params=_params("arbitrary"),
        name="dispatch",
    )(pad_lo, pad_hi, dest3, xp, gate_tok)


def _expert_kernel(blk_e_ref, nused_ref, nvalid_ref, xs_ref, wgu_ref, wd_ref, out_hbm,
                   wgu_bf, wd_bf, ybuf, idv, ids_smem, sem, id_sem, *, n_blk, n_slots):
    i = pl.program_id(0)
    nused = nused_ref[0]
    s = i % 2
    half = ybuf.shape[2]

    def rows_copy(parity):
        return pltpu.make_async_copy(ybuf.at[parity], out_hbm.at[pl.ds(0, ROW_BLOCK)], sem.at[parity])

    def spill_init(parity):
        return pltpu.make_async_copy(ybuf.at[parity], out_hbm.at[pl.ds(n_slots + parity * ROW_BLOCK, ROW_BLOCK)],
                                     sem.at[parity])

    def compute(par, retire_first, scatter_par=None):
        @pl.when((i == 0) | (blk_e_ref[i] != blk_e_ref[jnp.maximum(i - 1, 0)]))
        def _():
            wgu_bf[...] = wgu_ref[0, 0].astype(BF16)
            wd_bf[...] = wd_ref[0, 0].astype(BF16)

        if scatter_par is not None:
            ids_ready(scatter_par)

        tiles = [xs_ref[pl.ds(c, ROW_BLOCK, stride=SLAB), :] for c in range(SLAB)]
        meta = tiles[SLAB - 1]
        row = lax.broadcasted_iota(jnp.int32, meta.shape, 0)
        dst = jnp.where(row < nvalid_ref[i], meta, n_slots + par * ROW_BLOCK + row)
        idv[par] = jnp.transpose(dst)[0:idv.shape[1], :]
        pltpu.make_async_copy(idv.at[par], ids_smem.at[par], id_sem.at[par]).start()

        gate = lax.bitcast_convert_type(meta[:, 1:2], F32)
        xs = _unpack_rows(jnp.concatenate(tiles[:SLAB - 1], axis=1))
        up_rows = UP_SHARE // UP_CHUNKS
        d = xs.shape[1]
        step = d // UP_CHUNKS
        hgu = None
        for c in range(UP_CHUNKS):
            part = _dot(xs[:, c * step:(c + 1) * step], wgu_bf[c * step:(c + 1) * step, :])
            hgu = part if hgu is None else hgu + part
            if scatter_par is not None:
                scatter(scatter_par, c * up_rows, up_rows)
        h = (_silu(hgu[:, :EXPERT_DIM]) * hgu[:, EXPERT_DIM:]).astype(BF16)
        if scatter_par is not None:
            scatter(scatter_par, UP_SHARE, ROW_BLOCK - UP_SHARE)
        y = _pack_rows(_dot(h, wd_bf[...]) * gate)
        if retire_first:
            rows_copy(par).wait()
        ybuf[par] = y

    def ids_ready(par):
        pltpu.make_async_copy(idv.at[par], ids_smem.at[par], id_sem.at[par]).wait()

    def scatter(par, first, count):
        for r in range(first, first + count):
            pltpu.make_async_copy(ybuf.at[par, pl.ds(r, 1)], _row(out_hbm, ids_smem[par, 0, r]),
                                  sem.at[par]).start()

    @pl.when(i == 0)
    def _():
        ybuf[...] = jnp.zeros_like(ybuf)
        spill_init(0).start()
        spill_init(0).wait()
        spill_init(1).start()
        compute(0, retire_first=False)

    for par in range(2):
        @pl.when((i >= 1) & (i < nused) & (s == par))
        def _(par=par):
            compute(par, retire_first=True, scatter_par=1 - par)

        @pl.when((i >= 1) & (i == nused) & (s == par))
        def _(par=par):
            ids_ready(1 - par)
            scatter(1 - par, 0, ROW_BLOCK)

    @pl.when((i >= nused) & (i >= 2) & (i - 2 < nused))
    def _():
        rows_copy(s).wait()

    @pl.when((i == n_blk - 1) & (i - 1 < nused))
    def _():
        rows_copy(1 - s).wait()


def _experts(xs, blk_e, nused, nvalid, w_gu, w_down, layer, n_slots):
    half = (SLAB - 1) * LANES
    n_blk = xs.shape[0] // (ROW_BLOCK * SLAB)
    d, f2 = w_gu.shape[2], w_gu.shape[3]
    blk = lambda i, be, nu, nv: (jnp.minimum(i, nu[0] - 1), 0)
    wblk = lambda i, be, nu, nv: (layer, be[jnp.minimum(i, nu[0] - 1)], 0, 0)
    grid_spec = pltpu.PrefetchScalarGridSpec(
        num_scalar_prefetch=3,
        grid=(n_blk,),
        in_specs=[pl.BlockSpec((ROW_BLOCK * SLAB, LANES), blk),
                  pl.BlockSpec((1, 1, d, f2), wblk),
                  pl.BlockSpec((1, 1, f2 // 2, d), wblk)],
        out_specs=pl.BlockSpec(memory_space=pl.ANY),
        scratch_shapes=[pltpu.VMEM((d, f2), BF16), pltpu.VMEM((f2 // 2, d), BF16),
                        pltpu.VMEM((2, ROW_BLOCK, half), jnp.int32),
                        pltpu.VMEM((2, 8, ROW_BLOCK), jnp.int32),
                        pltpu.SMEM((2, 8, ROW_BLOCK), jnp.int32),
                        pltpu.SemaphoreType.DMA((2,)), pltpu.SemaphoreType.DMA((2,))],
    )
    return pl.pallas_call(
        functools.partial(_expert_kernel, n_blk=n_blk, n_slots=n_slots),
        grid_spec=grid_spec,
        out_shape=jax.ShapeDtypeStruct((n_slots + 2 * ROW_BLOCK, half), jnp.int32),
        compiler_params=_params("arbitrary"),
        name="experts",
    )(blk_e, nused, nvalid, xs, w_gu, w_down)


def _combine_kernel(*refs):
    y_refs = refs[:TOP_K]
    xf_ref, xb_ref, sgu_ref, sd_ref, g_ref, b_ref, of_ref, ob_ref = refs[TOP_K:]
    lo = hi = None
    for y_ref in y_refs:
        p = y_ref[...]
        lo_k = lax.bitcast_convert_type(lax.shift_left(p, 16), F32)
        hi_k = lax.bitcast_convert_type(p & HIGH_HALF, F32)
        lo = lo_k if lo is None else lo + lo_k
        hi = hi_k if hi is None else hi + hi_k
    routed = jnp.concatenate([lo, hi], axis=1)

    sgu = _dot(xb_ref[...], sgu_ref[...])
    sh = _silu(sgu[:, :SHARED_DIM]) * sgu[:, SHARED_DIM:]
    shared = _dot(sh.astype(BF16), sd_ref[...])
    z = ALPHA * xf_ref[...] + (routed + shared)
    y = _layer_norm(z, g_ref[...], b_ref[...])
    of_ref[...] = y
    ob_ref[...] = y.astype(BF16)


def _combine(y, xf, xb, sw_gu, sw_down, g, b, tm=256):
    t, d = xf.shape
    n_tiles = t // tm
    row = lambda i: (i, 0)
    const = lambda i: (0, 0)
    y_specs = [pl.BlockSpec((tm, d // 2), functools.partial(lambda i, k: (k * n_tiles + i, 0), k=k))
               for k in range(TOP_K)]
    return pl.pallas_call(
        _combine_kernel,
        grid=(n_tiles,),
        in_specs=y_specs + [pl.BlockSpec((tm, d), row), pl.BlockSpec((tm, d), row),
                            pl.BlockSpec(sw_gu.shape, const), pl.BlockSpec(sw_down.shape, const),
                            pl.BlockSpec((1, d), const), pl.BlockSpec((1, d), const)],
        out_specs=[pl.BlockSpec((tm, d), row), pl.BlockSpec((tm, d), row)],
        out_shape=[jax.ShapeDtypeStruct((t, d), F32), jax.ShapeDtypeStruct((t, d), BF16)],
        compiler_params=_params("arbitrary"),
        name="combine_ln",
    )(*([y] * TOP_K), xf, xb, sw_gu, sw_down, g, b)


def _moe_layer(xf, xb, xp, router_w, router_b, w_gu, w_down, layer, sw_gu, sw_down, g, b):
    t, d = xf.shape
    eidx, gate, pos, cnt = _route(xf, router_w.T, router_b.reshape(N_EXPERTS, 1))

    sizes = cnt[:, 0].astype(jnp.int32)
    padded = (sizes + ROW_BLOCK - 1) // ROW_BLOCK * ROW_BLOCK
    pend = jnp.cumsum(padded)
    pstart = pend - padded
    tk = t * TOP_K
    n_rows = -(-tk // ROW_BLOCK) * ROW_BLOCK + N_EXPERTS * ROW_BLOCK
    blk_start = jnp.arange(n_rows // ROW_BLOCK, dtype=jnp.int32) * ROW_BLOCK
    blk_e = jnp.minimum(jnp.sum(pend[None, :] <= blk_start[:, None], axis=1), N_EXPERTS - 1).astype(jnp.int32)
    nused = (pend[-1:] // ROW_BLOCK).astype(jnp.int32)

    real_end = pstart + sizes
    nvalid = jnp.clip(real_end[blk_e] - blk_start, 0, ROW_BLOCK).astype(jnp.int32)

    dest_tok = _dest_rows(pstart, eidx, pos).T
    xs = _dispatch(xp, dest_tok, gate.T, real_end, pend, n_rows)
    y = _experts(xs, blk_e, nused, nvalid, w_gu, w_down, layer, tk)
    return _combine(y, xf, xb, sw_gu, sw_down, g, b)


def kernel(x, rel_bias, pool_w_in, pool_w_group, pool_scale, pool_w_out, dil_w_qkv, dil_w_out,
           conv_w_in, conv_w, conv_w_out, ln_gain, ln_bias, router_w, router_bias,
           expert_w_gu, expert_w_down, shared_w_gu, shared_w_down):
    bn, s, d = x.shape
    t = bn * s
    xf = x.reshape(t, d)
    xb = xf.astype(BF16)
    bf = lambda a: a.astype(BF16)
    ia = ib = ic = 0
    for i in range(DEPTH):
        kind = i % 3
        g1, b1 = ln_gain[i, 0].reshape(1, d), ln_bias[i, 0].reshape(1, d)
        g2, b2 = ln_gain[i, 1].reshape(1, d), ln_bias[i, 1].reshape(1, d)
        if kind == 0:
            mixed = _pool_mix(xb, bf(pool_w_in[ia]), bf(pool_w_group[ia]), pool_scale[ia].reshape(1, d), s)
            xf, xb, xp = _proj_ln(mixed, bf(pool_w_out[ia]), xf, g1, b1)
            ia += 1
        elif kind == 1:
            w_qkv = bf(dil_w_qkv[ib])
            os_, ls_ = [], []
            for g in range(len(DIL_CONFIGS)):
                qkv = _qkv_proj(xb, w_qkv, g)
                o, l = _dilated_group(qkv, rel_bias[:, g * DIL_HEADS:(g + 1) * DIL_HEADS], g, bn, s)
                os_.append(o)
                ls_.append(l)
            xf, xb, xp = _attn_out(os_, ls_, bf(dil_w_out[ib]), xf, g1, b1)
            ib += 1
        else:
            v = _conv_mix(xb, bf(conv_w_in[ic]), conv_w[ic], s)
            xf, xb, xp = _proj_ln(v, bf(conv_w_out[ic]), xf, g1, b1)
            ic += 1
        xf, xb = _moe_layer(xf, xb, xp, router_w[i], router_bias[i], expert_w_gu, expert_w_down, i,
                            bf(shared_w_gu[i]), bf(shared_w_down[i]), g2, b2)
    return xf.reshape(bn, s, d)
```

```python
import functools
import math

import jax
import jax.numpy as jnp
from jax import lax
from jax.experimental import pallas as pl
from jax.experimental.pallas import tpu as pltpu

F32 = jnp.float32
BF16 = jnp.bfloat16

DEPTH = 4
D_MODEL = 2048
POOL_WINDOWS = (2, 4, 8, 16)
POOL_GROUP_DIM = D_MODEL // len(POOL_WINDOWS)
POOL_HALO = 16
DIL_CONFIGS = ((128, 1), (512, 4), (2048, 16))
HEAD_DIM = 64
DIL_HEADS = 8
DIL_WIDTH = DIL_HEADS * HEAD_DIM
DIL_BLOCK = 128
N_BUCKETS = 32
MAX_DISTANCE = 2048
CONV_WIDTH = 3
CONV_HALO = 8
N_EXPERTS = 64
TOP_K = 8
N_EXPERT_GROUPS = 8
GROUP_SIZE = N_EXPERTS // N_EXPERT_GROUPS
TOPK_GROUPS = 4
EXPERT_DIM = 128
SHARED_DIM = 256
ROUTED_SCALE = 2.5
ROW_BLOCK = 256
ALPHA = (2.0 * DEPTH) ** 0.25
LN_EPS = 1e-5
NEG = -1e30

LANES = 128
HIGH_HALF = -65536
SLAB = D_MODEL // 2 // LANES + 1
UP_CHUNKS = 8
UP_SHARE = 160
VMEM_LIMIT = 52 * 1024 * 1024

_NT = (((1,), (1,)), ((), ()))


def _dot(a, b):
    return jnp.dot(a, b, preferred_element_type=F32)


def _params(*sem):
    return pltpu.CompilerParams(dimension_semantics=sem, vmem_limit_bytes=VMEM_LIMIT)


def _layer_norm(z, g, b):
    mu = jnp.mean(z, axis=-1, keepdims=True)
    zc = z - mu
    var = jnp.mean(zc * zc, axis=-1, keepdims=True)
    return zc * lax.rsqrt(var + LN_EPS) * g + b


def _silu(v):
    return v / (1.0 + jnp.exp(-v))


def _pack_pair(lo, hi):
    lo = lax.bitcast_convert_type(lo.astype(BF16).astype(F32), jnp.int32)
    hi = lax.bitcast_convert_type(hi.astype(BF16).astype(F32), jnp.int32)
    return (hi & HIGH_HALF) | lax.shift_right_logical(lo, 16)


def _pack_rows(y):
    half = D_MODEL // 2
    return _pack_pair(y[:, :half], y[:, half:])


def _unpack_rows(p):
    lo = lax.bitcast_convert_type(lax.shift_left(p, 16), F32)
    hi = lax.bitcast_convert_type(p & HIGH_HALF, F32)
    return jnp.concatenate([lo, hi], axis=1).astype(BF16)


def _store_ln_outputs(y, of_ref, ob_ref, op_ref):
    of_ref[...] = y
    ob_ref[...] = y.astype(BF16)
    op_ref[...] = _pack_rows(y)


def _ln_out_specs(t, d, tm):
    row = lambda i: (i, 0)
    specs = [pl.BlockSpec((tm, d), row), pl.BlockSpec((tm, d), row), pl.BlockSpec((tm, d // 2), row)]
    shapes = [jax.ShapeDtypeStruct((t, d), F32), jax.ShapeDtypeStruct((t, d), BF16),
              jax.ShapeDtypeStruct((t, d // 2), jnp.int32)]
    return specs, shapes


def _proj_ln_kernel(a_ref, w_ref, x_ref, g_ref, b_ref, of_ref, ob_ref, op_ref):
    z = ALPHA * x_ref[...] + _dot(a_ref[...], w_ref[...])
    _store_ln_outputs(_layer_norm(z, g_ref[...], b_ref[...]), of_ref, ob_ref, op_ref)


def _proj_ln(a, w, x, g, b, tm=256):
    t, k = a.shape
    d = w.shape[1]
    row = lambda i: (i, 0)
    const = lambda i: (0, 0)
    out_specs, out_shape = _ln_out_specs(t, d, tm)
    return pl.pallas_call(
        _proj_ln_kernel,
        grid=(t // tm,),
        in_specs=[pl.BlockSpec((tm, k), row), pl.BlockSpec((k, d), const),
                  pl.BlockSpec((tm, d), row), pl.BlockSpec((1, d), const),
                  pl.BlockSpec((1, d), const)],
        out_specs=out_specs,
        out_shape=out_shape,
        compiler_params=_params("arbitrary"),
        name="proj_ln",
    )(a, w, x, g, b)


def _pool_kernel(x_ref, win_ref, wg_ref, sc_ref, o_ref, carry_ref, *, tm, tiles_per_seq):
    i = pl.program_id(0)
    seq_tile = i % tiles_per_seq

    @pl.when(seq_tile == 0)
    def _():
        carry_ref[...] = jnp.zeros_like(carry_ref)

    xb = x_ref[...]
    pos = lax.broadcasted_iota(jnp.int32, (tm, 1), 0) + seq_tile * tm
    gd = POOL_GROUP_DIM
    for g, w in enumerate(POOL_WINDOWS):
        cols = slice(g * gd, (g + 1) * gd)
        u = _dot(xb, win_ref[:, cols])
        s = jnp.concatenate([carry_ref[g], u], axis=0)
        sh = 1
        while sh < w:
            s = s + pltpu.roll(s, sh, axis=0)
            sh *= 2
        cnt = jnp.minimum(pos + 1, w).astype(F32)
        pooled = s[POOL_HALO:, :] / cnt - u
        carry_ref[g] = u[tm - POOL_HALO:, :]
        mixed = _dot(pooled.astype(BF16), wg_ref[g]) * sc_ref[:, cols]
        o_ref[:, cols] = mixed.astype(BF16)


def _pool_mix(xb, w_in, w_group, scale, seq_len, tm=256):
    t, d = xb.shape
    gd = POOL_GROUP_DIM
    ng = len(POOL_WINDOWS)
    return pl.pallas_call(
        functools.partial(_pool_kernel, tm=tm, tiles_per_seq=seq_len // tm),
        grid=(t // tm,),
        in_specs=[pl.BlockSpec((tm, d), lambda i: (i, 0)),
                  pl.BlockSpec((d, d), lambda i: (0, 0)),
                  pl.BlockSpec((ng, gd, gd), lambda i: (0, 0, 0)),
                  pl.BlockSpec((1, d), lambda i: (0, 0))],
        out_specs=pl.BlockSpec((tm, d), lambda i: (i, 0)),
        out_shape=jax.ShapeDtypeStruct((t, d), BF16),
        scratch_shapes=[pltpu.VMEM((ng, POOL_HALO, gd), F32)],
        compiler_params=_params("arbitrary"),
        name="pool_mix",
    )(xb, w_in, w_group, scale)


def _conv_kernel(x_ref, wb_ref, wc_ref, wh_ref, cw_ref, o_ref, carry_ref, *, tm, tiles_per_seq):
    i = pl.program_id(1)

    @pl.when(i % tiles_per_seq == 0)
    def _():
        carry_ref[...] = jnp.zeros_like(carry_ref)

    xb = x_ref[...]
    b_gate = _dot(xb, wb_ref[...])
    u = _dot(xb, wc_ref[...]) * _dot(xb, wh_ref[...])
    ext = jnp.concatenate([carry_ref[...], u], axis=0)
    y = cw_ref[CONV_WIDTH - 1:CONV_WIDTH, :] * u
    for back in range(1, CONV_WIDTH):
        tap = CONV_WIDTH - 1 - back
        y = y + cw_ref[tap:tap + 1, :] * pltpu.roll(ext, back, axis=0)[CONV_HALO:, :]
    carry_ref[...] = u[tm - CONV_HALO:, :]
    o_ref[...] = (b_gate * y).astype(BF16)


def _conv_mix(xb, w_in, conv_w, seq_len, tm=512, tn=512):
    t, d = xb.shape
    nj = d // tn
    return pl.pallas_call(
        functools.partial(_conv_kernel, tm=tm, tiles_per_seq=seq_len // tm),
        grid=(nj, t // tm),
        in_specs=[pl.BlockSpec((tm, d), lambda j, i: (i, 0)),
                  pl.BlockSpec((d, tn), lambda j, i: (0, j)),
                  pl.BlockSpec((d, tn), lambda j, i: (0, nj + j)),
                  pl.BlockSpec((d, tn), lambda j, i: (0, 2 * nj + j)),
                  pl.BlockSpec((CONV_WIDTH, tn), lambda j, i: (0, j))],
        out_specs=pl.BlockSpec((tm, tn), lambda j, i: (i, j)),
        out_shape=jax.ShapeDtypeStruct((t, d), BF16),
        scratch_shapes=[pltpu.VMEM((CONV_HALO, tn), F32)],
        compiler_params=_params("parallel", "arbitrary"),
        name="conv_mix",
    )(xb, w_in, w_in, w_in, conv_w)


def _t5_bucket(dist):
    max_exact = N_BUCKETS // 2
    is_small = dist < max_exact
    distf = jnp.maximum(dist, 1).astype(F32)
    large = max_exact + (jnp.log(distf / max_exact) / math.log(MAX_DISTANCE / max_exact)
                         * (N_BUCKETS - max_exact)).astype(jnp.int32)
    large = jnp.minimum(large, N_BUCKETS - 1)
    return jnp.where(is_small, dist, large)


def _attn_bias(bias_tab, window, dil, has_prev):
    steps = window // dil
    q_loc = jnp.arange(DIL_BLOCK)[:, None]
    k_loc = jnp.arange(2 * DIL_BLOCK)[None, :]
    rel = q_loc + DIL_BLOCK - k_loc
    band = (rel >= 0) & (rel <= steps)
    bucket = _t5_bucket(jnp.maximum(rel, 0) * dil)
    tab = bias_tab.astype(F32).T[:, :, None, None]
    bias = sum(jnp.where(bucket == b, tab[:, b], 0.0) for b in range(N_BUCKETS))
    bias = jnp.where(band[None], bias, NEG)
    return bias if has_prev else bias[:, :, DIL_BLOCK:]


def _attn_kernel(*refs, has_prev):
    if has_prev:
        q_ref, kc_ref, kp_ref, vc_ref, vp_ref, bias_ref, o_ref, l_ref = refs
    else:
        q_ref, kc_ref, vc_ref, bias_ref, o_ref, l_ref = refs
    n = pl.program_id(1)
    if has_prev:
        col = lax.broadcasted_iota(jnp.int32, (DIL_BLOCK, 2 * DIL_BLOCK), 1)
        keep = (col >= DIL_BLOCK) | (n > 0)
    for sub in range(q_ref.shape[1]):
        q, kc, vc = q_ref[0, sub], kc_ref[0, sub], vc_ref[0, sub]
        if has_prev:
            kp, vp = kp_ref[0, sub], vp_ref[0, sub]
        outs, lses = [], []
        for h in range(DIL_HEADS):
            sl = slice(h * HEAD_DIM, (h + 1) * HEAD_DIM)
            if has_prev:
                kh = jnp.concatenate([kp[:, sl], kc[:, sl]], axis=0)
                vh = jnp.concatenate([vp[:, sl], vc[:, sl]], axis=0)
            else:
                kh, vh = kc[:, sl], vc[:, sl]
            s = lax.dot_general(q[:, sl], kh, _NT, preferred_element_type=F32) * (HEAD_DIM ** -0.5) + bias_ref[h]
            if has_prev:
                s = jnp.where(keep, s, NEG)
            m = jnp.max(s, axis=-1, keepdims=True)
            p = jnp.exp(s - m)
            l = jnp.sum(p, axis=-1, keepdims=True)
            outs.append(_dot(p.astype(BF16), vh) / l)
            lses.append(jnp.broadcast_to(m + jnp.log(l), (DIL_BLOCK, HEAD_DIM)))
        o_ref[0, sub] = jnp.concatenate(outs, axis=1)
        l_ref[0, sub] = jnp.concatenate(lses, axis=1)


def _subseq_tile(dil, tm):
    span = DIL_BLOCK * dil
    if tm >= span:
        block = (tm // span, dil, DIL_BLOCK)
        imap = lambda i: (i, 0, 0, 0)
    else:
        per_span = span // tm
        block = (1, dil, tm // dil)
        imap = lambda i: (i // per_span, 0, i % per_span, 0)
    runs = [(sp, r, sp * span + r) for sp in range(block[0]) for r in range(dil)]
    return block, imap, runs


def _qkv_kernel(x_ref, w_ref, o_ref, *, dil):
    x = x_ref[...]
    if dil > 1:
        tm, n = x.shape[0], o_ref.shape[2]
        out_row = lax.broadcasted_iota(jnp.int32, (tm, tm), 0)
        in_row = lax.broadcasted_iota(jnp.int32, (tm, tm), 1)
        pick = in_row == (out_row % n) * dil + out_row // n
        x = _dot(pick.astype(BF16), x).astype(BF16)
    o_ref[...] = _dot(x, w_ref[...]).astype(o_ref.dtype).reshape(o_ref.shape)


def _qkv_proj(xb, w_qkv, g, tm=512):
    t, d = xb.shape
    _, dil = DIL_CONFIGS[g]
    c = 3 * DIL_WIDTH
    block, imap, _ = _subseq_tile(dil, tm)
    assert dil == 1 or block[0] == 1, "the in-kernel row reorder handles one span (or part of one) per tile"
    return pl.pallas_call(
        functools.partial(_qkv_kernel, dil=dil),
        grid=(t // tm,),
        in_specs=[pl.BlockSpec((tm, d), lambda i: (i, 0)),
                  pl.BlockSpec((d, c), lambda i: (0, g))],
        out_specs=pl.BlockSpec(block + (c,), imap),
        out_shape=jax.ShapeDtypeStruct((t // (DIL_BLOCK * dil), dil, DIL_BLOCK, c), BF16),
        compiler_params=_params("arbitrary"),
        name=f"qkv_proj_g{g}",
    )(xb, w_qkv)


def _dilated_group(qkv, bias_tab, g, batch, seq_len):
    window, dil = DIL_CONFIGS[g]
    nb = seq_len // (DIL_BLOCK * dil)
    has_prev = nb > 1
    subs = 1 if has_prev else min(2, dil)
    blk = (1, subs, DIL_BLOCK, DIL_WIDTH)

    def cur(kind):
        return pl.BlockSpec(blk, lambda b, n, r: (b * nb + n, r, 0, kind))

    def prev(kind):
        return pl.BlockSpec(blk, lambda b, n, r: (b * nb + jnp.maximum(n - 1, 0), r, 0, kind))

    bias = _attn_bias(bias_tab, window, dil, has_prev)
    bias_spec = pl.BlockSpec(bias.shape, lambda b, n, r: (0, 0, 0))
    if has_prev:
        in_specs = [cur(0), cur(1), prev(1), cur(2), prev(2), bias_spec]
        args = (qkv,) * 5 + (bias,)
    else:
        in_specs = [cur(0), cur(1), cur(2), bias_spec]
        args = (qkv,) * 3 + (bias,)
    out_spec = pl.BlockSpec(blk, lambda b, n, r: (b * nb + n, r, 0, 0))
    out_sds = jax.ShapeDtypeStruct(qkv.shape[:3] + (DIL_WIDTH,), F32)
    return pl.pallas_call(
        functools.partial(_attn_kernel, has_prev=has_prev),
        grid=(batch, nb, dil // subs),
        in_specs=in_specs,
        out_specs=[out_spec, out_spec],
        out_shape=[out_sds, out_sds],
        compiler_params=_params("parallel", "arbitrary", "arbitrary"),
        name=f"dilated_attn_g{g}",
    )(*args)


def _attn_out_kernel(*refs, tm, tiles):
    n_g = len(tiles)
    o_refs, l_refs = refs[:n_g], refs[n_g:2 * n_g]
    w_ref, x_ref, g_ref, b_ref, of_ref, ob_ref, op_ref = refs[2 * n_g:2 * n_g + 7]
    scratch = list(refs[2 * n_g + 7:])

    def token_order(ref, dil, runs):
        if dil == 1:
            return ref[...].reshape(tm, ref.shape[-1])
        buf = scratch.pop(0)
        n = ref.shape[2]
        for sp, r, first in runs:
            rows = ref[sp, r]
            for c in range(buf.shape[0]):
                buf[c, pl.ds(first, n, stride=dil), :] = rows[:, c * LANES:(c + 1) * LANES]
        return jnp.concatenate([buf[c] for c in range(buf.shape[0])], axis=1)

    os_ = [token_order(ref, dil, runs) for ref, (dil, runs) in zip(o_refs, tiles)]
    ls_ = [token_order(ref, dil, runs) for ref, (dil, runs) in zip(l_refs, tiles)]
    m = functools.reduce(jnp.maximum, ls_)
    es = [jnp.exp(a - m) for a in ls_]
    o = sum(e * ov for e, ov in zip(es, os_)) / sum(es)
    z = ALPHA * x_ref[...] + _dot(o.astype(BF16), w_ref[...])
    _store_ln_outputs(_layer_norm(z, g_ref[...], b_ref[...]), of_ref, ob_ref, op_ref)


def _attn_out(os_, ls_, w, x, g, b, tm=256):
    t, d = x.shape
    k = w.shape[0]
    row = lambda i: (i, 0)
    const = lambda i: (0, 0)
    specs, tiles = [], []
    for _, dil in DIL_CONFIGS:
        block, imap, runs = _subseq_tile(dil, tm)
        specs.append(pl.BlockSpec(block + (k,), imap))
        tiles.append((dil, runs))
    n_scratch = 2 * sum(1 for dil, _ in tiles if dil > 1)
    out_specs, out_shape = _ln_out_specs(t, d, tm)
    return pl.pallas_call(
        functools.partial(_attn_out_kernel, tm=tm, tiles=tiles),
        grid=(t // tm,),
        in_specs=specs * 2 + [pl.BlockSpec((k, d), const), pl.BlockSpec((tm, d), row),
                              pl.BlockSpec((1, d), const), pl.BlockSpec((1, d), const)],
        out_specs=out_specs,
        out_shape=out_shape,
        scratch_shapes=[pltpu.VMEM((k // LANES, tm, LANES), F32)] * n_scratch,
        compiler_params=_params("arbitrary"),
        name="attn_out_ln",
    )(*os_, *ls_, w, x, g, b)


def _router_kernel(x_ref, wt_ref, b_ref, eidx_ref, gate_ref, pos_ref, cnt_ref, base_ref, *, tm):
    @pl.when(pl.program_id(0) == 0)
    def _():
        base_ref[...] = jnp.zeros_like(base_ref)

    x = x_ref[...]
    w = wt_ref[...]
    xh = x.astype(BF16)
    xl = (x - xh.astype(F32)).astype(BF16)
    wh = w.astype(BF16)
    wl = (w - wh.astype(F32)).astype(BF16)
    dg = functools.partial(lax.dot_general, dimension_numbers=_NT, preferred_element_type=F32)
    logits = dg(wh, xh) + (dg(wh, xl) + dg(wl, xh))
    scores = 1.0 / (1.0 + jnp.exp(-logits))
    choice = scores + b_ref[...]

    iota_g = lax.broadcasted_iota(jnp.int32, (GROUP_SIZE, tm), 0)
    gs = []
    for gi in range(N_EXPERT_GROUPS):
        cg = choice[gi * GROUP_SIZE:(gi + 1) * GROUP_SIZE, :]
        t1 = jnp.max(cg, axis=0, keepdims=True)
        i1 = jnp.min(jnp.where(cg == t1, iota_g, GROUP_SIZE), axis=0, keepdims=True)
        t2 = jnp.max(jnp.where(iota_g == i1, -jnp.inf, cg), axis=0, keepdims=True)
        gs.append(t1 + t2)
    masked = []
    for gi in range(N_EXPERT_GROUPS):
        rank = jnp.zeros((1, tm), jnp.int32)
        for gj in range(N_EXPERT_GROUPS):
            if gj != gi:
                beats = (gs[gj] >= gs[gi]) if gj < gi else (gs[gj] > gs[gi])
                rank = rank + beats.astype(jnp.int32)
        masked.append(jnp.where(rank < TOPK_GROUPS, choice[gi * GROUP_SIZE:(gi + 1) * GROUP_SIZE, :], -jnp.inf))
    c = jnp.concatenate(masked, axis=0)

    iota_e = lax.broadcasted_iota(jnp.int32, (N_EXPERTS, tm), 0)
    picks, gates = [], []
    sel = jnp.zeros((N_EXPERTS, tm), F32)
    for _ in range(TOP_K):
        m = jnp.max(c, axis=0, keepdims=True)
        idx = jnp.min(jnp.where(c == m, iota_e, N_EXPERTS), axis=0, keepdims=True)
        hit = iota_e == idx
        picks.append(hit)
        gates.append(jnp.sum(jnp.where(hit, scores, 0.0), axis=0, keepdims=True))
        sel = jnp.where(hit, 1.0, sel)
        c = jnp.where(hit, -jnp.inf, c)
    gsum = gates[0]
    for gk in gates[1:]:
        gsum = gsum + gk

    before = (lax.broadcasted_iota(jnp.int32, (tm, tm), 0) < lax.broadcasted_iota(jnp.int32, (tm, tm), 1))
    rank_in_tile = _dot(sel.astype(BF16), before.astype(BF16))
    base = base_ref[...]
    posfull = base[:, 0:1] + rank_in_tile
    efull = iota_e.astype(F32)
    for k in range(TOP_K):
        hit = picks[k]
        eidx_ref[k:k + 1, :] = jnp.sum(jnp.where(hit, efull, 0.0), axis=0, keepdims=True).astype(jnp.int32)
        pos_ref[k:k + 1, :] = jnp.sum(jnp.where(hit, posfull, 0.0), axis=0, keepdims=True).astype(jnp.int32)
        gate_ref[k:k + 1, :] = gates[k] / gsum * ROUTED_SCALE
    new_base = base + jnp.sum(sel, axis=1, keepdims=True)
    base_ref[...] = new_base
    cnt_ref[...] = new_base


def _route(x, router_wt, router_b, tm=512):
    t, d = x.shape
    tok = pl.BlockSpec((TOP_K, tm), lambda i: (0, i))
    cnt = pl.BlockSpec((N_EXPERTS, LANES), lambda i: (0, 0))
    return pl.pallas_call(
        functools.partial(_router_kernel, tm=tm),
        grid=(t // tm,),
        in_specs=[pl.BlockSpec((tm, d), lambda i: (i, 0)),
                  pl.BlockSpec((N_EXPERTS, d), lambda i: (0, 0)),
                  pl.BlockSpec((N_EXPERTS, 1), lambda i: (0, 0))],
        out_specs=[tok, tok, tok, cnt],
        out_shape=[jax.ShapeDtypeStruct((TOP_K, t), jnp.int32), jax.ShapeDtypeStruct((TOP_K, t), F32),
                   jax.ShapeDtypeStruct((TOP_K, t), jnp.int32), jax.ShapeDtypeStruct((N_EXPERTS, LANES), F32)],
        scratch_shapes=[pltpu.VMEM((N_EXPERTS, LANES), F32)],
        compiler_params=_params("arbitrary"),
        name="router",
    )(x, router_wt, router_b)


def _row(ref, r):
    return ref.at[pl.ds(r, 1)]


def _slab(ref, r):
    return ref.at[pl.ds(r * SLAB, SLAB)]


def _dest_kernel(pstart_ref, eidx_ref, pos_ref, dest_ref):
    e = eidx_ref[...]
    acc = pos_ref[...]
    for k in range(N_EXPERTS):
        acc = acc + jnp.where(e == k, pstart_ref[k], 0)
    dest_ref[...] = acc


def _dest_rows(pstart, eidx, pos):
    full = pl.BlockSpec(eidx.shape, lambda i, ps: (0, 0))
    return pl.pallas_call(
        _dest_kernel,
        grid_spec=pltpu.PrefetchScalarGridSpec(num_scalar_prefetch=1, grid=(1,), in_specs=[full, full],
                                               out_specs=full),
        out_shape=jax.ShapeDtypeStruct(eidx.shape, jnp.int32),
        compiler_params=_params("arbitrary"),
        name="dest_rows",
    )(pstart, eidx, pos)


def _dispatch_kernel(pad_lo_ref, pad_hi_ref, dest_ref, x_ref, gate_ref, xs_hbm, src_ref, zero_ref, sem, pad_sem,
                     *, tm, n_blk, n_tok, n_tiles):
    i = pl.program_id(0)

    def zero_fill(wait):
        def each(lo, hi, copy):
            def body(r, c):
                copy(r).wait() if wait else copy(r).start()
                return c

            lax.fori_loop(lo, hi, body, 0)

        def per_expert(e, carry):
            each(pad_lo_ref[e], pad_hi_ref[e],
                 lambda r: pltpu.make_async_copy(_slab(zero_ref, 0), _slab(xs_hbm, r), pad_sem))
            return carry

        blk_rows = ROW_BLOCK * SLAB
        lax.fori_loop(0, N_EXPERTS, per_expert, 0)
        each(pad_hi_ref[N_EXPERTS - 1] // ROW_BLOCK, n_blk,
             lambda blk: pltpu.make_async_copy(
                 zero_ref, xs_hbm.at[pl.ds(pl.multiple_of(blk * blk_rows, blk_rows), blk_rows)], pad_sem))

    @pl.when(i == 0)
    def _():
        zero_ref[...] = jnp.zeros_like(zero_ref)
        zero_fill(wait=False)

    x = x_ref[...]
    gate_bits = lax.bitcast_convert_type(gate_ref[...], jnp.int32)
    lane = lax.broadcasted_iota(jnp.int32, (tm, LANES), 1)
    token = lax.broadcasted_iota(jnp.int32, (tm, LANES), 0) + i * tm

    def retire(par):
        for k in range(TOP_K):
            pltpu.make_async_copy(src_ref.at[par, k], xs_hbm.at[pl.ds(0, tm * SLAB)], sem.at[par]).wait()

    for par in range(2):
        @pl.when(i % 2 == par)
        def _(par=par):
            for k in range(TOP_K):
                for c in range(SLAB - 1):
                    src_ref[par, k, pl.ds(c, tm, stride=SLAB), :] = x[:, c * LANES:(c + 1) * LANES]
                src_ref[par, k, pl.ds(SLAB - 1, tm, stride=SLAB), :] = jnp.where(
                    lane == 0, token + k * n_tok, jnp.where(lane == 1, gate_bits[:, k:k + 1], 0))

            def per_token(j, carry):
                for k in range(TOP_K):
                    pltpu.make_async_copy(_slab(src_ref.at[par, k], j),
                                          _slab(xs_hbm, dest_ref[0, 0, j * TOP_K + k]), sem.at[par]).start()
                return carry

            lax.fori_loop(0, tm, per_token, 0)

            @pl.when(i >= 1)
            def _():
                retire(1 - par)

            @pl.when(i == n_tiles - 1)
            def _():
                retire(par)

    @pl.when(i == 0)
    def _():
        zero_fill(wait=True)


def _dispatch(xp, dest_tok, gate_tok, pad_lo, pad_hi, n_rows, tm=256):
    t, half = xp.shape
    assert half == (SLAB - 1) * LANES
    n_tiles = t // tm
    dest3 = dest_tok.reshape(n_tiles, 1, tm * TOP_K)
    grid_spec = pltpu.PrefetchScalarGridSpec(
        num_scalar_prefetch=2,
        grid=(n_tiles,),
        in_specs=[pl.BlockSpec((1, 1, tm * TOP_K), lambda i, lo, hi: (i, 0, 0), memory_space=pltpu.SMEM),
                  pl.BlockSpec((tm, half), lambda i, lo, hi: (i, 0)),
                  pl.BlockSpec((tm, TOP_K), lambda i, lo, hi: (i, 0))],
        out_specs=pl.BlockSpec(memory_space=pl.ANY),
        scratch_shapes=[pltpu.VMEM((2, TOP_K, tm * SLAB, LANES), jnp.int32),
                        pltpu.VMEM((ROW_BLOCK * SLAB, LANES), jnp.int32),
                        pltpu.SemaphoreType.DMA((2,)), pltpu.SemaphoreType.DMA(())],
    )
    return pl.pallas_call(
        functools.partial(_dispatch_kernel, tm=tm, n_blk=n_rows // ROW_BLOCK, n_tok=t, n_tiles=n_tiles),
        grid_spec=grid_spec,
        out_shape=jax.ShapeDtypeStruct((n_rows * SLAB, LANES), jnp.int32),
        compiler_params=_params("arbitrary"),
        name="dispatch",
    )(pad_lo, pad_hi, dest3, xp, gate_tok)


def _expert_kernel(blk_e_ref, nused_ref, nvalid_ref, xs_ref, wgu_ref, wd_ref, out_hbm,
                   wgu_bf, wd_bf, ybuf, idv, ids_smem, sem, id_sem, *, n_blk, n_slots):
    i = pl.program_id(0)
    nused = nused_ref[0]
    s = i % 2
    half = ybuf.shape[2]

    def rows_copy(parity):
        return pltpu.make_async_copy(ybuf.at[parity], out_hbm.at[pl.ds(0, ROW_BLOCK)], sem.at[parity])

    def spill_init(parity):
        return pltpu.make_async_copy(ybuf.at[parity], out_hbm.at[pl.ds(n_slots + parity * ROW_BLOCK, ROW_BLOCK)],
                                     sem.at[parity])

    def compute(par, retire_first, scatter_par=None):
        @pl.when((i == 0) | (blk_e_ref[i] != blk_e_ref[jnp.maximum(i - 1, 0)]))
        def _():
            wgu_bf[...] = wgu_ref[0, 0].astype(BF16)
            wd_bf[...] = wd_ref[0, 0].astype(BF16)

        if scatter_par is not None:
            ids_ready(scatter_par)

        tiles = [xs_ref[pl.ds(c, ROW_BLOCK, stride=SLAB), :] for c in range(SLAB)]
        meta = tiles[SLAB - 1]
        row = lax.broadcasted_iota(jnp.int32, meta.shape, 0)
        dst = jnp.where(row < nvalid_ref[i], meta, n_slots + par * ROW_BLOCK + row)
        idv[par] = jnp.transpose(dst)[0:idv.shape[1], :]
        pltpu.make_async_copy(idv.at[par], ids_smem.at[par], id_sem.at[par]).start()

        gate = lax.bitcast_convert_type(meta[:, 1:2], F32)
        xs = _unpack_rows(jnp.concatenate(tiles[:SLAB - 1], axis=1))
        up_rows = UP_SHARE // UP_CHUNKS
        d = xs.shape[1]
        step = d // UP_CHUNKS
        hgu = None
        for c in range(UP_CHUNKS):
            part = _dot(xs[:, c * step:(c + 1) * step], wgu_bf[c * step:(c + 1) * step, :])
            hgu = part if hgu is None else hgu + part
            if scatter_par is not None:
                scatter(scatter_par, c * up_rows, up_rows)
        h = (_silu(hgu[:, :EXPERT_DIM]) * hgu[:, EXPERT_DIM:]).astype(BF16)
        if scatter_par is not None:
            scatter(scatter_par, UP_SHARE, ROW_BLOCK - UP_SHARE)
        y = _pack_rows(_dot(h, wd_bf[...]) * gate)
        if retire_first:
            rows_copy(par).wait()
        ybuf[par] = y

    def ids_ready(par):
        pltpu.make_async_copy(idv.at[par], ids_smem.at[par], id_sem.at[par]).wait()

    def scatter(par, first, count):
        for r in range(first, first + count):
            pltpu.make_async_copy(ybuf.at[par, pl.ds(r, 1)], _row(out_hbm, ids_smem[par, 0, r]),
                                  sem.at[par]).start()

    @pl.when(i == 0)
    def _():
        ybuf[...] = jnp.zeros_like(ybuf)
        spill_init(0).start()
        spill_init(0).wait()
        spill_init(1).start()
        compute(0, retire_first=False)

    for par in range(2):
        @pl.when((i >= 1) & (i < nused) & (s == par))
        def _(par=par):
            compute(par, retire_first=True, scatter_par=1 - par)

        @pl.when((i >= 1) & (i == nused) & (s == par))
        def _(par=par):
            ids_ready(1 - par)
            scatter(1 - par, 0, ROW_BLOCK)

    @pl.when((i >= nused) & (i >= 2) & (i - 2 < nused))
    def _():
        rows_copy(s).wait()

    @pl.when((i == n_blk - 1) & (i - 1 < nused))
    def _():
        rows_copy(1 - s).wait()


def _experts(xs, blk_e, nused, nvalid, w_gu, w_down, layer, n_slots):
    half = (SLAB - 1) * LANES
    n_blk = xs.shape[0] // (ROW_BLOCK * SLAB)
    d, f2 = w_gu.shape[2], w_gu.shape[3]
    blk = lambda i, be, nu, nv: (jnp.minimum(i, nu[0] - 1), 0)
    wblk = lambda i, be, nu, nv: (layer, be[jnp.minimum(i, nu[0] - 1)], 0, 0)
    grid_spec = pltpu.PrefetchScalarGridSpec(
        num_scalar_prefetch=3,
        grid=(n_blk,),
        in_specs=[pl.BlockSpec((ROW_BLOCK * SLAB, LANES), blk),
                  pl.BlockSpec((1, 1, d, f2), wblk),
                  pl.BlockSpec((1, 1, f2 // 2, d), wblk)],
        out_specs=pl.BlockSpec(memory_space=pl.ANY),
        scratch_shapes=[pltpu.VMEM((d, f2), BF16), pltpu.VMEM((f2 // 2, d), BF16),
                        pltpu.VMEM((2, ROW_BLOCK, half), jnp.int32),
                        pltpu.VMEM((2, 8, ROW_BLOCK), jnp.int32),
                        pltpu.SMEM((2, 8, ROW_BLOCK), jnp.int32),
                        pltpu.SemaphoreType.DMA((2,)), pltpu.SemaphoreType.DMA((2,))],
    )
    return pl.pallas_call(
        functools.partial(_expert_kernel, n_blk=n_blk, n_slots=n_slots),
        grid_spec=grid_spec,
        out_shape=jax.ShapeDtypeStruct((n_slots + 2 * ROW_BLOCK, half), jnp.int32),
        compiler_params=_params("arbitrary"),
        name="experts",
    )(blk_e, nused, nvalid, xs, w_gu, w_down)


def _combine_kernel(*refs):
    y_refs = refs[:TOP_K]
    xf_ref, xb_ref, sgu_ref, sd_ref, g_ref, b_ref, of_ref, ob_ref = refs[TOP_K:]
    lo = hi = None
    for y_ref in y_refs:
        p = y_ref[...]
        lo_k = lax.bitcast_convert_type(lax.shift_left(p, 16), F32)
        hi_k = lax.bitcast_convert_type(p & HIGH_HALF, F32)
        lo = lo_k if lo is None else lo + lo_k
        hi = hi_k if hi is None else hi + hi_k
    routed = jnp.concatenate([lo, hi], axis=1)

    sgu = _dot(xb_ref[...], sgu_ref[...])
    sh = _silu(sgu[:, :SHARED_DIM]) * sgu[:, SHARED_DIM:]
    shared = _dot(sh.astype(BF16), sd_ref[...])
    z = ALPHA * xf_ref[...] + (routed + shared)
    y = _layer_norm(z, g_ref[...], b_ref[...])
    of_ref[...] = y
    ob_ref[...] = y.astype(BF16)


def _combine(y, xf, xb, sw_gu, sw_down, g, b, tm=256):
    t, d = xf.shape
    n_tiles = t // tm
    row = lambda i: (i, 0)
    const = lambda i: (0, 0)
    y_specs = [pl.BlockSpec((tm, d // 2), functools.partial(lambda i, k: (k * n_tiles + i, 0), k=k))
               for k in range(TOP_K)]
    return pl.pallas_call(
        _combine_kernel,
        grid=(n_tiles,),
        in_specs=y_specs + [pl.BlockSpec((tm, d), row), pl.BlockSpec((tm, d), row),
                            pl.BlockSpec(sw_gu.shape, const), pl.BlockSpec(sw_down.shape, const),
                            pl.BlockSpec((1, d), const), pl.BlockSpec((1, d), const)],
        out_specs=[pl.BlockSpec((tm, d), row), pl.BlockSpec((tm, d), row)],
        out_shape=[jax.ShapeDtypeStruct((t, d), F32), jax.ShapeDtypeStruct((t, d), BF16)],
        compiler_params=_params("arbitrary"),
        name="combine_ln",
    )(*([y] * TOP_K), xf, xb, sw_gu, sw_down, g, b)


def _moe_layer(xf, xb, xp, router_w, router_b, w_gu, w_down, layer, sw_gu, sw_down, g, b):
    t, d = xf.shape
    eidx, gate, pos, cnt = _route(xf, router_w.T, router_b.reshape(N_EXPERTS, 1))

    sizes = cnt[:, 0].astype(jnp.int32)
    padded = (sizes + ROW_BLOCK - 1) // ROW_BLOCK * ROW_BLOCK
    pend = jnp.cumsum(padded)
    pstart = pend - padded
    tk = t * TOP_K
    n_rows = -(-tk // ROW_BLOCK) * ROW_BLOCK + N_EXPERTS * ROW_BLOCK
    blk_start = jnp.arange(n_rows // ROW_BLOCK, dtype=jnp.int32) * ROW_BLOCK
    blk_e = jnp.minimum(jnp.sum(pend[None, :] <= blk_start[:, None], axis=1), N_EXPERTS - 1).astype(jnp.int32)
    nused = (pend[-1:] // ROW_BLOCK).astype(jnp.int32)

    real_end = pstart + sizes
    nvalid = jnp.clip(real_end[blk_e] - blk_start, 0, ROW_BLOCK).astype(jnp.int32)

    dest_tok = _dest_rows(pstart, eidx, pos).T
    xs = _dispatch(xp, dest_tok, gate.T, real_end, pend, n_rows)
    y = _experts(xs, blk_e, nused, nvalid, w_gu, w_down, layer, tk)
    return _combine(y, xf, xb, sw_gu, sw_down, g, b)


def kernel(x, rel_bias, pool_w_in, pool_w_group, pool_scale, pool_w_out, dil_w_qkv, dil_w_out,
           conv_w_in, conv_w, conv_w_out, ln_gain, ln_bias, router_w, router_bias,
           expert_w_gu, expert_w_down, shared_w_gu, shared_w_down):
    bn, s, d = x.shape
    t = bn * s
    xf = x.reshape(t, d)
    xb = xf.astype(BF16)
    bf = lambda a: a.astype(BF16)
    ia = ib = ic = 0
    for i in range(DEPTH):
        kind = i % 3
        g1, b1 = ln_gain[i, 0].reshape(1, d), ln_bias[i, 0].reshape(1, d)
        g2, b2 = ln_gain[i, 1].reshape(1, d), ln_bias[i, 1].reshape(1, d)
        if kind == 0:
            mixed = _pool_mix(xb, bf(pool_w_in[ia]), bf(pool_w_group[ia]), pool_scale[ia].reshape(1, d), s)
            xf, xb, xp = _proj_ln(mixed, bf(pool_w_out[ia]), xf, g1, b1)
            ia += 1
        elif kind == 1:
            w_qkv = bf(dil_w_qkv[ib])
            os_, ls_ = [], []
            for g in range(len(DIL_CONFIGS)):
                qkv = _qkv_proj(xb, w_qkv, g)
                o, l = _dilated_group(qkv, rel_bias[:, g * DIL_HEADS:(g + 1) * DIL_HEADS], g, bn, s)
                os_.append(o)
                ls_.append(l)
            xf, xb, xp = _attn_out(os_, ls_, bf(dil_w_out[ib]), xf, g1, b1)
            ib += 1
        else:
            v = _conv_mix(xb, bf(conv_w_in[ic]), conv_w[ic], s)
            xf, xb, xp = _proj_ln(v, bf(conv_w_out[ic]), xf, g1, b1)
            ic += 1
        xf, xb = _moe_layer(xf, xb, xp, router_w[i], router_bias[i], expert_w_gu, expert_w_down, i,
                            bf(shared_w_gu[i]), bf(shared_w_down[i]), g2, b2)
    return xf.reshape(bn, s, d)
```
